```python
import math
import jax, jax.numpy as jnp
from jax import lax
import numpy as np

D_MODEL = 2048
BATCH = 4
SEQ = 2048
DEPTH = 2
DEC_BATCH = 128
DEC_SEQ = 8
PAST_LEN = 16384
PAGE_SIZE = 128

N_MIXERS = 2
N_GMLP_LAYERS = (DEPTH + 1) // 2
N_SSD_LAYERS = DEPTH // 2

GMLP_WIDTH = D_MODEL
GMLP_GROUPS = 16
GMLP_GROUP_DIM = GMLP_WIDTH // GMLP_GROUPS
GMLP_CHUNK = 128

SSD_EXPAND = 2
SSD_INNER = SSD_EXPAND * D_MODEL
SSD_HEAD_DIM = 64
SSD_HEADS = SSD_INNER // SSD_HEAD_DIM
SSD_STATE = 128
SSD_GROUPS = 8
SSD_CONV = 4
SSD_CONV_DIM = SSD_INNER + 2 * SSD_GROUPS * SSD_STATE
SSD_IN_DIM = SSD_INNER + SSD_CONV_DIM + SSD_HEADS
SSD_CHUNK = 128
DT_MIN = 0.001
DT_MAX = 0.1

FFN_HIDDEN = int(math.ceil(8 * D_MODEL / 3 / 256) * 256)

NORM_EPS = 1e-6
LN_EPS = 1e-5

kernel_name = "hybrid_gmlp_ssd_adaln_step"


def _rmsnorm(x, w):
    xf = x.astype(jnp.float32)
    y = xf * lax.rsqrt(jnp.mean(xf * xf, axis=-1, keepdims=True) + NORM_EPS)
    return y.astype(x.dtype) * w


def _layernorm(x, w, b):
    xf = x.astype(jnp.float32)
    mu = jnp.mean(xf, axis=-1, keepdims=True)
    xc = xf - mu
    y = xc * lax.rsqrt(jnp.mean(xc * xc, axis=-1, keepdims=True) + LN_EPS)
    return y.astype(x.dtype) * w + b


def _group_rmsnorm(y, w):
    yg = y.reshape(y.shape[:-1] + (SSD_GROUPS, SSD_INNER // SSD_GROUPS))
    yg = yg * lax.rsqrt(jnp.mean(yg * yg, axis=-1, keepdims=True) + NORM_EPS)
    return yg.reshape(y.shape) * w.astype(jnp.float32)


def _chunk_gmlp_mixer(h, w_in, b_in, ln_w, ln_b, w_s, b_s, w_out):
    bt, seq_len, _ = h.shape
    z = jax.nn.gelu(h @ w_in + b_in, approximate=False)
    u, v = jnp.split(z, 2, axis=-1)
    v = _layernorm(v, ln_w, ln_b)
    t = min(GMLP_CHUNK, seq_len)
    n_chunks = seq_len // t
    causal = jnp.tril(jnp.ones((t, t), dtype=bool))
    ws = jnp.where(causal, w_s[:, :t, :t], 0)
    vc = v.reshape(bt, n_chunks, t, GMLP_GROUPS, GMLP_GROUP_DIM)
    s = jnp.einsum('gts,bcsgd->bctgd', ws, vc) + jnp.transpose(b_s[:, :t])[None, None, :, :, None]
    out = (u * s.reshape(bt, seq_len, GMLP_WIDTH)) @ w_out
    return out, v[:, seq_len - t:]


def _segsum(a):
    t = a.shape[-1]
    cs = jnp.cumsum(a, axis=-1)
    diff = cs[..., :, None] - cs[..., None, :]
    return jnp.where(jnp.tril(jnp.ones((t, t), dtype=bool)), diff, -jnp.inf)


def _ssd_scan(x, a, b, c, h0):
    bt, seq_len, n_heads, p = x.shape
    g, n = b.shape[2], b.shape[3]
    r = n_heads // g
    t = min(SSD_CHUNK, seq_len)
    nc = seq_len // t
    xc = x.reshape(bt, nc, t, g, r, p)
    ac = jnp.transpose(a.reshape(bt, nc, t, g, r), (0, 3, 4, 1, 2))
    bc = b.reshape(bt, nc, t, g, n)
    cc = c.reshape(bt, nc, t, g, n)
    a_cs = jnp.cumsum(ac, axis=-1)
    decay = jnp.exp(_segsum(ac))
    cb = jnp.einsum('bcign,bcjgn->bcgij', cc, bc)
    wmix = jnp.einsum('bcgij,bgrcij->bcgrij', cb, decay)
    y_diag = jnp.einsum('bcgrij,bcjgrp->bcigrp', wmix, xc)
    decay_to_end = jnp.exp(a_cs[..., -1:] - a_cs)
    states = jnp.einsum('bcjgn,bgrcj,bcjgrp->bcgrpn', bc, decay_to_end, xc)
    h0g = h0.reshape(bt, g, r, p, n)[:, None]
    states = jnp.concatenate([h0g, states], axis=1)
    chunk_tot = jnp.pad(a_cs[..., -1], ((0, 0), (0, 0), (0, 0), (1, 0)))
    decay_chunk = jnp.exp(_segsum(chunk_tot))
    states = jnp.einsum('bgrzc,bcgrpn->bzgrpn', decay_chunk, states)
    h_final = states[:, -1].reshape(bt, n_heads, p, n)
    states = states[:, :-1]
    y_off = jnp.einsum('bcign,bcgrpn,bgrci->bcigrp', cc, states, jnp.exp(a_cs))
    y = (y_diag + y_off).reshape(bt, seq_len, n_heads, p)
    return y, h_final


def _causal_dwconv(xbc, conv_state, w, bias):
    xp = jnp.concatenate([conv_state.astype(xbc.dtype), xbc], axis=1)
    out = lax.conv_general_dilated(xp, w[:, None, :].astype(xbc.dtype), (1,), 'VALID',
                                   dimension_numbers=('NWC', 'WIO', 'NWC'),
                                   feature_group_count=xbc.shape[-1])
    return out + bias, xp[:, xp.shape[1] - (SSD_CONV - 1):]


def _ssd_mixer(h, ssm0, conv0, w_in, conv_w, conv_b, dt_bias, a_log, d_skip, norm_w, w_out):
    bt, seq_len, _ = h.shape
    f32 = jnp.float32
    zxbcdt = h @ w_in
    z, xbc, dt = jnp.split(zxbcdt, [SSD_INNER, SSD_INNER + SSD_CONV_DIM], axis=-1)
    xbc, conv_new = _causal_dwconv(xbc, conv0, conv_w, conv_b)
    xbc = jax.nn.silu(xbc)
    xs, bmat, cmat = jnp.split(xbc, [SSD_INNER, SSD_INNER + SSD_GROUPS * SSD_STATE], axis=-1)
    dt = jax.nn.softplus(dt.astype(f32) + dt_bias.astype(f32))
    a = -jnp.exp(a_log.astype(f32))
    xh = xs.reshape(bt, seq_len, SSD_HEADS, SSD_HEAD_DIM).astype(f32)
    y, ssm_new = _ssd_scan(xh * dt[..., None], dt * a,
                           bmat.reshape(bt, seq_len, SSD_GROUPS, SSD_STATE).astype(f32),
                           cmat.reshape(bt, seq_len, SSD_GROUPS, SSD_STATE).astype(f32),
                           ssm0.astype(f32))
    y = y + d_skip.astype(f32)[:, None] * xh
    y = y.reshape(bt, seq_len, SSD_INNER) * jax.nn.silu(z.astype(f32))
    y = _group_rmsnorm(y, norm_w)
    out = y.astype(h.dtype) @ w_out
    return out, ssm_new.astype(ssm0.dtype), conv_new.astype(conv0.dtype)


def _swiglu(h, w_in, w_out):
    gate, up = jnp.split(h @ w_in, 2, axis=-1)
    return (jax.nn.silu(gate) * up) @ w_out


def _trunk(x, c, ssm0, conv0, mod_w, mod_b, norm_mix_w, norm_ffn_w,
           a_w_in, a_b_in, a_ln_w, a_ln_b, a_w_s, a_b_s, a_w_out,
           b_w_in, b_conv_w, b_conv_b, b_dt_bias, b_a_log, b_d, b_norm_w, b_w_out,
           f_w_in, f_w_out, final_norm_w):
    v_rows, ssm_states, conv_states = [], [], []
    for i in range(DEPTH):
        mod = (jax.nn.silu(c) @ mod_w[i] + mod_b[i])[:, None, :]
        shift_m, scale_m, gate_m, shift_f, scale_f, gate_f = jnp.split(mod, 6, axis=-1)
        h = _rmsnorm(x, norm_mix_w[i]) * (1 + scale_m) + shift_m
        j = i // N_MIXERS
        if i % N_MIXERS == 0:
            out, v = _chunk_gmlp_mixer(h, a_w_in[j], a_b_in[j], a_ln_w[j], a_ln_b[j],
                                       a_w_s[j], a_b_s[j], a_w_out[j])
            v_rows.append(v)
        else:
            out, s_new, cv_new = _ssd_mixer(h, ssm0[j], conv0[j], b_w_in[j], b_conv_w[j], b_conv_b[j],
                                            b_dt_bias[j], b_a_log[j], b_d[j], b_norm_w[j], b_w_out[j])
            ssm_states.append(s_new)
            conv_states.append(cv_new)
        x = x + gate_m * out
        h = _rmsnorm(x, norm_ffn_w[i]) * (1 + scale_f) + shift_f
        x = x + gate_f * _swiglu(h, f_w_in[i], f_w_out[i])
    y = _rmsnorm(x, final_norm_w)
    return y, jnp.stack(v_rows), jnp.stack(ssm_states), jnp.stack(conv_states)


def setup_inputs(seed: int = 0) -> dict:
    key = jax.random.key(seed)
    ks = jax.random.split(key, 32)
    f32 = jnp.float32
    nrm = lambda k, shape, s: s * jax.random.normal(k, shape, f32)
    D = D_MODEL
    u_dt = jax.random.uniform(ks[20], (N_SSD_LAYERS, SSD_HEADS), f32)
    dt0 = jnp.exp(u_dt * (math.log(DT_MAX) - math.log(DT_MIN)) + math.log(DT_MIN))
    dt_bias = dt0 + jnp.log(-jnp.expm1(-dt0))
    a_log = jnp.log(jax.random.uniform(ks[21], (N_SSD_LAYERS, SSD_HEADS), f32, 1.0, 16.0))
    return {
        "x_prompt": nrm(ks[0], (BATCH, SEQ, D), 1.0),
        "x_sample": nrm(ks[1], (DEC_BATCH, DEC_SEQ, D), 1.0),
        "c_prompt": nrm(ks[2], (BATCH, D), 1.0),
        "c_sample": nrm(ks[3], (DEC_BATCH, D), 1.0),
        "state_ssm": nrm(ks[4], (N_SSD_LAYERS, DEC_BATCH, SSD_HEADS, SSD_HEAD_DIM, SSD_STATE), 0.1),
        "state_conv": nrm(ks[5], (N_SSD_LAYERS, DEC_BATCH, SSD_CONV - 1, SSD_CONV_DIM), 1.0),
        "mod_w": nrm(ks[6], (DEPTH, D, 6 * D), D ** -0.5),
        "mod_b": nrm(ks[7], (DEPTH, 6 * D), 0.01),
        "norm_mix_w": 1.0 + nrm(ks[8], (DEPTH, D), 0.1),
        "norm_ffn_w": 1.0 + nrm(ks[9], (DEPTH, D), 0.1),
        "a_w_in": nrm(ks[10], (N_GMLP_LAYERS, D, 2 * GMLP_WIDTH), D ** -0.5),
        "a_b_in": nrm(ks[11], (N_GMLP_LAYERS, 2 * GMLP_WIDTH), 0.01),
        "a_ln_w": 1.0 + nrm(ks[12], (N_GMLP_LAYERS, GMLP_WIDTH), 0.1),
        "a_ln_b": nrm(ks[13], (N_GMLP_LAYERS, GMLP_WIDTH), 0.01),
        "a_w_s": nrm(ks[14], (N_GMLP_LAYERS, GMLP_GROUPS, GMLP_CHUNK, GMLP_CHUNK), GMLP_CHUNK ** -0.5),
        "a_b_s": 1.0 + nrm(ks[15], (N_GMLP_LAYERS, GMLP_GROUPS, GMLP_CHUNK), 0.1),
        "a_w_out": nrm(ks[16], (N_GMLP_LAYERS, GMLP_WIDTH, D), GMLP_WIDTH ** -0.5),
        "b_w_in": nrm(ks[17], (N_SSD_LAYERS, D, SSD_IN_DIM), D ** -0.5),
        "b_conv_w": nrm(ks[18], (N_SSD_LAYERS, SSD_CONV, SSD_CONV_DIM), SSD_CONV ** -0.5),
        "b_conv_b": nrm(ks[19], (N_SSD_LAYERS, SSD_CONV_DIM), 0.01),
        "b_dt_bias": dt_bias,
        "b_a_log": a_log,
        "b_d": 1.0 + nrm(ks[22], (N_SSD_LAYERS, SSD_HEADS), 0.1),
        "b_norm_w": 1.0 + nrm(ks[23], (N_SSD_LAYERS, SSD_INNER), 0.1),
        "b_w_out": nrm(ks[24], (N_SSD_LAYERS, SSD_INNER, D), SSD_INNER ** -0.5),
        "f_w_in": nrm(ks[25], (DEPTH, D, 2 * FFN_HIDDEN), D ** -0.5),
        "f_w_out": nrm(ks[26], (DEPTH, FFN_HIDDEN, D), FFN_HIDDEN ** -0.5),
        "final_norm_w": 1.0 + nrm(ks[27], (D,), 0.1),
    }


def reference(x_prompt, x_sample, c_prompt, c_sample, state_ssm, state_conv,
              mod_w, mod_b, norm_mix_w, norm_ffn_w,
              a_w_in, a_b_in, a_ln_w, a_ln_b, a_w_s, a_b_s, a_w_out,
              b_w_in, b_conv_w, b_conv_b, b_dt_bias, b_a_log, b_d, b_norm_w, b_w_out,
              f_w_in, f_w_out, final_norm_w):
    bp = x_prompt.shape[0]
    ssm0_prompt = jnp.zeros((N_SSD_LAYERS, bp) + state_ssm.shape[2:], state_ssm.dtype)
    conv0_prompt = jnp.zeros((N_SSD_LAYERS, bp) + state_conv.shape[2:], state_conv.dtype)
    y_prompt, v_prompt, ssm_prompt, conv_prompt = _trunk(
        x_prompt, c_prompt, ssm0_prompt, conv0_prompt, mod_w, mod_b, norm_mix_w, norm_ffn_w,
        a_w_in, a_b_in, a_ln_w, a_ln_b, a_w_s, a_b_s, a_w_out,
        b_w_in, b_conv_w, b_conv_b, b_dt_bias, b_a_log, b_d, b_norm_w, b_w_out,
        f_w_in, f_w_out, final_norm_w)
    y_sample, v_sample, ssm_sample, conv_sample = _trunk(
        x_sample, c_sample, state_ssm, state_conv, mod_w, mod_b, norm_mix_w, norm_ffn_w,
        a_w_in, a_b_in, a_ln_w, a_ln_b, a_w_s, a_b_s, a_w_out,
        b_w_in, b_conv_w, b_conv_b, b_dt_bias, b_a_log, b_d, b_norm_w, b_w_out,
        f_w_in, f_w_out, final_norm_w)
    return (y_prompt, y_sample, v_prompt, v_sample, ssm_prompt, ssm_sample, conv_prompt, conv_sample)
```

```python
import functools
import math
from typing import NamedTuple

import jax
import jax.numpy as jnp
from jax import lax
from jax.experimental import pallas as pl
from jax.experimental.pallas import tpu as pltpu

F32 = jnp.float32
BF16 = jnp.bfloat16

NORM_EPS = 1e-6
LN_EPS = 1e-5

GMLP_GROUPS = 16
GMLP_CHUNK = 128
SSD_HEAD_DIM = 64
SSD_STATE = 128
SSD_GROUPS = 8
SSD_CONV = 4
SSD_CHUNK = 128

SUBLANES = 8
LANES = 128
VMEM_LIMIT_BYTES = 56 * 1024 * 1024

TM = 1024
TN = 512


class Cfg(NamedTuple):
    n_ptiles: int
    tiles_per_seq: int
    seq_len_s: int
    srow0: int


def _cparams(sem):
    return pltpu.CompilerParams(dimension_semantics=sem, vmem_limit_bytes=VMEM_LIMIT_BYTES)


def _silu(x):
    return x / (1.0 + jnp.exp(-x))


def _gelu(x):
    return 0.5 * x * (1.0 + lax.erf(x * (1.0 / math.sqrt(2.0))))


def _rms_mod(x, w, scale, shift):
    y = x * lax.rsqrt(jnp.mean(x * x, axis=-1, keepdims=True) + NORM_EPS)
    return (y * w) * (1.0 + scale) + shift


ROW_CHUNK = 256


def _per_seq(i, cfg, x_refs, mod_refs, fn, o_ref):
    rows = o_ref.shape[0]
    n_chunks = rows // ROW_CHUNK

    def chunk_rows(c):
        return pl.ds(pl.multiple_of(c * ROW_CHUNK, ROW_CHUNK), ROW_CHUNK)

    @pl.when(i < cfg.n_ptiles)
    def _():
        s = i // cfg.tiles_per_seq
        ms = [m[pl.ds(s, 1), :] for m in mod_refs]

        def body(c, carry):
            rs = chunk_rows(c)
            o_ref[rs, :] = fn([x[rs, :] for x in x_refs], ms).astype(o_ref.dtype)
            return carry
        lax.fori_loop(0, n_chunks, body, 0)

    @pl.when(i >= cfg.n_ptiles)
    def _():
        nseq = ROW_CHUNK // cfg.seq_len_s
        row0 = cfg.srow0 + (i - cfg.n_ptiles) * (rows // cfg.seq_len_s)

        def body(c, carry):
            rs = chunk_rows(c)
            r0 = pl.multiple_of(row0 + c * nseq, SUBLANES)
            ms = [m[pl.ds(r0, nseq), :][:, None, :] for m in mod_refs]
            x3 = [x[rs, :].reshape(nseq, cfg.seq_len_s, x.shape[-1]) for x in x_refs]
            o_ref[rs, :] = fn(x3, ms).reshape(ROW_CHUNK, o_ref.shape[-1]).astype(o_ref.dtype)
            return carry
        lax.fori_loop(0, n_chunks, body, 0)


def _mod_body(c_ref, w_ref, b_ref, o_ref):
    sc = _silu(c_ref[...]).astype(BF16)
    o_ref[...] = jnp.dot(sc, w_ref[...].astype(BF16), preferred_element_type=F32) + b_ref[...]


def _mod_table(c_all, mod_w, mod_b):
    depth, d, n = mod_w.shape
    r = c_all.shape[0]
    tn = 1024
    return pl.pallas_call(
        _mod_body,
        grid=(depth, n // tn),
        in_specs=[
            pl.BlockSpec((r, d), lambda l, j: (0, 0)),
            pl.BlockSpec((None, d, tn), lambda l, j: (l, 0, j)),
            pl.BlockSpec((None, 1, tn), lambda l, j: (l, 0, j)),
        ],
        out_specs=pl.BlockSpec((None, r, tn), lambda l, j: (l, 0, j)),
        out_shape=jax.ShapeDtypeStruct((depth, r, n), F32),
        compiler_params=_cparams(("arbitrary", "arbitrary")),
        name="mod_table",
    )(c_all, mod_w, mod_b.reshape(depth, 1, n))


def _norm_prologue(i, j, cfg, x_ref, nw_ref, sc_ref, sh_ref, h_ref):
    @pl.when(j == 0)
    def _():
        _per_seq(i, cfg, [x_ref], [sc_ref, sh_ref],
                 lambda xs, ms: _rms_mod(xs[0], nw_ref[...], ms[0], ms[1]), h_ref)


def _gmlp_in_body(x_ref, nw_ref, sh_ref, sc_ref, wu_ref, wv_ref, bu_ref, bv_ref, u_ref, v_ref, h_ref, *, cfg):
    i, j = pl.program_id(0), pl.program_id(1)
    _norm_prologue(i, j, cfg, x_ref, nw_ref, sc_ref, sh_ref, h_ref)
    h = h_ref[...]
    u_ref[...] = _gelu(jnp.dot(h, wu_ref[...].astype(BF16), preferred_element_type=F32) + bu_ref[...])
    v_ref[...] = _gelu(jnp.dot(h, wv_ref[...].astype(BF16), preferred_element_type=F32) + bv_ref[...])


def _gmlp_in(x, mods, layer, norm_w, w_in, b_in, j_layer, cfg):
    m, d = x.shape
    width = w_in.shape[-1] // 2
    nj = width // TN
    r = mods.shape[1]
    return pl.pallas_call(
        functools.partial(_gmlp_in_body, cfg=cfg),
        grid=(m // TM, nj),
        in_specs=[
            pl.BlockSpec((TM, d), lambda i, j: (i, 0)),
            pl.BlockSpec((None, 1, d), lambda i, j: (layer, 0, 0)),
            pl.BlockSpec((None, r, d), lambda i, j: (layer, 0, 0)),
            pl.BlockSpec((None, r, d), lambda i, j: (layer, 0, 1)),
            pl.BlockSpec((None, d, TN), lambda i, j: (j_layer, 0, j)),
            pl.BlockSpec((None, d, TN), lambda i, j: (j_layer, 0, j + nj)),
            pl.BlockSpec((None, 1, TN), lambda i, j: (j_layer, 0, j)),
            pl.BlockSpec((None, 1, TN), lambda i, j: (j_layer, 0, j + nj)),
        ],
        out_specs=[pl.BlockSpec((TM, TN), lambda i, j: (i, j)), pl.BlockSpec((TM, TN), lambda i, j: (i, j))],
        out_shape=[jax.ShapeDtypeStruct((m, width), F32), jax.ShapeDtypeStruct((m, width), F32)],
        scratch_shapes=[pltpu.VMEM((TM, d), BF16)],
        compiler_params=_cparams(("arbitrary", "arbitrary")),
        name="gmlp_in",
    )(x, norm_w.reshape(-1, 1, d), mods, mods, w_in, w_in,
      b_in.reshape(b_in.shape[0], 1, -1), b_in.reshape(b_in.shape[0], 1, -1))


def _ffn_in_body(x_ref, nw_ref, sh_ref, sc_ref, wg_ref, wu_ref, a_ref, h_ref, *, cfg):
    i, j = pl.program_id(0), pl.program_id(1)
    _norm_prologue(i, j, cfg, x_ref, nw_ref, sc_ref, sh_ref, h_ref)
    h = h_ref[...]
    gate = jnp.dot(h, wg_ref[...].astype(BF16), preferred_element_type=F32)
    up = jnp.dot(h, wu_ref[...].astype(BF16), preferred_element_type=F32)
    a_ref[...] = (_silu(gate) * up).astype(BF16)


def _ffn_in(x, mods, layer, norm_w, w_in, cfg):
    m, d = x.shape
    hidden = w_in.shape[-1] // 2
    nj = hidden // TN
    r = mods.shape[1]
    return pl.pallas_call(
        functools.partial(_ffn_in_body, cfg=cfg),
        grid=(m // TM, nj),
        in_specs=[
            pl.BlockSpec((TM, d), lambda i, j: (i, 0)),
            pl.BlockSpec((None, 1, d), lambda i, j: (layer, 0, 0)),
            pl.BlockSpec((None, r, d), lambda i, j: (layer, 0, 3)),
            pl.BlockSpec((None, r, d), lambda i, j: (layer, 0, 4)),
            pl.BlockSpec((None, d, TN), lambda i, j: (layer, 0, j)),
            pl.BlockSpec((None, d, TN), lambda i, j: (layer, 0, j + nj)),
        ],
        out_specs=pl.BlockSpec((TM, TN), lambda i, j: (i, j)),
        out_shape=jax.ShapeDtypeStruct((m, hidden), BF16),
        scratch_shapes=[pltpu.VMEM((TM, d), BF16)],
        compiler_params=_cparams(("arbitrary", "arbitrary")),
        name="ffn_in",
    )(x, norm_w.reshape(-1, 1, d), mods, mods, w_in, w_in)


def _ssd_in_body(x_ref, nw_ref, sh_ref, sc_ref, w_ref, wdt_ref, o_ref, dt_ref, h_ref, *, cfg):
    i, j = pl.program_id(0), pl.program_id(1)
    _norm_prologue(i, j, cfg, x_ref, nw_ref, sc_ref, sh_ref, h_ref)
    h = h_ref[...]
    o_ref[...] = jnp.dot(h, w_ref[...].astype(BF16), preferred_element_type=F32)

    @pl.when(j == 0)
    def _():
        dt_ref[...] = jnp.dot(h, wdt_ref[...].astype(BF16), preferred_element_type=F32)


def _ssd_in(x, mods, layer, norm_w, w_in, j_layer, n_main, cfg):
    m, d = x.shape
    r = mods.shape[1]
    n_dt = w_in.shape[-1] - n_main
    w_dt = w_in[j_layer, :, n_main:]
    return pl.pallas_call(
        functools.partial(_ssd_in_body, cfg=cfg),
        grid=(m // TM, n_main // TN),
        in_specs=[
            pl.BlockSpec((TM, d), lambda i, j: (i, 0)),
            pl.BlockSpec((None, 1, d), lambda i, j: (layer, 0, 0)),
            pl.BlockSpec((None, r, d), lambda i, j: (layer, 0, 0)),
            pl.BlockSpec((None, r, d), lambda i, j: (layer, 0, 1)),
            pl.BlockSpec((None, d, TN), lambda i, j: (j_layer, 0, j)),
            pl.BlockSpec((d, n_dt), lambda i, j: (0, 0)),
        ],
        out_specs=[pl.BlockSpec((TM, TN), lambda i, j: (i, j)), pl.BlockSpec((TM, n_dt), lambda i, j: (i, 0))],
        out_shape=[jax.ShapeDtypeStruct((m, n_main), F32), jax.ShapeDtypeStruct((m, n_dt), F32)],
        scratch_shapes=[pltpu.VMEM((TM, d), BF16)],
        compiler_params=_cparams(("arbitrary", "arbitrary")),
        name="ssd_in",
    )(x, norm_w.reshape(-1, 1, d), mods, mods, w_in, w_dt)


def _resid_body(a_ref, w_ref, x_ref, g_ref, o_ref, acc_ref, *, cfg, nk):
    i, k = pl.program_id(0), pl.program_id(2)
    part = jnp.dot(a_ref[...], w_ref[...].astype(BF16), preferred_element_type=F32)

    def finalize():
        _per_seq(i, cfg, [x_ref, acc_ref], [g_ref], lambda xs, ms: xs[0] + ms[0] * xs[1], o_ref)

    if nk == 1:
        acc_ref[...] = part
        finalize()
    else:
        @pl.when(k == 0)
        def _():
            acc_ref[...] = part

        @pl.when(k > 0)
        def _():
            acc_ref[...] += part

        @pl.when(k == nk - 1)
        def _():
            finalize()


def _resid_matmul(a, w, w_layer, x, mods, layer, gate_chunk, nk, cfg, name):
    m, kdim = a.shape
    d = x.shape[1]
    tk = kdim // nk
    r = mods.shape[1]
    goff = gate_chunk * (d // TN)
    return pl.pallas_call(
        functools.partial(_resid_body, cfg=cfg, nk=nk),
        grid=(m // TM, d // TN, nk),
        in_specs=[
            pl.BlockSpec((TM, tk), lambda i, j, k: (i, k)),
            pl.BlockSpec((None, tk, TN), lambda i, j, k: (w_layer, k, j)),
            pl.BlockSpec((TM, TN), lambda i, j, k: (i, j)),
            pl.BlockSpec((None, r, TN), lambda i, j, k: (layer, 0, goff + j)),
        ],
        out_specs=pl.BlockSpec((TM, TN), lambda i, j, k: (i, j)),
        out_shape=jax.ShapeDtypeStruct((m, d), F32),
        scratch_shapes=[pltpu.VMEM((TM, TN), F32)],
        compiler_params=_cparams(("arbitrary", "arbitrary", "arbitrary")),
        name=name,
    )(a, w, x, mods)


MIX_ROWS = 2 * GMLP_CHUNK


def _gmlp_mix_body(u_ref, v_ref, lnw_ref, lnb_ref, ws_ref, mask_ref, bias_ref, g_ref, vp_ref, vs_ref,
                   vn_ref, *, n_prompt_steps, steps_per_seq):
    t = pl.program_id(0)
    v = v_ref[...]
    xc = v - jnp.mean(v, axis=-1, keepdims=True)
    vn = xc * lax.rsqrt(jnp.mean(xc * xc, axis=-1, keepdims=True) + LN_EPS) * lnw_ref[...] + lnb_ref[...]
    vn_ref[...] = vn

    @pl.when(jnp.logical_and(t < n_prompt_steps, t % steps_per_seq == steps_per_seq - 1))
    def _():
        vp_ref[...] = vn[MIX_ROWS - GMLP_CHUNK:, :]

    @pl.when(t >= n_prompt_steps)
    def _():
        vs_ref[...] = vn

    mask = mask_ref[...]
    for g in range(GMLP_GROUPS):
        wb = (ws_ref[g] * mask).astype(BF16)
        cols = slice(g * GMLP_CHUNK, (g + 1) * GMLP_CHUNK)
        for c in range(MIX_ROWS // GMLP_CHUNK):
            rows = slice(c * GMLP_CHUNK, (c + 1) * GMLP_CHUNK)
            s = jnp.dot(wb, vn_ref[rows, cols].astype(BF16), preferred_element_type=F32) + bias_ref[:, cols]
            g_ref[rows, cols] = (u_ref[rows, cols] * s).astype(BF16)


def _gmlp_mix(u, v, ln_w, ln_b, wmix, mask, bias, n_prompt_rows, seq_len, n_prompt_seq):
    m, width = u.shape
    n_prompt_steps = n_prompt_rows // MIX_ROWS
    steps_per_seq = seq_len // MIX_ROWS
    n_sample_rows = m - n_prompt_rows

    def variant(t):
        return jnp.where(t < n_prompt_steps, 0, 1)

    return pl.pallas_call(
        functools.partial(_gmlp_mix_body, n_prompt_steps=n_prompt_steps, steps_per_seq=steps_per_seq),
        grid=(m // MIX_ROWS,),
        in_specs=[
            pl.BlockSpec((MIX_ROWS, width), lambda t: (t, 0)),
            pl.BlockSpec((MIX_ROWS, width), lambda t: (t, 0)),
            pl.BlockSpec((1, width), lambda t: (0, 0)),
            pl.BlockSpec((1, width), lambda t: (0, 0)),
            pl.BlockSpec((None, GMLP_GROUPS, GMLP_CHUNK, GMLP_CHUNK), lambda t: (variant(t), 0, 0, 0)),
            pl.BlockSpec((None, GMLP_CHUNK, GMLP_CHUNK), lambda t: (variant(t), 0, 0)),
            pl.BlockSpec((None, GMLP_CHUNK, width), lambda t: (variant(t), 0, 0)),
        ],
        out_specs=[
            pl.BlockSpec((MIX_ROWS, width), lambda t: (t, 0)),
            pl.BlockSpec((GMLP_CHUNK, width), lambda t: (jnp.minimum(t // steps_per_seq, n_prompt_seq - 1), 0)),
            pl.BlockSpec((MIX_ROWS, width), lambda t: (jnp.maximum(t - n_prompt_steps, 0), 0)),
        ],
        out_shape=[
            jax.ShapeDtypeStruct((m, width), BF16),
            jax.ShapeDtypeStruct((n_prompt_seq * GMLP_CHUNK, width), F32),
            jax.ShapeDtypeStruct((n_sample_rows, width), F32),
        ],
        scratch_shapes=[pltpu.VMEM((MIX_ROWS, width), F32)],
        compiler_params=_cparams(("arbitrary",)),
        name="gmlp_mix",
    )(u, v, ln_w.reshape(1, width), ln_b.reshape(1, width), wmix, mask, bias)


GROUP_HEADS = 8
GROUP_W = GROUP_HEADS * SSD_HEAD_DIM
T = SSD_CHUNK


def _hi_dot(a, b):
    return jnp.dot(a, b, precision=lax.Precision.HIGHEST, preferred_element_type=F32)


def _ssd_group(xs, bm, cm, z, dtr, states, n_seg, dtb, alog, dsk, nw, m_ref, mt_ref, mseg_ref, e8_ref, e128_ref):
    seg = T // n_seg
    dt = jax.nn.softplus(dtr + dtb)
    a = dt * (-jnp.exp(alog))
    mmat = m_ref[...]
    cs = _hi_dot(mmat, a)
    cs_end = _hi_dot(mseg_ref[...], a)
    cs_t = _hi_dot(a.T, mt_ref[...])
    ecs = jnp.exp(cs)
    dte = jnp.exp(cs_end - cs)
    wide = _hi_dot(jnp.concatenate([dt, ecs, dte], axis=0), e8_ref[...])
    dt_x, ecs_x, dte_x = wide[:T], wide[T:2 * T], wide[2 * T:]
    csx = _hi_dot(cs, e128_ref[...])

    xdt = xs * dt_x
    xdt_b = xdt.astype(BF16)
    xd_t = (xdt * dte_x).T.astype(BF16)
    bmb = bm.astype(BF16)
    cmb = cm.astype(BF16)
    cb = lax.dot_general(cmb, bmb, (((1,), (1,)), ((), ())), preferred_element_type=F32)
    keep = mmat > 0.5
    lane = lax.broadcasted_iota(jnp.int32, (T, LANES), 1)
    ys = []
    for q in range(GROUP_HEADS // 2):
        ws = []
        for h in (2 * q, 2 * q + 1):
            diff = csx[:, h * T:(h + 1) * T] - cs_t[h:h + 1, :]
            ws.append((cb * jnp.exp(jnp.where(keep, diff, -jnp.inf))).astype(BF16))
        xp = xdt_b[:, q * LANES:(q + 1) * LANES]
        zero = jnp.zeros_like(xp)
        xpair = jnp.concatenate([jnp.where(lane < SSD_HEAD_DIM, xp, zero), jnp.where(lane >= SSD_HEAD_DIM, xp, zero)],
                                axis=0)
        ys.append(jnp.dot(jnp.concatenate(ws, axis=1), xpair, preferred_element_type=F32))
    y_diag = jnp.concatenate(ys, axis=1)

    row = lax.broadcasted_iota(jnp.int32, (T, SSD_STATE), 0)
    y_offs, new_states = [], []
    for s in range(n_seg):
        st = states[s]
        c_seg = cmb if n_seg == 1 else cm[s * seg:(s + 1) * seg].astype(BF16)
        y_offs.append(lax.dot_general(c_seg, st.astype(BF16), (((1,), (1,)), ((), ())), preferred_element_type=F32))
        if n_seg == 1:
            b_seg = bmb
        else:
            b_seg = jnp.where(jnp.logical_and(row >= s * seg, row < (s + 1) * seg), bm, 0.0).astype(BF16)
        upd = jnp.dot(xd_t, b_seg, preferred_element_type=F32)
        d_a = jnp.exp(cs_end[s * seg:s * seg + 1, :])
        decayed = jnp.concatenate(
            [st[h * SSD_HEAD_DIM:(h + 1) * SSD_HEAD_DIM, :] * d_a[:, h:h + 1] for h in range(GROUP_HEADS)], axis=0)
        new_states.append(decayed + upd)
    y_off = jnp.concatenate(y_offs, axis=0) if n_seg > 1 else y_offs[0]
    y = y_diag + y_off * ecs_x + dsk * xs
    y = y * _silu(z)
    yn = y * lax.rsqrt(jnp.mean(y * y, axis=-1, keepdims=True) + NORM_EPS) * nw
    return yn.astype(BF16), new_states


def _ssd_prompt_body(z_ref, xs_ref, b_ref, c_ref, dt_ref, wx_ref, wb_ref, wc_ref, bx_ref, bb_ref, bc_ref,
                     dtb_ref, alog_ref, dsk_ref, nw_ref, m_ref, mt_ref, mseg_ref, e8_ref, e128_ref,
                     yn_ref, st_ref, bufx_ref, bufb_ref, bufc_ref):
    c = pl.program_id(2)
    bufs = (bufx_ref, bufb_ref, bufc_ref)

    @pl.when(c == 0)
    def _():
        st_ref[...] = jnp.zeros_like(st_ref)
        for buf in bufs:
            buf[0:SUBLANES, :] = jnp.zeros((SUBLANES, buf.shape[1]), F32)

    def conv(buf, x_ref, w_ref, bias_ref):
        buf[SUBLANES:SUBLANES + T, :] = x_ref[...]
        acc = bias_ref[...] + buf[SUBLANES:SUBLANES + T, :] * w_ref[SSD_CONV - 1:SSD_CONV, :]
        for k in range(SSD_CONV - 1):
            lo = SUBLANES - (SSD_CONV - 1) + k
            acc = acc + buf[lo:lo + T, :] * w_ref[k:k + 1, :]
        buf[SUBLANES - (SSD_CONV - 1):SUBLANES, :] = buf[SUBLANES + T - (SSD_CONV - 1):SUBLANES + T, :]
        return _silu(acc)

    xs = conv(bufx_ref, xs_ref, wx_ref, bx_ref)
    bm = conv(bufb_ref, b_ref, wb_ref, bb_ref)
    cm = conv(bufc_ref, c_ref, wc_ref, bc_ref)
    yn, new_states = _ssd_group(xs, bm, cm, z_ref[...], dt_ref[...], [st_ref[...]], 1,
                                dtb_ref[...], alog_ref[...], dsk_ref[...], nw_ref[...],
                                m_ref, mt_ref, mseg_ref, e8_ref, e128_ref)
    yn_ref[...] = yn
    st_ref[...] = new_states[0]


def _ssd_sample_body(z_ref, xs_ref, b_ref, c_ref, px_ref, pb_ref, pc_ref, dt_ref, wx_ref, wb_ref, wc_ref,
                     bx_ref, bb_ref, bc_ref, dtb_ref, alog_ref, dsk_ref, nw_ref, m_ref, mt_ref, mseg_ref,
                     e8_ref, e128_ref, st_in_ref, yn_ref, st_ref, *, seq_len):
    n_seg = T // seq_len

    def conv(x_ref, p_ref, w_ref, bias_ref):
        x = x_ref[...]
        p = p_ref[...]
        tpos = lax.broadcasted_iota(jnp.int32, x.shape, 0) % seq_len
        acc = bias_ref[...] + x * w_ref[SSD_CONV - 1:SSD_CONV, :]
        for sh in range(1, SSD_CONV):
            shifted = jnp.where(tpos >= sh, pltpu.roll(x, sh, 0), pltpu.roll(p, T - seq_len + sh, 0))
            acc = acc + shifted * w_ref[SSD_CONV - 1 - sh:SSD_CONV - sh, :]
        return _silu(acc)

    xs = conv(xs_ref, px_ref, wx_ref, bx_ref)
    bm = conv(b_ref, pb_ref, wb_ref, bb_ref)
    cm = conv(c_ref, pc_ref, wc_ref, bc_ref)
    yn, new_states = _ssd_group(xs, bm, cm, z_ref[...], dt_ref[...], [st_in_ref[s] for s in range(n_seg)], n_seg,
                                dtb_ref[...], alog_ref[...], dsk_ref[...], nw_ref[...],
                                m_ref, mt_ref, mseg_ref, e8_ref, e128_ref)
    yn_ref[...] = yn
    for s in range(n_seg):
        st_ref[s] = new_states[s]


def _ssd_consts(seq_len):
    r = jnp.arange(T)
    same = (r[:, None] // seq_len) == (r[None, :] // seq_len)
    m = jnp.logical_and(same, r[None, :] <= r[:, None]).astype(F32)
    h = jnp.arange(GROUP_HEADS)
    e8 = (jnp.arange(GROUP_W)[None, :] // SSD_HEAD_DIM == h[:, None]).astype(F32)
    e128 = (jnp.arange(GROUP_HEADS * T)[None, :] // T == h[:, None]).astype(F32)
    return m, m.T, same.astype(F32), e8, e128


def _ssd_param_specs(nidx):
    def at(f):
        return lambda *ids: f(nidx(*ids))
    xoff = 0
    boff = (SSD_GROUPS * GROUP_W) // SSD_STATE
    coff = boff + SSD_GROUPS
    specs = []
    for rows in (SSD_CONV, 1):
        specs += [pl.BlockSpec((rows, GROUP_W), at(lambda g: (0, xoff + g))),
                  pl.BlockSpec((rows, SSD_STATE), at(lambda g: (0, boff + g))),
                  pl.BlockSpec((rows, SSD_STATE), at(lambda g: (0, coff + g)))]
    specs += [pl.BlockSpec((None, 1, GROUP_HEADS), at(lambda g: (g, 0, 0))),
              pl.BlockSpec((None, 1, GROUP_HEADS), at(lambda g: (g, 0, 0))),
              pl.BlockSpec((None, 1, GROUP_W), at(lambda g: (g, 0, 0))),
              pl.BlockSpec((None, 1, GROUP_W), at(lambda g: (g, 0, 0)))]
    specs += [pl.BlockSpec((T, T), lambda *ids: (0, 0))] * 3
    specs += [pl.BlockSpec((GROUP_HEADS, GROUP_W), lambda *ids: (0, 0)),
              pl.BlockSpec((GROUP_HEADS, GROUP_HEADS * T), lambda *ids: (0, 0))]
    return specs


def _zx_specs(ridx, gidx):
    zoff = 0
    xoff = SSD_GROUPS
    boff = (2 * SSD_GROUPS * GROUP_W) // SSD_STATE
    coff = boff + SSD_GROUPS
    return [pl.BlockSpec((T, GROUP_W), lambda *ids: (ridx(*ids), zoff + gidx(*ids))),
            pl.BlockSpec((T, GROUP_W), lambda *ids: (ridx(*ids), xoff + gidx(*ids))),
            pl.BlockSpec((T, SSD_STATE), lambda *ids: (ridx(*ids), boff + gidx(*ids))),
            pl.BlockSpec((T, SSD_STATE), lambda *ids: (ridx(*ids), coff + gidx(*ids)))]


def _ssd_scan(zx, dtg, conv_w, conv_b, dtb, alog, dsk, nw, state_s, conv_pad_s, n_pseq, seq_len_p, seq_len_s):
    inner = SSD_GROUPS * GROUP_W
    params = (conv_w, conv_w, conv_w, conv_b, conv_b, conv_b, dtb, alog, dsk, nw)
    n_chunks = seq_len_p // T

    rp = lambda b, g, c: b * n_chunks + c
    gp = lambda b, g, c: g
    yn_p, st_p = pl.pallas_call(
        _ssd_prompt_body,
        grid=(n_pseq, SSD_GROUPS, n_chunks),
        in_specs=_zx_specs(rp, gp) + [pl.BlockSpec((None, T, GROUP_HEADS), lambda b, g, c: (g, rp(b, g, c), 0))]
        + _ssd_param_specs(gp),
        out_specs=[pl.BlockSpec((T, GROUP_W), lambda b, g, c: (rp(b, g, c), g)),
                   pl.BlockSpec((None, GROUP_W, SSD_STATE), lambda b, g, c: (b, g, 0))],
        out_shape=[jax.ShapeDtypeStruct((n_pseq * seq_len_p, inner), BF16),
                   jax.ShapeDtypeStruct((n_pseq, inner, SSD_STATE), F32)],
        scratch_shapes=[pltpu.VMEM((SUBLANES + T, GROUP_W), F32), pltpu.VMEM((SUBLANES + T, SSD_STATE), F32),
                        pltpu.VMEM((SUBLANES + T, SSD_STATE), F32)],
        compiler_params=_cparams(("arbitrary", "arbitrary", "arbitrary")),
        name="ssd_scan_prompt",
    )(zx, zx, zx, zx, dtg, *params, *_ssd_consts(T))

    n_seg = T // seq_len_s
    n_sseq = state_s.shape[0]
    row0 = (n_pseq * seq_len_p) // T
    rs = lambda t, g: row0 + t
    gs = lambda t, g: g
    boff = inner // SSD_STATE
    yn_s, st_s = pl.pallas_call(
        functools.partial(_ssd_sample_body, seq_len=seq_len_s),
        grid=(n_sseq // n_seg, SSD_GROUPS),
        in_specs=_zx_specs(rs, gs)
        + [pl.BlockSpec((T, GROUP_W), lambda t, g: (t, g)),
           pl.BlockSpec((T, SSD_STATE), lambda t, g: (t, boff + g)),
           pl.BlockSpec((T, SSD_STATE), lambda t, g: (t, boff + SSD_GROUPS + g)),
           pl.BlockSpec((None, T, GROUP_HEADS), lambda t, g: (g, row0 + t, 0))]
        + _ssd_param_specs(gs)
        + [pl.BlockSpec((n_seg, GROUP_W, SSD_STATE), lambda t, g: (t, g, 0))],
        out_specs=[pl.BlockSpec((T, GROUP_W), lambda t, g: (t, g)),
                   pl.BlockSpec((n_seg, GROUP_W, SSD_STATE), lambda t, g: (t, g, 0))],
        out_shape=[jax.ShapeDtypeStruct((n_sseq * seq_len_s, inner), BF16),
                   jax.ShapeDtypeStruct(state_s.shape, F32)],
        compiler_params=_cparams(("arbitrary", "arbitrary")),
        name="ssd_scan_sample",
    )(zx, zx, zx, zx, conv_pad_s, conv_pad_s, conv_pad_s, dtg, *params, *_ssd_consts(seq_len_s), state_s)
    return jnp.concatenate([yn_p, yn_s], axis=0), st_p, st_s


def _final_body(x_ref, w_ref, yp_ref, ys_ref, *, n_ptiles):
    i = pl.program_id(0)
    x = x_ref[...]
    y = x * lax.rsqrt(jnp.mean(x * x, axis=-1, keepdims=True) + NORM_EPS) * w_ref[...]

    @pl.when(i < n_ptiles)
    def _():
        yp_ref[...] = y

    @pl.when(i >= n_ptiles)
    def _():
        ys_ref[...] = y


def _final_norm(x, w, n_prompt_rows):
    m, d = x.shape
    tm = 512
    n_ptiles = n_prompt_rows // tm
    return pl.pallas_call(
        functools.partial(_final_body, n_ptiles=n_ptiles),
        grid=(m // tm,),
        in_specs=[pl.BlockSpec((tm, d), lambda i: (i, 0)), pl.BlockSpec((1, d), lambda i: (0, 0))],
        out_specs=[pl.BlockSpec((tm, d), lambda i: (jnp.minimum(i, n_ptiles - 1), 0)),
                   pl.BlockSpec((tm, d), lambda i: (jnp.maximum(i - n_ptiles, 0), 0))],
        out_shape=[jax.ShapeDtypeStruct((n_prompt_rows, d), F32), jax.ShapeDtypeStruct((m - n_prompt_rows, d), F32)],
        compiler_params=_cparams(("arbitrary",)),
        name="final_norm",
    )(x, w.reshape(1, d))


def kernel(x_prompt, x_sample, c_prompt, c_sample, state_ssm, state_conv, mod_w, mod_b, norm_mix_w, norm_ffn_w,
           a_w_in, a_b_in, a_ln_w, a_ln_b, a_w_s, a_b_s, a_w_out,
           b_w_in, b_conv_w, b_conv_b, b_dt_bias, b_a_log, b_d, b_norm_w, b_w_out,
           f_w_in, f_w_out, final_norm_w):
    bp, lp, d = x_prompt.shape
    bs, ls, _ = x_sample.shape
    depth = mod_w.shape[0]
    n_prompt = bp * lp
    n_sample = bs * ls
    assert lp % TM == 0 and n_sample % TM == 0 and TM % ls == 0 and bp <= SUBLANES
    assert ls >= SSD_CONV - 1 and T % ls == 0 and lp % T == 0 and GMLP_CHUNK % ls == 0 and lp % MIX_ROWS == 0
    assert depth == 2 and a_w_in.shape[0] == 1 and b_w_in.shape[0] == 1
    cfg = Cfg(n_ptiles=n_prompt // TM, tiles_per_seq=lp // TM, seq_len_s=ls, srow0=SUBLANES)

    x = jnp.concatenate([x_prompt.reshape(n_prompt, d), x_sample.reshape(n_sample, d)], axis=0)
    c_all = jnp.concatenate([c_prompt, jnp.zeros((SUBLANES - bp, d), F32), c_sample], axis=0)
    mods = _mod_table(c_all, mod_w, mod_b)

    u, v = _gmlp_in(x, mods, 0, norm_mix_w, a_w_in, a_b_in, 0, cfg)
    r = jnp.arange(GMLP_CHUNK)
    tril = r[None, :] <= r[:, None]
    mask = jnp.stack([tril, jnp.logical_and(tril, (r[:, None] // ls) == (r[None, :] // ls))]).astype(F32)
    rep = GMLP_CHUNK // ls
    wmix = jnp.stack([a_w_s[0], jnp.tile(a_w_s[0, :, :ls, :ls], (1, rep, rep))])
    width = a_w_in.shape[-1] // 2
    bias = jnp.stack([jnp.repeat(a_b_s[0].T, width // GMLP_GROUPS, axis=1),
                      jnp.repeat(jnp.tile(a_b_s[0, :, :ls].T, (rep, 1)), width // GMLP_GROUPS, axis=1)])
    gated, v_p, v_s = _gmlp_mix(u, v, a_ln_w[0], a_ln_b[0], wmix, mask, bias, n_prompt, lp, bp)
    x = _resid_matmul(gated, a_w_out, 0, x, mods, 0, 2, 1, cfg, "gmlp_out")
    act = _ffn_in(x, mods, 0, norm_ffn_w, f_w_in, cfg)
    x = _resid_matmul(act, f_w_out, 0, x, mods, 0, 5, 2, cfg, "ffn_out0")

    inner = b_w_out.shape[1]
    conv_dim = b_conv_w.shape[-1]
    n_main = inner + conv_dim
    zx, dt_raw = _ssd_in(x, mods, 1, norm_mix_w, b_w_in, 0, n_main, cfg)
    m_all = n_prompt + n_sample
    dtg = dt_raw.reshape(m_all, SSD_GROUPS, GROUP_HEADS).transpose(1, 0, 2)
    conv_pad = jnp.pad(state_conv[0], ((0, 0), (ls - (SSD_CONV - 1), 0), (0, 0))).reshape(n_sample, conv_dim)
    yn, ssm_p, ssm_s = _ssd_scan(
        zx, dtg, b_conv_w[0], b_conv_b,
        b_dt_bias.reshape(SSD_GROUPS, 1, GROUP_HEADS), b_a_log.reshape(SSD_GROUPS, 1, GROUP_HEADS),
        jnp.repeat(b_d[0], SSD_HEAD_DIM).reshape(SSD_GROUPS, 1, GROUP_W), b_norm_w.reshape(SSD_GROUPS, 1, GROUP_W),
        state_ssm[0].reshape(bs, inner, SSD_STATE), conv_pad, bp, lp, ls)
    x = _resid_matmul(yn, b_w_out, 0, x, mods, 1, 2, 1, cfg, "ssd_out")
    act = _ffn_in(x, mods, 1, norm_ffn_w, f_w_in, cfg)
    x = _resid_matmul(act, f_w_out, 1, x, mods, 1, 5, 2, cfg, "ffn_out1")

    y_p, y_s = _final_norm(x, final_norm_w, n_prompt)

    heads = inner // SSD_HEAD_DIM
    zx3 = zx.reshape(m_all // ls, ls, n_main)
    conv_p = zx3[lp // ls - 1:n_prompt // ls:lp // ls, ls - (SSD_CONV - 1):, inner:]
    conv_s = zx3[n_prompt // ls:, ls - (SSD_CONV - 1):, inner:]
    return (y_p.reshape(bp, lp, d), y_s.reshape(bs, ls, d),
            v_p.reshape(1, bp, GMLP_CHUNK, width), v_s.reshape(1, bs, ls, width),
            ssm_p.reshape(1, bp, heads, SSD_HEAD_DIM, SSD_STATE), ssm_s.reshape(1, bs, heads, SSD_HEAD_DIM, SSD_STATE),
            conv_p[None], conv_s[None])
```

```python
import functools
import math
from typing import NamedTuple

import jax
import jax.numpy as jnp
from jax import lax
from jax.experimental import pallas as pl
from jax.experimental.pallas import tpu as pltpu

F32 = jnp.float32
BF16 = jnp.bfloat16

NORM_EPS = 1e-6
LN_EPS = 1e-5

GMLP_GROUPS = 16
GMLP_CHUNK = 128
SSD_HEAD_DIM = 64
SSD_STATE = 128
SSD_GROUPS = 8
SSD_CONV = 4
SSD_CHUNK = 128

SUBLANES = 8
LANES = 128
VMEM_LIMIT_BYTES = 56 * 1024 * 1024

TM = 1024
TN = 512
TN_NARROW = 256
ROW_CHUNK = 256


class Cfg(NamedTuple):
    n_ptiles: int
    tiles_per_seq: int
    seq_len_s: int
    srow0: int


def _cparams(sem):
    return pltpu.CompilerParams(dimension_semantics=sem, vmem_limit_bytes=VMEM_LIMIT_BYTES)


def _silu(x):
    return x / (1.0 + jnp.exp(-x))


def _gelu(x):
    return 0.5 * x * (1.0 + lax.erf(x * (1.0 / math.sqrt(2.0))))


def _rms_mod(x, w, scale, shift):
    y = x * lax.rsqrt(jnp.mean(x * x, axis=-1, keepdims=True) + NORM_EPS)
    return (y * w) * (1.0 + scale) + shift


def _per_seq(i, cfg, xp_refs, xs_refs, mod_refs, fn, o_ref):
    rows = o_ref.shape[0]
    n_chunks = rows // ROW_CHUNK

    def chunk_rows(c):
        return pl.ds(pl.multiple_of(c * ROW_CHUNK, ROW_CHUNK), ROW_CHUNK)

    @pl.when(i < cfg.n_ptiles)
    def _():
        s = i // cfg.tiles_per_seq
        ms = [m[pl.ds(s, 1), :] for m in mod_refs]

        def body(c, carry):
            rs = chunk_rows(c)
            o_ref[rs, :] = fn([x[rs, :] for x in xp_refs], ms).astype(o_ref.dtype)
            return carry
        lax.fori_loop(0, n_chunks, body, 0)

    @pl.when(i >= cfg.n_ptiles)
    def _():
        nseq = ROW_CHUNK // cfg.seq_len_s
        row0 = cfg.srow0 + (i - cfg.n_ptiles) * (rows // cfg.seq_len_s)

        def body(c, carry):
            rs = chunk_rows(c)
            r0 = pl.multiple_of(row0 + c * nseq, SUBLANES)
            ms = [m[pl.ds(r0, nseq), :][:, None, :] for m in mod_refs]
            x3 = [x[rs, :].reshape(nseq, cfg.seq_len_s, x.shape[-1]) for x in xs_refs]
            o_ref[rs, :] = fn(x3, ms).reshape(ROW_CHUNK, o_ref.shape[-1]).astype(o_ref.dtype)
            return carry
        lax.fori_loop(0, n_chunks, body, 0)


def _prompt_block(cfg):
    return lambda i: jnp.minimum(i, cfg.n_ptiles - 1)


def _sample_block(cfg):
    return lambda i: jnp.maximum(i - cfg.n_ptiles, 0)


def _mod_body(c_ref, w_ref, b_ref, o_ref):
    sc = _silu(c_ref[...]).astype(BF16)
    o_ref[...] = jnp.dot(sc, w_ref[...].astype(BF16), preferred_element_type=F32) + b_ref[...]


def _mod_table(c_all, mod_w, mod_b):
    depth, d, n = mod_w.shape
    r = c_all.shape[0]
    tn = 1024
    return pl.pallas_call(
        _mod_body,
        grid=(depth, n // tn),
        in_specs=[
            pl.BlockSpec((r, d), lambda l, j: (0, 0)),
            pl.BlockSpec((None, d, tn), lambda l, j: (l, 0, j)),
            pl.BlockSpec((None, 1, tn), lambda l, j: (l, 0, j)),
        ],
        out_specs=pl.BlockSpec((None, r, tn), lambda l, j: (l, 0, j)),
        out_shape=jax.ShapeDtypeStruct((depth, r, n), F32),
        compiler_params=_cparams(("arbitrary", "arbitrary")),
        name="mod_table",
    )(c_all, mod_w, mod_b.reshape(depth, 1, n))


def _norm_prologue(i, j, cfg, xp_ref, xs_ref, nw_ref, sc_ref, sh_ref, h_ref):
    @pl.when(j == 0)
    def _():
        _per_seq(i, cfg, [xp_ref], [xs_ref], [sc_ref, sh_ref],
                 lambda xs, ms: _rms_mod(xs[0], nw_ref[...], ms[0], ms[1]), h_ref)


def _gmlp_in_body(xp_ref, xs_ref, nw_ref, sh_ref, sc_ref, wu_ref, wv_ref, bu_ref, bv_ref, u_ref, v_ref, h_ref, *, cfg):
    i, j = pl.program_id(0), pl.program_id(1)
    _norm_prologue(i, j, cfg, xp_ref, xs_ref, nw_ref, sc_ref, sh_ref, h_ref)
    h = h_ref[...]
    u_ref[...] = _gelu(jnp.dot(h, wu_ref[...].astype(BF16), preferred_element_type=F32) + bu_ref[...])
    v_ref[...] = _gelu(jnp.dot(h, wv_ref[...].astype(BF16), preferred_element_type=F32) + bv_ref[...])


def _gmlp_in(x_p, x_s, mods, layer, norm_w, w_in, b_in, j_layer, cfg):
    d = x_p.shape[1]
    m = x_p.shape[0] + x_s.shape[0]
    width = w_in.shape[-1] // 2
    tn = TN_NARROW
    nj = width // tn
    r = mods.shape[1]
    pb, sb = _prompt_block(cfg), _sample_block(cfg)
    b2 = b_in.reshape(b_in.shape[0], 1, -1)
    return pl.pallas_call(
        functools.partial(_gmlp_in_body, cfg=cfg),
        grid=(m // TM, nj),
        in_specs=[
            pl.BlockSpec((TM, d), lambda i, j: (pb(i), 0)),
            pl.BlockSpec((TM, d), lambda i, j: (sb(i), 0), pipeline_mode=pl.Buffered(1)),
            pl.BlockSpec((None, 1, d), lambda i, j: (layer, 0, 0)),
            pl.BlockSpec((None, r, d), lambda i, j: (layer, 0, 0)),
            pl.BlockSpec((None, r, d), lambda i, j: (layer, 0, 1)),
            pl.BlockSpec((None, d, tn), lambda i, j: (j_layer, 0, j)),
            pl.BlockSpec((None, d, tn), lambda i, j: (j_layer, 0, j + nj)),
            pl.BlockSpec((None, 1, tn), lambda i, j: (j_layer, 0, j)),
            pl.BlockSpec((None, 1, tn), lambda i, j: (j_layer, 0, j + nj)),
        ],
        out_specs=[pl.BlockSpec((TM, tn), lambda i, j: (i, j)), pl.BlockSpec((TM, tn), lambda i, j: (i, j))],
        out_shape=[jax.ShapeDtypeStruct((m, width), F32), jax.ShapeDtypeStruct((m, width), F32)],
        scratch_shapes=[pltpu.VMEM((TM, d), BF16)],
        compiler_params=_cparams(("arbitrary", "arbitrary")),
        name="gmlp_in",
    )(x_p, x_s, norm_w.reshape(-1, 1, d), mods, mods, w_in, w_in, b2, b2)


def _ffn_in_body(x_ref, nw_ref, sh_ref, sc_ref, wg_ref, wu_ref, a_ref, h_ref, *, cfg):
    i, j = pl.program_id(0), pl.program_id(1)
    _norm_prologue(i, j, cfg, x_ref, x_ref, nw_ref, sc_ref, sh_ref, h_ref)
    h = h_ref[...]
    gate = jnp.dot(h, wg_ref[...].astype(BF16), preferred_element_type=F32)
    up = jnp.dot(h, wu_ref[...].astype(BF16), preferred_element_type=F32)
    a_ref[...] = (_silu(gate) * up).astype(BF16)


def _ffn_in(x, mods, layer, norm_w, w_in, cfg):
    m, d = x.shape
    hidden = w_in.shape[-1] // 2
    nj = hidden // TN
    r = mods.shape[1]
    return pl.pallas_call(
        functools.partial(_ffn_in_body, cfg=cfg),
        grid=(m // TM, nj),
        in_specs=[
            pl.BlockSpec((TM, d), lambda i, j: (i, 0)),
            pl.BlockSpec((None, 1, d), lambda i, j: (layer, 0, 0)),
            pl.BlockSpec((None, r, d), lambda i, j: (layer, 0, 3)),
            pl.BlockSpec((None, r, d), lambda i, j: (layer, 0, 4)),
            pl.BlockSpec((None, d, TN), lambda i, j: (layer, 0, j)),
            pl.BlockSpec((None, d, TN), lambda i, j: (layer, 0, j + nj)),
        ],
        out_specs=pl.BlockSpec((TM, TN), lambda i, j: (i, j)),
        out_shape=jax.ShapeDtypeStruct((m, hidden), BF16),
        scratch_shapes=[pltpu.VMEM((TM, d), BF16)],
        compiler_params=_cparams(("arbitrary", "arbitrary")),
        name="ffn_in",
    )(x, norm_w.reshape(-1, 1, d), mods, mods, w_in, w_in)


_NT = (((1,), (1,)), ((), ()))


def _ssd_in_body(x_ref, nw_ref, sh_ref, sc_ref, w_ref, wdt_ref, o_ref, dt_ref, h_ref, *, cfg):
    i, j = pl.program_id(0), pl.program_id(1)
    _norm_prologue(i, j, cfg, x_ref, x_ref, nw_ref, sc_ref, sh_ref, h_ref)
    h = h_ref[...]
    o_ref[...] = lax.dot_general(h, w_ref[...].astype(BF16), _NT, preferred_element_type=F32)

    @pl.when(j == 0)
    def _():
        dt_ref[...] = lax.dot_general(h, wdt_ref[...].astype(BF16), _NT, preferred_element_type=F32)


def _ssd_in(x, mods, layer, norm_w, w_in_t, n_main, cfg):
    m, d = x.shape
    r = mods.shape[1]
    n_dt = w_in_t.shape[0] - n_main
    nj = n_main // TN
    return pl.pallas_call(
        functools.partial(_ssd_in_body, cfg=cfg),
        grid=(m // TM, nj),
        in_specs=[
            pl.BlockSpec((TM, d), lambda i, j: (i, 0)),
            pl.BlockSpec((None, 1, d), lambda i, j: (layer, 0, 0)),
            pl.BlockSpec((None, r, d), lambda i, j: (layer, 0, 0)),
            pl.BlockSpec((None, r, d), lambda i, j: (layer, 0, 1)),
            pl.BlockSpec((TN, d), lambda i, j: (j, 0)),
            pl.BlockSpec((n_dt, d), lambda i, j: (n_main // n_dt, 0)),
        ],
        out_specs=[pl.BlockSpec((None, TM, TN), lambda i, j: (j, i, 0)), pl.BlockSpec((TM, n_dt), lambda i, j: (i, 0))],
        out_shape=[jax.ShapeDtypeStruct((nj, m, TN), F32), jax.ShapeDtypeStruct((m, n_dt), F32)],
        scratch_shapes=[pltpu.VMEM((TM, d), BF16)],
        compiler_params=_cparams(("arbitrary", "arbitrary")),
        name="ssd_in",
    )(x, norm_w.reshape(-1, 1, d), mods, mods, w_in_t, w_in_t)


def _resid_body(a_ref, w_ref, xp_ref, xs_ref, g_ref, o_ref, acc_ref, *, cfg, nk):
    i = pl.program_id(0)
    if nk == 1:
        acc_ref[...] = jnp.dot(a_ref[...], w_ref[...].astype(BF16), preferred_element_type=F32)
    else:
        tk = w_ref.shape[0] // nk
        acc = jnp.dot(a_ref[0], w_ref[0:tk, :].astype(BF16), preferred_element_type=F32)
        for k in range(1, nk):
            acc = acc + jnp.dot(a_ref[k], w_ref[k * tk:(k + 1) * tk, :].astype(BF16), preferred_element_type=F32)
        acc_ref[...] = acc
    _per_seq(i, cfg, [xp_ref, acc_ref], [xs_ref, acc_ref], [g_ref], lambda xs, ms: xs[0] + ms[0] * xs[1], o_ref)


def _resid_matmul(a, w, w_layer, x_p, x_s, mods, layer, gate_chunk, tn, cfg, name):
    nk = 1 if a.ndim == 2 else a.shape[0]
    m = a.shape[-2]
    kdim = a.shape[-1] * nk
    d = x_p.shape[1]
    r = mods.shape[1]
    goff = gate_chunk * (d // tn)
    if x_s is None:
        x_s = x_p
        xp_spec = xs_spec = pl.BlockSpec((TM, tn), lambda i, j: (i, j))
    else:
        pb, sb = _prompt_block(cfg), _sample_block(cfg)
        xp_spec = pl.BlockSpec((TM, tn), lambda i, j: (pb(i), j))
        xs_spec = pl.BlockSpec((TM, tn), lambda i, j: (sb(i), j))
    if nk == 1:
        a_spec = pl.BlockSpec((TM, kdim), lambda i, j: (i, 0))
    else:
        a_spec = pl.BlockSpec((nk, TM, kdim // nk), lambda i, j: (0, i, 0))
    return pl.pallas_call(
        functools.partial(_resid_body, cfg=cfg, nk=nk),
        grid=(m // TM, d // tn),
        in_specs=[
            a_spec,
            pl.BlockSpec((None, kdim, tn), lambda i, j: (w_layer, 0, j)),
            xp_spec,
            xs_spec,
            pl.BlockSpec((None, r, tn), lambda i, j: (layer, 0, goff + j)),
        ],
        out_specs=pl.BlockSpec((TM, tn), lambda i, j: (i, j)),
        out_shape=jax.ShapeDtypeStruct((m, d), F32),
        scratch_shapes=[pltpu.VMEM((TM, tn), F32)],
        compiler_params=_cparams(("arbitrary", "arbitrary")),
        name=name,
    )(a, w, x_p, x_s, mods)


MIX_ROWS = 2 * GMLP_CHUNK


def _gmlp_mix_body(u_ref, v_ref, lnw_ref, lnb_ref, ws_ref, mask_ref, bias_ref, g_ref, vp_ref, vs_ref,
                   vn_ref, *, n_prompt_steps, steps_per_seq):
    t = pl.program_id(0)
    v = v_ref[...]
    xc = v - jnp.mean(v, axis=-1, keepdims=True)
    vn = xc * lax.rsqrt(jnp.mean(xc * xc, axis=-1, keepdims=True) + LN_EPS) * lnw_ref[...] + lnb_ref[...]
    vn_ref[...] = vn

    @pl.when(jnp.logical_and(t < n_prompt_steps, t % steps_per_seq == steps_per_seq - 1))
    def _():
        vp_ref[...] = vn[MIX_ROWS - GMLP_CHUNK:, :]

    @pl.when(t >= n_prompt_steps)
    def _():
        vs_ref[...] = vn

    mask = mask_ref[...]
    for g in range(GMLP_GROUPS):
        wb = (ws_ref[g] * mask).astype(BF16)
        cols = slice(g * GMLP_CHUNK, (g + 1) * GMLP_CHUNK)
        for c in range(MIX_ROWS // GMLP_CHUNK):
            rows = slice(c * GMLP_CHUNK, (c + 1) * GMLP_CHUNK)
            s = jnp.dot(wb, vn_ref[rows, cols].astype(BF16), preferred_element_type=F32) + bias_ref[:, cols]
            g_ref[rows, cols] = (u_ref[rows, cols] * s).astype(BF16)


def _gmlp_mix(u, v, ln_w, ln_b, wmix, mask, bias, n_prompt_rows, seq_len, n_prompt_seq):
    m, width = u.shape
    n_prompt_steps = n_prompt_rows // MIX_ROWS
    steps_per_seq = seq_len // MIX_ROWS
    n_sample_rows = m - n_prompt_rows

    def variant(t):
        return jnp.where(t < n_prompt_steps, 0, 1)

    return pl.pallas_call(
        functools.partial(_gmlp_mix_body, n_prompt_steps=n_prompt_steps, steps_per_seq=steps_per_seq),
        grid=(m // MIX_ROWS,),
        in_specs=[
            pl.BlockSpec((MIX_ROWS, width), lambda t: (t, 0)),
            pl.BlockSpec((MIX_ROWS, width), lambda t: (t, 0)),
            pl.BlockSpec((1, width), lambda t: (0, 0)),
            pl.BlockSpec((1, width), lambda t: (0, 0)),
            pl.BlockSpec((None, GMLP_GROUPS, GMLP_CHUNK, GMLP_CHUNK), lambda t: (variant(t), 0, 0, 0)),
            pl.BlockSpec((None, GMLP_CHUNK, GMLP_CHUNK), lambda t: (variant(t), 0, 0)),
            pl.BlockSpec((None, GMLP_CHUNK, width), lambda t: (variant(t), 0, 0)),
        ],
        out_specs=[
            pl.BlockSpec((MIX_ROWS, width), lambda t: (t, 0)),
            pl.BlockSpec((GMLP_CHUNK, width), lambda t: (jnp.minimum(t // steps_per_seq, n_prompt_seq - 1), 0)),
            pl.BlockSpec((MIX_ROWS, width), lambda t: (jnp.maximum(t - n_prompt_steps, 0), 0)),
        ],
        out_shape=[
            jax.ShapeDtypeStruct((m, width), BF16),
            jax.ShapeDtypeStruct((n_prompt_seq * GMLP_CHUNK, width), F32),
            jax.ShapeDtypeStruct((n_sample_rows, width), F32),
        ],
        scratch_shapes=[pltpu.VMEM((MIX_ROWS, width), F32)],
        compiler_params=_cparams(("arbitrary",)),
        name="gmlp_mix",
    )(u, v, ln_w.reshape(1, width), ln_b.reshape(1, width), wmix, mask, bias)


GROUP_HEADS = 8
GROUP_W = GROUP_HEADS * SSD_HEAD_DIM
T = SSD_CHUNK
BC_PER_BLOCK = GROUP_W // SSD_STATE


def _split3(x):
    hi = x.astype(BF16)
    r = x - hi.astype(F32)
    mid = r.astype(BF16)
    lo = (r - mid.astype(F32)).astype(BF16)
    return hi, mid, lo


def _dot_exact_rhs(m_b, pieces):
    return sum(jnp.dot(m_b, p, preferred_element_type=F32) for p in pieces)


def _dot_exact_lhs(pieces, e_b):
    return sum(jnp.dot(p, e_b, preferred_element_type=F32) for p in pieces)


def _ssd_group(xs, bm, cm, z, dt_x, cs_x, cs_end_x, cs_row, d_a, states, n_seg, dsk, nw, keep):
    seg = T // n_seg
    xdt = xs * dt_x
    ecs_x = jnp.exp(cs_x)
    dte_x = jnp.exp(cs_end_x - cs_x)
    xdt_b = xdt.astype(BF16)
    xd_t = (xdt * dte_x).T.astype(BF16)
    bmb = bm.astype(BF16)
    cmb = cm.astype(BF16)
    cb = lax.dot_general(cmb, bmb, _NT, preferred_element_type=F32)
    lane = lax.broadcasted_iota(jnp.int32, (T, LANES), 1)
    lo_half = lane < SSD_HEAD_DIM
    ys = []
    for q in range(GROUP_HEADS // 2):
        v = cs_x[:, q * LANES:(q + 1) * LANES]
        r = pltpu.roll(v, SSD_HEAD_DIM, 1)
        cols = (jnp.where(lo_half, v, r), jnp.where(lo_half, r, v))
        ws = []
        for e in range(2):
            diff = cols[e] - cs_row(2 * q + e)
            ws.append((cb * jnp.exp(jnp.where(keep, diff, -jnp.inf))).astype(BF16))
        xp = xdt_b[:, q * LANES:(q + 1) * LANES]
        zero = jnp.zeros_like(xp)
        xpair = jnp.concatenate([jnp.where(lo_half, xp, zero), jnp.where(lo_half, zero, xp)], axis=0)
        ys.append(jnp.dot(jnp.concatenate(ws, axis=1), xpair, preferred_element_type=F32))
    y_diag = jnp.concatenate(ys, axis=1)

    row = lax.broadcasted_iota(jnp.int32, (T, SSD_STATE), 0)
    y_offs, new_states = [], []
    for s in range(n_seg):
        st = states[s]
        c_seg = cmb if n_seg == 1 else cm[s * seg:(s + 1) * seg].astype(BF16)
        y_offs.append(lax.dot_general(c_seg, st.astype(BF16), _NT, preferred_element_type=F32))
        if n_seg == 1:
            b_seg = bmb
        else:
            b_seg = jnp.where(jnp.logical_and(row >= s * seg, row < (s + 1) * seg), bm, 0.0).astype(BF16)
        upd = jnp.dot(xd_t, b_seg, preferred_element_type=F32)
        decayed = jnp.concatenate(
            [st[h * SSD_HEAD_DIM:(h + 1) * SSD_HEAD_DIM, :] * d_a(s, h) for h in range(GROUP_HEADS)], axis=0)
        new_states.append(decayed + upd)
    y_off = jnp.concatenate(y_offs, axis=0) if n_seg > 1 else y_offs[0]
    y = y_diag + y_off * ecs_x + dsk * xs
    y = y * _silu(z)
    yn = y * lax.rsqrt(jnp.mean(y * y, axis=-1, keepdims=True) + NORM_EPS) * nw
    return yn.astype(BF16), new_states


def _causal_keep(seq_len):
    r = lax.broadcasted_iota(jnp.int32, (T, T), 0)
    c = lax.broadcasted_iota(jnp.int32, (T, T), 1)
    keep = c <= r
    if seq_len < T:
        keep = jnp.logical_and(keep, (r // seq_len) == (c // seq_len))
    return keep


def _ssd_prompt_body(z_ref, xs_ref, bc_ref, dt_ref, cw_ref, cbias_ref, dtb_ref, alog_ref, dsk_ref, nw_ref,
                     m_ref, e_ref, yn_ref, st_ref,
                     cbuf_ref, xc_ref, bcs_ref, dt3_ref, cs3_ref, cst_ref):
    c = pl.program_id(1)
    n_x = xs_ref.shape[0]
    pad = SUBLANES
    taps = SSD_CONV - 1

    @pl.when(c == 0)
    def _():
        st_ref[...] = jnp.zeros_like(st_ref)
        cbuf_ref[:, 0:pad, :] = jnp.zeros((cbuf_ref.shape[0], pad, GROUP_W), F32)

    def conv(k, src):
        w = cw_ref[k]
        cbuf_ref[k, pad:pad + T, :] = src
        acc = cbias_ref[k] + src * w[taps:taps + 1, :]
        for kk in range(taps):
            acc = acc + cbuf_ref[k, pad - taps + kk:pad - taps + kk + T, :] * w[kk:kk + 1, :]
        cbuf_ref[k, pad - taps:pad, :] = cbuf_ref[k, pad + T - taps:pad + T, :]
        return _silu(acc)

    def conv_x(k, carry):
        xc_ref[k] = conv(k, xs_ref[k])
        return carry
    lax.fori_loop(0, n_x, conv_x, 0)

    def conv_bc(k, carry):
        out = conv(n_x + k, bc_ref[k])
        for gg in range(BC_PER_BLOCK):
            bcs_ref[k * BC_PER_BLOCK + gg] = out[:, gg * SSD_STATE:(gg + 1) * SSD_STATE]
        return carry
    lax.fori_loop(0, bc_ref.shape[0], conv_bc, 0)

    dt = jax.nn.softplus(dt_ref[...] + dtb_ref[...])
    a = dt * (-jnp.exp(alog_ref[...]))
    cs = _dot_exact_rhs(m_ref[...], _split3(a))
    cst_ref[...] = cs.T
    for p, (dt_p, cs_p) in enumerate(zip(_split3(dt), _split3(cs))):
        dt3_ref[p] = dt_p
        cs3_ref[p] = cs_p

    keep = _causal_keep(T)

    def group(g):
        h0 = g * GROUP_HEADS
        rows = pl.ds(g * GROUP_W, GROUP_W)
        e_g = e_ref[g]
        cs_x = _dot_exact_lhs([cs3_ref[p] for p in range(3)], e_g)
        yn, new_states = _ssd_group(
            xc_ref[g], bcs_ref[g], bcs_ref[SSD_GROUPS + g], z_ref[g],
            _dot_exact_lhs([dt3_ref[p] for p in range(3)], e_g), cs_x, cs_x[T - 1:T, :],
            lambda h: cst_ref[pl.ds(h0 + h, 1), :],
            lambda s, h: jnp.exp(cst_ref[pl.ds(h0 + h, 1), T - 1:T]),
            [st_ref[rows, :]], 1, dsk_ref[g], nw_ref[g], keep)
        yn_ref[g] = yn
        st_ref[rows, :] = new_states[0]

    for g in range(SSD_GROUPS):
        group(g)


def _ssd_sample_body(z_ref, xs_ref, b_ref, c_ref, px_ref, pb_ref, pc_ref, dt_ref, wx_ref, wb_ref, wc_ref,
                     bx_ref, bb_ref, bc_ref, dtb_ref, alog_ref, dsk_ref, nw_ref, m_ref, mseg_ref, e_ref,
                     st_in_ref, yn_in_ref, yn_ref, st_ref, cst_ref, cet_ref, *, seq_len):
    del yn_in_ref
    g = pl.program_id(1)
    n_seg = T // seq_len

    def conv(x_ref, p_ref, w_ref, bias_ref):
        x = x_ref[...]
        p = p_ref[...]
        tpos = lax.broadcasted_iota(jnp.int32, x.shape, 0) % seq_len
        acc = bias_ref[...] + x * w_ref[SSD_CONV - 1:SSD_CONV, :]
        for sh in range(1, SSD_CONV):
            shifted = jnp.where(tpos >= sh, pltpu.roll(x, sh, 0), pltpu.roll(p, T - seq_len + sh, 0))
            acc = acc + shifted * w_ref[SSD_CONV - 1 - sh:SSD_CONV - sh, :]
        return _silu(acc)

    xs = conv(xs_ref, px_ref, wx_ref, bx_ref)
    bm = conv(b_ref, pb_ref, wb_ref, bb_ref)
    cm = conv(c_ref, pc_ref, wc_ref, bc_ref)

    dt = jax.nn.softplus(dt_ref[...] + dtb_ref[...])
    a3 = _split3(dt * (-jnp.exp(alog_ref[...])))
    cs = _dot_exact_rhs(m_ref[...], a3)
    cs_end = _dot_exact_rhs(mseg_ref[...], a3)
    cst_ref[...] = cs.T
    cet_ref[...] = cs_end.T
    e_g = e_ref[...]
    h0 = g * GROUP_HEADS
    yn, new_states = _ssd_group(
        xs, bm, cm, z_ref[...], _dot_exact_lhs(_split3(dt), e_g), _dot_exact_lhs(_split3(cs), e_g),
        _dot_exact_lhs(_split3(cs_end), e_g),
        lambda h: cst_ref[pl.ds(h0 + h, 1), :],
        lambda s, h: jnp.exp(cet_ref[pl.ds(h0 + h, 1), s * seq_len:s * seq_len + 1]),
        [st_in_ref[s] for s in range(n_seg)], n_seg, dsk_ref[...], nw_ref[...], _causal_keep(seq_len))
    yn_ref[...] = yn
    for s in range(n_seg):
        st_ref[s] = new_states[s]


def _ssd_masks(seq_len):
    r = jnp.arange(T)
    same = (r[:, None] // seq_len) == (r[None, :] // seq_len)
    return jnp.logical_and(same, r[None, :] <= r[:, None]).astype(BF16), same.astype(BF16)


def _ssd_scan_prompt(zx, dt_raw, conv_w, conv_b, dtb, alog, dsk, nw, n_pseq, seq_len_p):
    _, m_all, _ = zx.shape
    heads = dt_raw.shape[1]
    inner = SSD_GROUPS * GROUP_W
    n_chunks = seq_len_p // T
    n_bc = 2 * SSD_GROUPS // BC_PER_BLOCK
    n_cblk = SSD_GROUPS + n_bc
    cw = conv_w.reshape(SSD_CONV, n_cblk, GROUP_W).transpose(1, 0, 2)
    cbias = conv_b.reshape(n_cblk, 1, GROUP_W)
    e_grp = (jnp.arange(GROUP_W)[None, None, :] // SSD_HEAD_DIM + GROUP_HEADS * jnp.arange(SSD_GROUPS)[:, None, None]
             == jnp.arange(heads)[None, :, None]).astype(BF16)
    m_p, _ = _ssd_masks(T)
    yn_shape = jax.ShapeDtypeStruct((SSD_GROUPS, m_all, GROUP_W), BF16)

    rp = lambda b, c: b * n_chunks + c
    full = lambda shape: pl.BlockSpec(shape, lambda b, c: (0,) * len(shape))
    yn, st_p = pl.pallas_call(
        _ssd_prompt_body,
        grid=(n_pseq, n_chunks),
        in_specs=[
            pl.BlockSpec((SSD_GROUPS, T, GROUP_W), lambda b, c: (0, rp(b, c), 0)),
            pl.BlockSpec((SSD_GROUPS, T, GROUP_W), lambda b, c: (1, rp(b, c), 0)),
            pl.BlockSpec((n_bc, T, GROUP_W), lambda b, c: (2 * SSD_GROUPS // n_bc, rp(b, c), 0)),
            pl.BlockSpec((T, heads), lambda b, c: (rp(b, c), 0)),
            full((n_cblk, SSD_CONV, GROUP_W)), full((n_cblk, 1, GROUP_W)), full((1, heads)), full((1, heads)),
            full((SSD_GROUPS, 1, GROUP_W)), full((SSD_GROUPS, 1, GROUP_W)), full((T, T)),
            full((SSD_GROUPS, heads, GROUP_W)),
        ],
        out_specs=[pl.BlockSpec((SSD_GROUPS, T, GROUP_W), lambda b, c: (0, rp(b, c), 0)),
                   pl.BlockSpec((None, inner, SSD_STATE), lambda b, c: (b, 0, 0))],
        out_shape=[yn_shape, jax.ShapeDtypeStruct((n_pseq, inner, SSD_STATE), F32)],
        scratch_shapes=[pltpu.VMEM((n_cblk, SUBLANES + T, GROUP_W), F32),
                        pltpu.VMEM((SSD_GROUPS, T, GROUP_W), F32),
                        pltpu.VMEM((2 * SSD_GROUPS, T, SSD_STATE), F32),
                        pltpu.VMEM((3, T, heads), BF16),
                        pltpu.VMEM((3, T, heads), BF16),
                        pltpu.VMEM((heads, T), F32)],
        compiler_params=_cparams(("arbitrary", "arbitrary")),
        name="ssd_scan_prompt",
    )(zx, zx, zx, dt_raw, cw, cbias, dtb, alog, dsk, nw, m_p, e_grp)
    return yn, st_p


def _ssd_scan_sample(zx, yn, dt_raw, conv_w, conv_b, dtb, alog, dsk, nw, state_s, conv_pad_s, row0, seq_len_s):
    heads = dt_raw.shape[1]
    inner = SSD_GROUPS * GROUP_W
    n_seg = T // seq_len_s
    n_sseq = state_s.shape[0]
    e_all = (jnp.arange(inner)[None, :] // SSD_HEAD_DIM == jnp.arange(heads)[:, None]).astype(BF16)
    m_s, mseg_s = _ssd_masks(seq_len_s)
    xoff = SSD_GROUPS
    boff = 2 * SSD_GROUPS
    coff = boff + SSD_GROUPS // BC_PER_BLOCK
    cb0 = inner // SSD_STATE
    cc0 = cb0 + SSD_GROUPS
    full = lambda shape: pl.BlockSpec(shape, lambda t, g: (0,) * len(shape))
    per_g = lambda shape: pl.BlockSpec(shape, lambda t, g: (g,) + (0,) * (len(shape) - 1))
    yn, st_s = pl.pallas_call(
        functools.partial(_ssd_sample_body, seq_len=seq_len_s),
        grid=(n_sseq // n_seg, SSD_GROUPS),
        in_specs=[
            pl.BlockSpec((None, T, GROUP_W), lambda t, g: (g, row0 + t, 0)),
            pl.BlockSpec((None, T, GROUP_W), lambda t, g: (xoff + g, row0 + t, 0)),
            pl.BlockSpec((None, T, SSD_STATE), lambda t, g: (boff + g // BC_PER_BLOCK, row0 + t, g % BC_PER_BLOCK)),
            pl.BlockSpec((None, T, SSD_STATE), lambda t, g: (coff + g // BC_PER_BLOCK, row0 + t, g % BC_PER_BLOCK)),
            pl.BlockSpec((T, GROUP_W), lambda t, g: (t, g)),
            pl.BlockSpec((T, SSD_STATE), lambda t, g: (t, cb0 + g)),
            pl.BlockSpec((T, SSD_STATE), lambda t, g: (t, cc0 + g)),
            pl.BlockSpec((T, heads), lambda t, g: (row0 + t, 0)),
            pl.BlockSpec((SSD_CONV, GROUP_W), lambda t, g: (0, g)),
            pl.BlockSpec((SSD_CONV, SSD_STATE), lambda t, g: (0, cb0 + g)),
            pl.BlockSpec((SSD_CONV, SSD_STATE), lambda t, g: (0, cc0 + g)),
            pl.BlockSpec((1, GROUP_W), lambda t, g: (0, g)),
            pl.BlockSpec((1, SSD_STATE), lambda t, g: (0, cb0 + g)),
            pl.BlockSpec((1, SSD_STATE), lambda t, g: (0, cc0 + g)),
            full((1, heads)), full((1, heads)),
            per_g((None, 1, GROUP_W)), per_g((None, 1, GROUP_W)),
            full((T, T)), full((T, T)),
            pl.BlockSpec((heads, GROUP_W), lambda t, g: (0, g)),
            pl.BlockSpec((n_seg, GROUP_W, SSD_STATE), lambda t, g: (t, g, 0)),
            pl.BlockSpec(memory_space=pl.ANY),
        ],
        out_specs=[pl.BlockSpec((None, T, GROUP_W), lambda t, g: (g, row0 + t, 0)),
                   pl.BlockSpec((n_seg, GROUP_W, SSD_STATE), lambda t, g: (t, g, 0))],
        out_shape=[jax.ShapeDtypeStruct(yn.shape, BF16), jax.ShapeDtypeStruct(state_s.shape, F32)],
        scratch_shapes=[pltpu.VMEM((heads, T), F32), pltpu.VMEM((heads, T), F32)],
        input_output_aliases={22: 0},
        compiler_params=_cparams(("arbitrary", "arbitrary")),
        name="ssd_scan_sample",
    )(zx, zx, zx, zx, conv_pad_s, conv_pad_s, conv_pad_s, dt_raw, conv_w, conv_w, conv_w, conv_b, conv_b, conv_b,
      dtb, alog, dsk, nw, m_s, mseg_s, e_all, state_s, yn)
    return yn, st_s


def _final_body(x_ref, w_ref, yp_ref, ys_ref, *, n_ptiles):
    i = pl.program_id(0)
    x = x_ref[...]
    y = x * lax.rsqrt(jnp.mean(x * x, axis=-1, keepdims=True) + NORM_EPS) * w_ref[...]

    @pl.when(i < n_ptiles)
    def _():
        yp_ref[...] = y

    @pl.when(i >= n_ptiles)
    def _():
        ys_ref[...] = y


def _final_norm(x, w, n_prompt_rows):
    m, d = x.shape
    tm = 512
    n_ptiles = n_prompt_rows // tm
    return pl.pallas_call(
        functools.partial(_final_body, n_ptiles=n_ptiles),
        grid=(m // tm,),
        in_specs=[pl.BlockSpec((tm, d), lambda i: (i, 0)), pl.BlockSpec((1, d), lambda i: (0, 0))],
        out_specs=[pl.BlockSpec((tm, d), lambda i: (jnp.minimum(i, n_ptiles - 1), 0)),
                   pl.BlockSpec((tm, d), lambda i: (jnp.maximum(i - n_ptiles, 0), 0))],
        out_shape=[jax.ShapeDtypeStruct((n_prompt_rows, d), F32), jax.ShapeDtypeStruct((m - n_prompt_rows, d), F32)],
        compiler_params=_cparams(("arbitrary",)),
        name="final_norm",
    )(x, w.reshape(1, d))


def kernel(x_prompt, x_sample, c_prompt, c_sample, state_ssm, state_conv, mod_w, mod_b, norm_mix_w, norm_ffn_w,
           a_w_in, a_b_in, a_ln_w, a_ln_b, a_w_s, a_b_s, a_w_out,
           b_w_in, b_conv_w, b_conv_b, b_dt_bias, b_a_log, b_d, b_norm_w, b_w_out,
           f_w_in, f_w_out, final_norm_w):
    bp, lp, d = x_prompt.shape
    bs, ls, _ = x_sample.shape
    depth = mod_w.shape[0]
    n_prompt = bp * lp
    n_sample = bs * ls
    m_all = n_prompt + n_sample
    assert lp % TM == 0 and n_sample % TM == 0 and TM % ls == 0 and bp <= SUBLANES
    assert ls >= SSD_CONV - 1 and T % ls == 0 and lp % T == 0 and GMLP_CHUNK % ls == 0 and lp % MIX_ROWS == 0
    assert depth == 2 and a_w_in.shape[0] == 1 and b_w_in.shape[0] == 1
    cfg = Cfg(n_ptiles=n_prompt // TM, tiles_per_seq=lp // TM, seq_len_s=ls, srow0=SUBLANES)

    x_p = x_prompt.reshape(n_prompt, d)
    x_s = x_sample.reshape(n_sample, d)
    c_all = jnp.concatenate([c_prompt, jnp.zeros((SUBLANES - bp, d), F32), c_sample], axis=0)
    mods = _mod_table(c_all, mod_w, mod_b)

    u, v = _gmlp_in(x_p, x_s, mods, 0, norm_mix_w, a_w_in, a_b_in, 0, cfg)
    r = jnp.arange(GMLP_CHUNK)
    tril = r[None, :] <= r[:, None]
    mask = jnp.stack([tril, jnp.logical_and(tril, (r[:, None] // ls) == (r[None, :] // ls))]).astype(F32)
    rep = GMLP_CHUNK // ls
    wmix = jnp.stack([a_w_s[0], jnp.tile(a_w_s[0, :, :ls, :ls], (1, rep, rep))])
    width = a_w_in.shape[-1] // 2
    bias = jnp.stack([jnp.repeat(a_b_s[0].T, width // GMLP_GROUPS, axis=1),
                      jnp.repeat(jnp.tile(a_b_s[0, :, :ls].T, (rep, 1)), width // GMLP_GROUPS, axis=1)])
    gated, v_p, v_s = _gmlp_mix(u, v, a_ln_w[0], a_ln_b[0], wmix, mask, bias, n_prompt, lp, bp)
    x = _resid_matmul(gated, a_w_out, 0, x_p, x_s, mods, 0, 2, TN, cfg, "gmlp_out")
    act = _ffn_in(x, mods, 0, norm_ffn_w, f_w_in, cfg)
    x = _resid_matmul(act, f_w_out, 0, x, None, mods, 0, 5, TN_NARROW, cfg, "ffn_out0")

    inner = b_w_out.shape[1]
    conv_dim = b_conv_w.shape[-1]
    n_main = inner + conv_dim
    heads = inner // SSD_HEAD_DIM
    zx, dt_raw = _ssd_in(x, mods, 1, norm_mix_w, jnp.swapaxes(b_w_in, 1, 2)[0], n_main, cfg)
    conv_pad = jnp.pad(state_conv[0], ((0, 0), (ls - (SSD_CONV - 1), 0), (0, 0))).reshape(n_sample, conv_dim)
    ssd_params = (b_conv_w[0], b_conv_b, b_dt_bias, b_a_log,
                  jnp.repeat(b_d[0], SSD_HEAD_DIM).reshape(SSD_GROUPS, 1, GROUP_W),
                  b_norm_w.reshape(SSD_GROUPS, 1, GROUP_W))
    state_s = state_ssm[0].reshape(bs, inner, SSD_STATE)
    yn, ssm_p = _ssd_scan_prompt(zx, dt_raw, *ssd_params, bp, lp)
    yn, ssm_s = _ssd_scan_sample(zx, yn, dt_raw, *ssd_params, state_s, conv_pad, n_prompt // T, ls)
    x = _resid_matmul(yn, b_w_out, 0, x, None, mods, 1, 2, TN, cfg, "ssd_out")
    act = _ffn_in(x, mods, 1, norm_ffn_w, f_w_in, cfg)
    x = _resid_matmul(act, f_w_out, 1, x, None, mods, 1, 5, TN_NARROW, cfg, "ffn_out1")

    y_p, y_s = _final_norm(x, final_norm_w, n_prompt)

    zx4 = zx.reshape(zx.shape[0], m_all // ls, ls, TN)

    def tails(groups):
        t = groups[SSD_GROUPS:, :, ls - (SSD_CONV - 1):, :]
        return jnp.moveaxis(t, 0, 2).reshape(t.shape[1], SSD_CONV - 1, conv_dim)
    conv_p = tails(zx4[:, lp // ls - 1:n_prompt // ls:lp // ls])
    conv_s = tails(zx4[:, n_prompt // ls:])
    return (y_p.reshape(bp, lp, d), y_s.reshape(bs, ls, d),
            v_p.reshape(1, bp, GMLP_CHUNK, width), v_s.reshape(1, bs, ls, width),
            ssm_p.reshape(1, bp, heads, SSD_HEAD_DIM, SSD_STATE), ssm_s.reshape(1, bs, heads, SSD_HEAD_DIM, SSD_STATE),
            conv_p[None], conv_s[None])
```

```python
import functools
import math
from typing import NamedTuple

import jax
import jax.numpy as jnp
from jax import lax
from jax.experimental import pallas as pl
from jax.experimental.pallas import tpu as pltpu

F32 = jnp.float32
BF16 = jnp.bfloat16

NORM_EPS = 1e-6
LN_EPS = 1e-5

GMLP_GROUPS = 16
GMLP_CHUNK = 128
SSD_HEAD_DIM = 64
SSD_STATE = 128
SSD_GROUPS = 8
SSD_CONV = 4
SSD_CHUNK = 128

SUBLANES = 8
LANES = 128
VMEM_LIMIT_BYTES = 56 * 1024 * 1024

TM = 1024
TN = 512
TN_NARROW = 256
ROW_CHUNK = 256


class Cfg(NamedTuple):
    n_ptiles: int
    tiles_per_seq: int
    seq_len_s: int
    srow0: int


def _cparams(sem):
    return pltpu.CompilerParams(dimension_semantics=sem, vmem_limit_bytes=VMEM_LIMIT_BYTES)


def _silu(x):
    return x / (1.0 + jnp.exp(-x))


def _gelu(x):
    return 0.5 * x * (1.0 + lax.erf(x * (1.0 / math.sqrt(2.0))))


def _rms_mod(x, w, scale, shift):
    y = x * lax.rsqrt(jnp.mean(x * x, axis=-1, keepdims=True) + NORM_EPS)
    return (y * w) * (1.0 + scale) + shift


def _chunk_rows(c):
    return pl.ds(pl.multiple_of(c * ROW_CHUNK, ROW_CHUNK), ROW_CHUNK)


def _sample_rows(i, cfg, x_refs, mod_refs, fn, o_ref):
    rows = o_ref.shape[0]
    nseq = ROW_CHUNK // cfg.seq_len_s
    row0 = cfg.srow0 + (i - cfg.n_ptiles) * (rows // cfg.seq_len_s)

    def body(c, carry):
        rs = _chunk_rows(c)
        r0 = pl.multiple_of(row0 + c * nseq, SUBLANES)
        ms = [m[pl.ds(r0, nseq), :][:, None, :] for m in mod_refs]
        x3 = [x[rs, :].reshape(nseq, cfg.seq_len_s, x.shape[-1]) for x in x_refs]
        o_ref[rs, :] = fn(x3, ms).reshape(ROW_CHUNK, o_ref.shape[-1]).astype(o_ref.dtype)
        return carry
    lax.fori_loop(0, rows // ROW_CHUNK, body, 0)


def _per_seq(i, cfg, xp_refs, xs_refs, mod_refs, fn, o_ref):
    @pl.when(i < cfg.n_ptiles)
    def _():
        s = i // cfg.tiles_per_seq
        ms = [m[pl.ds(s, 1), :] for m in mod_refs]

        def body(c, carry):
            rs = _chunk_rows(c)
            o_ref[rs, :] = fn([x[rs, :] for x in xp_refs], ms).astype(o_ref.dtype)
            return carry
        lax.fori_loop(0, o_ref.shape[0] // ROW_CHUNK, body, 0)

    @pl.when(i >= cfg.n_ptiles)
    def _():
        _sample_rows(i, cfg, xs_refs, mod_refs, fn, o_ref)


def _prompt_block(cfg):
    return lambda i: jnp.minimum(i, cfg.n_ptiles - 1)


def _sample_block(cfg):
    return lambda i: jnp.maximum(i - cfg.n_ptiles, 0)


def _mod_body(c_ref, w_ref, b_ref, o_ref):
    sc = _silu(c_ref[...]).astype(BF16)
    o_ref[...] = jnp.dot(sc, w_ref[...].astype(BF16), preferred_element_type=F32) + b_ref[...]


def _mod_table(c_all, mod_w, mod_b):
    depth, d, n = mod_w.shape
    r = c_all.shape[0]
    tn = 1024
    return pl.pallas_call(
        _mod_body,
        grid=(depth, n // tn),
        in_specs=[
            pl.BlockSpec((r, d), lambda l, j: (0, 0)),
            pl.BlockSpec((None, d, tn), lambda l, j: (l, 0, j)),
            pl.BlockSpec((None, 1, tn), lambda l, j: (l, 0, j)),
        ],
        out_specs=pl.BlockSpec((None, r, tn), lambda l, j: (l, 0, j)),
        out_shape=jax.ShapeDtypeStruct((depth, r, n), F32),
        compiler_params=_cparams(("arbitrary", "arbitrary")),
        name="mod_table",
    )(c_all, mod_w, mod_b.reshape(depth, 1, n))


def _norm_prologue(i, j, cfg, xp_ref, xs_ref, nw_ref, sc_ref, sh_ref, h_ref):
    @pl.when(j == 0)
    def _():
        _per_seq(i, cfg, [xp_ref], [xs_ref], [sc_ref, sh_ref],
                 lambda xs, ms: _rms_mod(xs[0], nw_ref[...], ms[0], ms[1]), h_ref)


def _gmlp_in_body(xp_ref, xs_ref, nw_ref, sh_ref, sc_ref, wu_ref, wv_ref, bu_ref, bv_ref, u_ref, v_ref, h_ref, *, cfg):
    i, j = pl.program_id(0), pl.program_id(1)
    _norm_prologue(i, j, cfg, xp_ref, xs_ref, nw_ref, sc_ref, sh_ref, h_ref)
    h = h_ref[...]
    u_ref[...] = _gelu(jnp.dot(h, wu_ref[...].astype(BF16), preferred_element_type=F32) + bu_ref[...])
    v_ref[...] = _gelu(jnp.dot(h, wv_ref[...].astype(BF16), preferred_element_type=F32) + bv_ref[...])


def _gmlp_in(x_p, x_s, mods, layer, norm_w, w_in, b_in, j_layer, cfg):
    d = x_p.shape[1]
    m = x_p.shape[0] + x_s.shape[0]
    width = w_in.shape[-1] // 2
    tn = TN_NARROW
    nj = width // tn
    r = mods.shape[1]
    pb, sb = _prompt_block(cfg), _sample_block(cfg)
    b2 = b_in.reshape(b_in.shape[0], 1, -1)
    return pl.pallas_call(
        functools.partial(_gmlp_in_body, cfg=cfg),
        grid=(m // TM, nj),
        in_specs=[
            pl.BlockSpec((TM, d), lambda i, j: (pb(i), 0)),
            pl.BlockSpec((TM, d), lambda i, j: (sb(i), 0), pipeline_mode=pl.Buffered(1)),
            pl.BlockSpec((None, 1, d), lambda i, j: (layer, 0, 0)),
            pl.BlockSpec((None, r, d), lambda i, j: (layer, 0, 0)),
            pl.BlockSpec((None, r, d), lambda i, j: (layer, 0, 1)),
            pl.BlockSpec((None, d, tn), lambda i, j: (j_layer, 0, j)),
            pl.BlockSpec((None, d, tn), lambda i, j: (j_layer, 0, j + nj)),
            pl.BlockSpec((None, 1, tn), lambda i, j: (j_layer, 0, j)),
            pl.BlockSpec((None, 1, tn), lambda i, j: (j_layer, 0, j + nj)),
        ],
        out_specs=[pl.BlockSpec((TM, tn), lambda i, j: (i, j)), pl.BlockSpec((TM, tn), lambda i, j: (i, j))],
        out_shape=[jax.ShapeDtypeStruct((m, width), F32), jax.ShapeDtypeStruct((m, width), F32)],
        scratch_shapes=[pltpu.VMEM((TM, d), BF16)],
        compiler_params=_cparams(("arbitrary", "arbitrary")),
        name="gmlp_in",
    )(x_p, x_s, norm_w.reshape(-1, 1, d), mods, mods, w_in, w_in, b2, b2)


def _ffn_in_body(x_ref, nw_ref, sh_ref, sc_ref, wg_ref, wu_ref, a_ref, h_ref, *, cfg):
    i, j = pl.program_id(0), pl.program_id(1)
    _norm_prologue(i, j, cfg, x_ref, x_ref, nw_ref, sc_ref, sh_ref, h_ref)
    h = h_ref[...]
    gate = jnp.dot(h, wg_ref[...].astype(BF16), preferred_element_type=F32)
    up = jnp.dot(h, wu_ref[...].astype(BF16), preferred_element_type=F32)
    a_ref[...] = (_silu(gate) * up).astype(BF16)


def _ffn_in(x, mods, layer, norm_w, w_in, cfg):
    m, d = x.shape
    hidden = w_in.shape[-1] // 2
    nj = hidden // TN
    r = mods.shape[1]
    return pl.pallas_call(
        functools.partial(_ffn_in_body, cfg=cfg),
        grid=(m // TM, nj),
        in_specs=[
            pl.BlockSpec((TM, d), lambda i, j: (i, 0)),
            pl.BlockSpec((None, 1, d), lambda i, j: (layer, 0, 0)),
            pl.BlockSpec((None, r, d), lambda i, j: (layer, 0, 3)),
            pl.BlockSpec((None, r, d), lambda i, j: (layer, 0, 4)),
            pl.BlockSpec((None, d, TN), lambda i, j: (layer, 0, j)),
            pl.BlockSpec((None, d, TN), lambda i, j: (layer, 0, j + nj)),
        ],
        out_specs=pl.BlockSpec((TM, TN), lambda i, j: (i, j)),
        out_shape=jax.ShapeDtypeStruct((m, hidden), BF16),
        scratch_shapes=[pltpu.VMEM((TM, d), BF16)],
        compiler_params=_cparams(("arbitrary", "arbitrary")),
        name="ffn_in",
    )(x, norm_w.reshape(-1, 1, d), mods, mods, w_in, w_in)


_NT = (((1,), (1,)), ((), ()))


def _ssd_in_body(x_ref, nw_ref, sh_ref, sc_ref, w_ref, wdt_ref, o_ref, dt_ref, h_ref, *, cfg):
    i, j = pl.program_id(0), pl.program_id(1)
    _norm_prologue(i, j, cfg, x_ref, x_ref, nw_ref, sc_ref, sh_ref, h_ref)
    h = h_ref[...]
    o_ref[...] = lax.dot_general(h, w_ref[...].astype(BF16), _NT, preferred_element_type=F32)

    @pl.when(j == 0)
    def _():
        dt_ref[...] = lax.dot_general(h, wdt_ref[...].astype(BF16), _NT, preferred_element_type=F32)


def _ssd_in(x, mods, layer, norm_w, w_in_t, n_main, cfg):
    m, d = x.shape
    r = mods.shape[1]
    n_dt = w_in_t.shape[0] - n_main
    nj = n_main // TN
    return pl.pallas_call(
        functools.partial(_ssd_in_body, cfg=cfg),
        grid=(m // TM, nj),
        in_specs=[
            pl.BlockSpec((TM, d), lambda i, j: (i, 0)),
            pl.BlockSpec((None, 1, d), lambda i, j: (layer, 0, 0)),
            pl.BlockSpec((None, r, d), lambda i, j: (layer, 0, 0)),
            pl.BlockSpec((None, r, d), lambda i, j: (layer, 0, 1)),
            pl.BlockSpec((TN, d), lambda i, j: (j, 0)),
            pl.BlockSpec((n_dt, d), lambda i, j: (n_main // n_dt, 0)),
        ],
        out_specs=[pl.BlockSpec((None, TM, TN), lambda i, j: (j, i, 0)), pl.BlockSpec((TM, n_dt), lambda i, j: (i, 0))],
        out_shape=[jax.ShapeDtypeStruct((nj, m, TN), F32), jax.ShapeDtypeStruct((m, n_dt), F32)],
        scratch_shapes=[pltpu.VMEM((TM, d), BF16)],
        compiler_params=_cparams(("arbitrary", "arbitrary")),
        name="ssd_in",
    )(x, norm_w.reshape(-1, 1, d), mods, mods, w_in_t, w_in_t)


def _resid_body(a_ref, w_ref, xp_ref, xs_ref, g_ref, o_ref, acc_ref, *, cfg, nk):
    i = pl.program_id(0)

    def matmul():
        if nk == 1:
            return jnp.dot(a_ref[...], w_ref[...].astype(BF16), preferred_element_type=F32)
        tk = w_ref.shape[0] // nk
        acc = jnp.dot(a_ref[0], w_ref[0:tk, :].astype(BF16), preferred_element_type=F32)
        for k in range(1, nk):
            acc = acc + jnp.dot(a_ref[k], w_ref[k * tk:(k + 1) * tk, :].astype(BF16), preferred_element_type=F32)
        return acc

    @pl.when(i < cfg.n_ptiles)
    def _():
        s = i // cfg.tiles_per_seq
        o_ref[...] = xp_ref[...] + g_ref[pl.ds(s, 1), :] * matmul()

    @pl.when(i >= cfg.n_ptiles)
    def _():
        acc_ref[...] = matmul()
        _sample_rows(i, cfg, [xs_ref, acc_ref], [g_ref], lambda xs, ms: xs[0] + ms[0] * xs[1], o_ref)


def _resid_matmul(a, w, w_layer, x_p, x_s, mods, layer, gate_chunk, tn, cfg, name):
    nk = 1 if a.ndim == 2 else a.shape[0]
    m = a.shape[-2]
    kdim = a.shape[-1] * nk
    d = x_p.shape[1]
    r = mods.shape[1]
    goff = gate_chunk * (d // tn)
    if x_s is None:
        x_s = x_p
        xp_spec = xs_spec = pl.BlockSpec((TM, tn), lambda i, j: (i, j))
    else:
        pb, sb = _prompt_block(cfg), _sample_block(cfg)
        xp_spec = pl.BlockSpec((TM, tn), lambda i, j: (pb(i), j))
        xs_spec = pl.BlockSpec((TM, tn), lambda i, j: (sb(i), j))
    if nk == 1:
        a_spec = pl.BlockSpec((TM, kdim), lambda i, j: (i, 0))
    else:
        a_spec = pl.BlockSpec((nk, TM, kdim // nk), lambda i, j: (0, i, 0))
    return pl.pallas_call(
        functools.partial(_resid_body, cfg=cfg, nk=nk),
        grid=(m // TM, d // tn),
        in_specs=[
            a_spec,
            pl.BlockSpec((None, kdim, tn), lambda i, j: (w_layer, 0, j)),
            xp_spec,
            xs_spec,
            pl.BlockSpec((None, r, tn), lambda i, j: (layer, 0, goff + j)),
        ],
        out_specs=pl.BlockSpec((TM, tn), lambda i, j: (i, j)),
        out_shape=jax.ShapeDtypeStruct((m, d), F32),
        scratch_shapes=[pltpu.VMEM((TM, tn), F32)],
        compiler_params=_cparams(("arbitrary", "arbitrary")),
        name=name,
    )(a, w, x_p, x_s, mods)


MIX_ROWS = 2 * GMLP_CHUNK


def _gmlp_mix_body(u_ref, v_ref, lnw_ref, lnb_ref, ws_ref, mask_ref, bias_ref, g_ref, vp_ref, vs_ref,
                   vn_ref, *, n_prompt_steps, steps_per_seq):
    t = pl.program_id(0)
    v = v_ref[...]
    xc = v - jnp.mean(v, axis=-1, keepdims=True)
    vn = xc * lax.rsqrt(jnp.mean(xc * xc, axis=-1, keepdims=True) + LN_EPS) * lnw_ref[...] + lnb_ref[...]
    vn_ref[...] = vn

    @pl.when(jnp.logical_and(t < n_prompt_steps, t % steps_per_seq == steps_per_seq - 1))
    def _():
        vp_ref[...] = vn[MIX_ROWS - GMLP_CHUNK:, :]

    @pl.when(t >= n_prompt_steps)
    def _():
        vs_ref[...] = vn

    mask = mask_ref[...]
    for g in range(GMLP_GROUPS):
        wb = (ws_ref[g] * mask).astype(BF16)
        cols = slice(g * GMLP_CHUNK, (g + 1) * GMLP_CHUNK)
        for c in range(MIX_ROWS // GMLP_CHUNK):
            rows = slice(c * GMLP_CHUNK, (c + 1) * GMLP_CHUNK)
            s = jnp.dot(wb, vn_ref[rows, cols].astype(BF16), preferred_element_type=F32) + bias_ref[:, cols]
            g_ref[rows, cols] = (u_ref[rows, cols] * s).astype(BF16)


def _gmlp_mix(u, v, ln_w, ln_b, wmix, mask, bias, n_prompt_rows, seq_len, n_prompt_seq):
    m, width = u.shape
    n_prompt_steps = n_prompt_rows // MIX_ROWS
    steps_per_seq = seq_len // MIX_ROWS
    n_sample_rows = m - n_prompt_rows

    def variant(t):
        return jnp.where(t < n_prompt_steps, 0, 1)

    return pl.pallas_call(
        functools.partial(_gmlp_mix_body, n_prompt_steps=n_prompt_steps, steps_per_seq=steps_per_seq),
        grid=(m // MIX_ROWS,),
        in_specs=[
            pl.BlockSpec((MIX_ROWS, width), lambda t: (t, 0)),
            pl.BlockSpec((MIX_ROWS, width), lambda t: (t, 0)),
            pl.BlockSpec((1, width), lambda t: (0, 0)),
            pl.BlockSpec((1, width), lambda t: (0, 0)),
            pl.BlockSpec((None, GMLP_GROUPS, GMLP_CHUNK, GMLP_CHUNK), lambda t: (variant(t), 0, 0, 0)),
            pl.BlockSpec((None, GMLP_CHUNK, GMLP_CHUNK), lambda t: (variant(t), 0, 0)),
            pl.BlockSpec((None, GMLP_CHUNK, width), lambda t: (variant(t), 0, 0)),
        ],
        out_specs=[
            pl.BlockSpec((MIX_ROWS, width), lambda t: (t, 0)),
            pl.BlockSpec((GMLP_CHUNK, width), lambda t: (jnp.minimum(t // steps_per_seq, n_prompt_seq - 1), 0)),
            pl.BlockSpec((MIX_ROWS, width), lambda t: (jnp.maximum(t - n_prompt_steps, 0), 0)),
        ],
        out_shape=[
            jax.ShapeDtypeStruct((m, width), BF16),
            jax.ShapeDtypeStruct((n_prompt_seq * GMLP_CHUNK, width), F32),
            jax.ShapeDtypeStruct((n_sample_rows, width), F32),
        ],
        scratch_shapes=[pltpu.VMEM((MIX_ROWS, width), F32)],
        compiler_params=_cparams(("arbitrary",)),
        name="gmlp_mix",
    )(u, v, ln_w.reshape(1, width), ln_b.reshape(1, width), wmix, mask, bias)


GROUP_HEADS = 8
GROUP_W = GROUP_HEADS * SSD_HEAD_DIM
T = SSD_CHUNK
BC_PER_BLOCK = GROUP_W // SSD_STATE


def _split3(x):
    hi = x.astype(BF16)
    r = x - hi.astype(F32)
    mid = r.astype(BF16)
    lo = (r - mid.astype(F32)).astype(BF16)
    return hi, mid, lo


def _dot_exact_rhs(m_b, pieces):
    return sum(jnp.dot(m_b, p, preferred_element_type=F32) for p in pieces)


def _dot_exact_lhs(pieces, e_b):
    return sum(jnp.dot(p, e_b, preferred_element_type=F32) for p in pieces)


def _ssd_group(xs, bm, cm, z, dt_x, cs_x, cs_end_x, cs_row, d_a, states, n_seg, dsk, nw, keep):
    seg = T // n_seg
    xdt = xs * dt_x
    ecs_x = jnp.exp(cs_x)
    dte_x = jnp.exp(cs_end_x - cs_x)
    xdt_b = xdt.astype(BF16)
    xd_t = (xdt * dte_x).T.astype(BF16)
    bmb = bm.astype(BF16)
    cmb = cm.astype(BF16)
    cb = lax.dot_general(cmb, bmb, _NT, preferred_element_type=F32)

    row = lax.broadcasted_iota(jnp.int32, (T, SSD_STATE), 0)
    y_offs, new_states = [], []
    for s in range(n_seg):
        st = states[s]
        c_seg = cmb if n_seg == 1 else cm[s * seg:(s + 1) * seg].astype(BF16)
        y_offs.append(lax.dot_general(c_seg, st.astype(BF16), _NT, preferred_element_type=F32))
        if n_seg == 1:
            b_seg = bmb
        else:
            b_seg = jnp.where(jnp.logical_and(row >= s * seg, row < (s + 1) * seg), bm, 0.0).astype(BF16)
        upd = jnp.dot(xd_t, b_seg, preferred_element_type=F32)
        decayed = jnp.concatenate(
            [st[h * SSD_HEAD_DIM:(h + 1) * SSD_HEAD_DIM, :] * d_a(s, h) for h in range(GROUP_HEADS)], axis=0)
        new_states.append(decayed + upd)
    y_off = jnp.concatenate(y_offs, axis=0) if n_seg > 1 else y_offs[0]

    lane = lax.broadcasted_iota(jnp.int32, (T, LANES), 1)
    lo_half = lane < SSD_HEAD_DIM
    ys = []
    for q in range(GROUP_HEADS // 2):
        v = cs_x[:, q * LANES:(q + 1) * LANES]
        r = pltpu.roll(v, SSD_HEAD_DIM, 1)
        cols = (jnp.where(lo_half, v, r), jnp.where(lo_half, r, v))
        ws = []
        for e in range(2):
            diff = cols[e] - cs_row(2 * q + e)
            ws.append((cb * jnp.exp(jnp.where(keep, diff, -jnp.inf))).astype(BF16))
        xp = xdt_b[:, q * LANES:(q + 1) * LANES]
        zero = jnp.zeros_like(xp)
        xpair = jnp.concatenate([jnp.where(lo_half, xp, zero), jnp.where(lo_half, zero, xp)], axis=0)
        ys.append(jnp.dot(jnp.concatenate(ws, axis=1), xpair, preferred_element_type=F32))
    y_diag = jnp.concatenate(ys, axis=1)

    y = y_diag + y_off * ecs_x + dsk * xs
    y = y * _silu(z)
    yn = y * lax.rsqrt(jnp.mean(y * y, axis=-1, keepdims=True) + NORM_EPS) * nw
    return yn.astype(BF16), new_states


def _causal_keep(seq_len):
    r = lax.broadcasted_iota(jnp.int32, (T, T), 0)
    c = lax.broadcasted_iota(jnp.int32, (T, T), 1)
    keep = c <= r
    if seq_len < T:
        keep = jnp.logical_and(keep, (r // seq_len) == (c // seq_len))
    return keep


def _ssd_prompt_body(z_ref, xs_ref, bc_ref, dt_ref, cw_ref, cbias_ref, dtb_ref, alog_ref, dsk_ref, nw_ref,
                     m_ref, e_ref, yn_ref, st_ref,
                     cbuf_ref, xc_ref, bcs_ref, dt3_ref, cs3_ref, cst_ref):
    c = pl.program_id(1)
    n_x = xs_ref.shape[0]
    pad = SUBLANES
    taps = SSD_CONV - 1

    @pl.when(c == 0)
    def _():
        st_ref[...] = jnp.zeros_like(st_ref)
        cbuf_ref[:, 0:pad, :] = jnp.zeros((cbuf_ref.shape[0], pad, GROUP_W), F32)

    def conv(k, src):
        w = cw_ref[k]
        cbuf_ref[k, pad:pad + T, :] = src
        acc = cbias_ref[k] + src * w[taps:taps + 1, :]
        for kk in range(taps):
            acc = acc + cbuf_ref[k, pad - taps + kk:pad - taps + kk + T, :] * w[kk:kk + 1, :]
        cbuf_ref[k, pad - taps:pad, :] = cbuf_ref[k, pad + T - taps:pad + T, :]
        return _silu(acc)

    def conv_x(k, carry):
        xc_ref[k] = conv(k, xs_ref[k])
        return carry
    lax.fori_loop(0, n_x, conv_x, 0)

    def conv_bc(k, carry):
        out = conv(n_x + k, bc_ref[k])
        for gg in range(BC_PER_BLOCK):
            bcs_ref[k * BC_PER_BLOCK + gg] = out[:, gg * SSD_STATE:(gg + 1) * SSD_STATE]
        return carry
    lax.fori_loop(0, bc_ref.shape[0], conv_bc, 0)

    dt = jax.nn.softplus(dt_ref[...] + dtb_ref[...])
    a = dt * (-jnp.exp(alog_ref[...]))
    cs = _dot_exact_rhs(m_ref[...], _split3(a))
    cst_ref[...] = cs.T
    for p, (dt_p, cs_p) in enumerate(zip(_split3(dt), _split3(cs))):
        dt3_ref[p] = dt_p
        cs3_ref[p] = cs_p

    keep = _causal_keep(T)

    def spread(g):
        e_g = e_ref[g]
        return (_dot_exact_lhs([dt3_ref[p] for p in range(3)], e_g), _dot_exact_lhs([cs3_ref[p] for p in range(3)], e_g))

    def group(g, dt_x, cs_x):
        h0 = g * GROUP_HEADS
        rows = pl.ds(g * GROUP_W, GROUP_W)
        yn, new_states = _ssd_group(
            xc_ref[g], bcs_ref[g], bcs_ref[SSD_GROUPS + g], z_ref[g], dt_x, cs_x, cs_x[T - 1:T, :],
            lambda h: cst_ref[pl.ds(h0 + h, 1), :],
            lambda s, h: jnp.exp(cst_ref[pl.ds(h0 + h, 1), T - 1:T]),
            [st_ref[rows, :]], 1, dsk_ref[g], nw_ref[g], keep)
        yn_ref[g] = yn
        st_ref[rows, :] = new_states[0]

    spread_next = spread(0)
    for g in range(SSD_GROUPS):
        dt_x, cs_x = spread_next
        if g + 1 < SSD_GROUPS:
            spread_next = spread(g + 1)
        group(g, dt_x, cs_x)


def _ssd_sample_body(z_ref, xs_ref, b_ref, c_ref, px_ref, pb_ref, pc_ref, dt_ref, wx_ref, wb_ref, wc_ref,
                     bx_ref, bb_ref, bc_ref, dtb_ref, alog_ref, dsk_ref, nw_ref, m_ref, mseg_ref, e_ref,
                     st_in_ref, yn_in_ref, yn_ref, st_ref, cst_ref, cet_ref, *, seq_len):
    del yn_in_ref
    g = pl.program_id(1)
    n_seg = T // seq_len

    def conv(x_ref, p_ref, w_ref, bias_ref):
        x = x_ref[...]
        p = p_ref[...]
        tpos = lax.broadcasted_iota(jnp.int32, x.shape, 0) % seq_len
        acc = bias_ref[...] + x * w_ref[SSD_CONV - 1:SSD_CONV, :]
        for sh in range(1, SSD_CONV):
            shifted = jnp.where(tpos >= sh, pltpu.roll(x, sh, 0), pltpu.roll(p, T - seq_len + sh, 0))
            acc = acc + shifted * w_ref[SSD_CONV - 1 - sh:SSD_CONV - sh, :]
        return _silu(acc)

    xs = conv(xs_ref, px_ref, wx_ref, bx_ref)
    bm = conv(b_ref, pb_ref, wb_ref, bb_ref)
    cm = conv(c_ref, pc_ref, wc_ref, bc_ref)

    dt = jax.nn.softplus(dt_ref[...] + dtb_ref[...])
    a3 = _split3(dt * (-jnp.exp(alog_ref[...])))
    cs = _dot_exact_rhs(m_ref[...], a3)
    cs_end = _dot_exact_rhs(mseg_ref[...], a3)
    cst_ref[...] = cs.T
    cet_ref[...] = cs_end.T
    e_g = e_ref[...]
    h0 = g * GROUP_HEADS
    yn, new_states = _ssd_group(
        xs, bm, cm, z_ref[...], _dot_exact_lhs(_split3(dt), e_g), _dot_exact_lhs(_split3(cs), e_g),
        _dot_exact_lhs(_split3(cs_end), e_g),
        lambda h: cst_ref[pl.ds(h0 + h, 1), :],
        lambda s, h: jnp.exp(cet_ref[pl.ds(h0 + h, 1), s * seq_len:s * seq_len + 1]),
        [st_in_ref[s] for s in range(n_seg)], n_seg, dsk_ref[...], nw_ref[...], _causal_keep(seq_len))
    yn_ref[...] = yn
    for s in range(n_seg):
        st_ref[s] = new_states[s]


def _ssd_masks(seq_len):
    r = jnp.arange(T)
    same = (r[:, None] // seq_len) == (r[None, :] // seq_len)
    return jnp.logical_and(same, r[None, :] <= r[:, None]).astype(BF16), same.astype(BF16)


def _ssd_scan_prompt(zx, dt_raw, conv_w, conv_b, dtb, alog, dsk, nw, n_pseq, seq_len_p):
    _, m_all, _ = zx.shape
    heads = dt_raw.shape[1]
    inner = SSD_GROUPS * GROUP_W
    n_chunks = seq_len_p // T
    n_bc = 2 * SSD_GROUPS // BC_PER_BLOCK
    n_cblk = SSD_GROUPS + n_bc
    cw = conv_w.reshape(SSD_CONV, n_cblk, GROUP_W).transpose(1, 0, 2)
    cbias = conv_b.reshape(n_cblk, 1, GROUP_W)
    e_grp = (jnp.arange(GROUP_W)[None, None, :] // SSD_HEAD_DIM + GROUP_HEADS * jnp.arange(SSD_GROUPS)[:, None, None]
             == jnp.arange(heads)[None, :, None]).astype(BF16)
    m_p, _ = _ssd_masks(T)
    yn_shape = jax.ShapeDtypeStruct((SSD_GROUPS, m_all, GROUP_W), BF16)

    rp = lambda b, c: b * n_chunks + c
    full = lambda shape: pl.BlockSpec(shape, lambda b, c: (0,) * len(shape))
    yn, st_p = pl.pallas_call(
        _ssd_prompt_body,
        grid=(n_pseq, n_chunks),
        in_specs=[
            pl.BlockSpec((SSD_GROUPS, T, GROUP_W), lambda b, c: (0, rp(b, c), 0)),
            pl.BlockSpec((SSD_GROUPS, T, GROUP_W), lambda b, c: (1, rp(b, c), 0)),
            pl.BlockSpec((n_bc, T, GROUP_W), lambda b, c: (2 * SSD_GROUPS // n_bc, rp(b, c), 0)),
            pl.BlockSpec((T, heads), lambda b, c: (rp(b, c), 0)),
            full((n_cblk, SSD_CONV, GROUP_W)), full((n_cblk, 1, GROUP_W)), full((1, heads)), full((1, heads)),
            full((SSD_GROUPS, 1, GROUP_W)), full((SSD_GROUPS, 1, GROUP_W)), full((T, T)),
            full((SSD_GROUPS, heads, GROUP_W)),
        ],
        out_specs=[pl.BlockSpec((SSD_GROUPS, T, GROUP_W), lambda b, c: (0, rp(b, c), 0)),
                   pl.BlockSpec((None, inner, SSD_STATE), lambda b, c: (b, 0, 0))],
        out_shape=[yn_shape, jax.ShapeDtypeStruct((n_pseq, inner, SSD_STATE), F32)],
        scratch_shapes=[pltpu.VMEM((n_cblk, SUBLANES + T, GROUP_W), F32),
                        pltpu.VMEM((SSD_GROUPS, T, GROUP_W), F32),
                        pltpu.VMEM((2 * SSD_GROUPS, T, SSD_STATE), F32),
                        pltpu.VMEM((3, T, heads), BF16),
                        pltpu.VMEM((3, T, heads), BF16),
                        pltpu.VMEM((heads, T), F32)],
        compiler_params=_cparams(("arbitrary", "arbitrary")),
        name="ssd_scan_prompt",
    )(zx, zx, zx, dt_raw, cw, cbias, dtb, alog, dsk, nw, m_p, e_grp)
    return yn, st_p


def _ssd_scan_sample(zx, yn, dt_raw, conv_w, conv_b, dtb, alog, dsk, nw, state_s, conv_pad_s, row0, seq_len_s):
    heads = dt_raw.shape[1]
    inner = SSD_GROUPS * GROUP_W
    n_seg = T // seq_len_s
    n_sseq = state_s.shape[0]
    e_all = (jnp.arange(inner)[None, :] // SSD_HEAD_DIM == jnp.arange(heads)[:, None]).astype(BF16)
    m_s, mseg_s = _ssd_masks(seq_len_s)
    xoff = SSD_GROUPS
    boff = 2 * SSD_GROUPS
    coff = boff + SSD_GROUPS // BC_PER_BLOCK
    cb0 = inner // SSD_STATE
    cc0 = cb0 + SSD_GROUPS
    full = lambda shape: pl.BlockSpec(shape, lambda t, g: (0,) * len(shape))
    per_g = lambda shape: pl.BlockSpec(shape, lambda t, g: (g,) + (0,) * (len(shape) - 1))
    yn, st_s = pl.pallas_call(
        functools.partial(_ssd_sample_body, seq_len=seq_len_s),
        grid=(n_sseq // n_seg, SSD_GROUPS),
        in_specs=[
            pl.BlockSpec((None, T, GROUP_W), lambda t, g: (g, row0 + t, 0)),
            pl.BlockSpec((None, T, GROUP_W), lambda t, g: (xoff + g, row0 + t, 0)),
            pl.BlockSpec((None, T, SSD_STATE), lambda t, g: (boff + g // BC_PER_BLOCK, row0 + t, g % BC_PER_BLOCK)),
            pl.BlockSpec((None, T, SSD_STATE), lambda t, g: (coff + g // BC_PER_BLOCK, row0 + t, g % BC_PER_BLOCK)),
            pl.BlockSpec((T, GROUP_W), lambda t, g: (t, g)),
            pl.BlockSpec((T, SSD_STATE), lambda t, g: (t, cb0 + g)),
            pl.BlockSpec((T, SSD_STATE), lambda t, g: (t, cc0 + g)),
            pl.BlockSpec((T, heads), lambda t, g: (row0 + t, 0)),
            pl.BlockSpec((SSD_CONV, GROUP_W), lambda t, g: (0, g)),
            pl.BlockSpec((SSD_CONV, SSD_STATE), lambda t, g: (0, cb0 + g)),
            pl.BlockSpec((SSD_CONV, SSD_STATE), lambda t, g: (0, cc0 + g)),
            pl.BlockSpec((1, GROUP_W), lambda t, g: (0, g)),
            pl.BlockSpec((1, SSD_STATE), lambda t, g: (0, cb0 + g)),
            pl.BlockSpec((1, SSD_STATE), lambda t, g: (0, cc0 + g)),
            full((1, heads)), full((1, heads)),
            per_g((None, 1, GROUP_W)), per_g((None, 1, GROUP_W)),
            full((T, T)), full((T, T)),
            pl.BlockSpec((heads, GROUP_W), lambda t, g: (0, g)),
            pl.BlockSpec((n_seg, GROUP_W, SSD_STATE), lambda t, g: (t, g, 0)),
            pl.BlockSpec(memory_space=pl.ANY),
        ],
        out_specs=[pl.BlockSpec((None, T, GROUP_W), lambda t, g: (g, row0 + t, 0)),
                   pl.BlockSpec((n_seg, GROUP_W, SSD_STATE), lambda t, g: (t, g, 0))],
        out_shape=[jax.ShapeDtypeStruct(yn.shape, BF16), jax.ShapeDtypeStruct(state_s.shape, F32)],
        scratch_shapes=[pltpu.VMEM((heads, T), F32), pltpu.VMEM((heads, T), F32)],
        input_output_aliases={22: 0},
        compiler_params=_cparams(("arbitrary", "arbitrary")),
        name="ssd_scan_sample",
    )(zx, zx, zx, zx, conv_pad_s, conv_pad_s, conv_pad_s, dt_raw, conv_w, conv_w, conv_w, conv_b, conv_b, conv_b,
      dtb, alog, dsk, nw, m_s, mseg_s, e_all, state_s, yn)
    return yn, st_s


def _final_body(x_ref, w_ref, yp_ref, ys_ref, *, n_ptiles):
    i = pl.program_id(0)
    x = x_ref[...]
    y = x * lax.rsqrt(jnp.mean(x * x, axis=-1, keepdims=True) + NORM_EPS) * w_ref[...]

    @pl.when(i < n_ptiles)
    def _():
        yp_ref[...] = y

    @pl.when(i >= n_ptiles)
    def _():
        ys_ref[...] = y


def _final_norm(x, w, n_prompt_rows):
    m, d = x.shape
    tm = 512
    n_ptiles = n_prompt_rows // tm
    return pl.pallas_call(
        functools.partial(_final_body, n_ptiles=n_ptiles),
        grid=(m // tm,),
        in_specs=[pl.BlockSpec((tm, d), lambda i: (i, 0)), pl.BlockSpec((1, d), lambda i: (0, 0))],
        out_specs=[pl.BlockSpec((tm, d), lambda i: (jnp.minimum(i, n_ptiles - 1), 0)),
                   pl.BlockSpec((tm, d), lambda i: (jnp.maximum(i - n_ptiles, 0), 0))],
        out_shape=[jax.ShapeDtypeStruct((n_prompt_rows, d), F32), jax.ShapeDtypeStruct((m - n_prompt_rows, d), F32)],
        compiler_params=_cparams(("arbitrary",)),
        name="final_norm",
    )(x, w.reshape(1, d))


def kernel(x_prompt, x_sample, c_prompt, c_sample, state_ssm, state_conv, mod_w, mod_b, norm_mix_w, norm_ffn_w,
           a_w_in, a_b_in, a_ln_w, a_ln_b, a_w_s, a_b_s, a_w_out,
           b_w_in, b_conv_w, b_conv_b, b_dt_bias, b_a_log, b_d, b_norm_w, b_w_out,
           f_w_in, f_w_out, final_norm_w):
    bp, lp, d = x_prompt.shape
    bs, ls, _ = x_sample.shape
    depth = mod_w.shape[0]
    n_prompt = bp * lp
    n_sample = bs * ls
    m_all = n_prompt + n_sample
    assert lp % TM == 0 and n_sample % TM == 0 and TM % ls == 0 and bp <= SUBLANES
    assert ls >= SSD_CONV - 1 and T % ls == 0 and lp % T == 0 and GMLP_CHUNK % ls == 0 and lp % MIX_ROWS == 0
    assert depth == 2 and a_w_in.shape[0] == 1 and b_w_in.shape[0] == 1
    cfg = Cfg(n_ptiles=n_prompt // TM, tiles_per_seq=lp // TM, seq_len_s=ls, srow0=SUBLANES)

    x_p = x_prompt.reshape(n_prompt, d)
    x_s = x_sample.reshape(n_sample, d)
    c_all = jnp.concatenate([c_prompt, jnp.zeros((SUBLANES - bp, d), F32), c_sample], axis=0)
    mods = _mod_table(c_all, mod_w, mod_b)

    u, v = _gmlp_in(x_p, x_s, mods, 0, norm_mix_w, a_w_in, a_b_in, 0, cfg)
    r = jnp.arange(GMLP_CHUNK)
    tril = r[None, :] <= r[:, None]
    mask = jnp.stack([tril, jnp.logical_and(tril, (r[:, None] // ls) == (r[None, :] // ls))]).astype(F32)
    rep = GMLP_CHUNK // ls
    wmix = jnp.stack([a_w_s[0], jnp.tile(a_w_s[0, :, :ls, :ls], (1, rep, rep))])
    width = a_w_in.shape[-1] // 2
    bias = jnp.stack([jnp.repeat(a_b_s[0].T, width // GMLP_GROUPS, axis=1),
                      jnp.repeat(jnp.tile(a_b_s[0, :, :ls].T, (rep, 1)), width // GMLP_GROUPS, axis=1)])
    gated, v_p, v_s = _gmlp_mix(u, v, a_ln_w[0], a_ln_b[0], wmix, mask, bias, n_prompt, lp, bp)
    x = _resid_matmul(gated, a_w_out, 0, x_p, x_s, mods, 0, 2, TN, cfg, "gmlp_out")
    act = _ffn_in(x, mods, 0, norm_ffn_w, f_w_in, cfg)
    x = _resid_matmul(act, f_w_out, 0, x, None, mods, 0, 5, TN_NARROW, cfg, "ffn_out0")

    inner = b_w_out.shape[1]
    conv_dim = b_conv_w.shape[-1]
    n_main = inner + conv_dim
    heads = inner // SSD_HEAD_DIM
    zx, dt_raw = _ssd_in(x, mods, 1, norm_mix_w, jnp.swapaxes(b_w_in, 1, 2)[0], n_main, cfg)
    conv_pad = jnp.pad(state_conv[0], ((0, 0), (ls - (SSD_CONV - 1), 0), (0, 0))).reshape(n_sample, conv_dim)
    ssd_params = (b_conv_w[0], b_conv_b, b_dt_bias, b_a_log,
                  jnp.repeat(b_d[0], SSD_HEAD_DIM).reshape(SSD_GROUPS, 1, GROUP_W),
                  b_norm_w.reshape(SSD_GROUPS, 1, GROUP_W))
    state_s = state_ssm[0].reshape(bs, inner, SSD_STATE)
    yn, ssm_p = _ssd_scan_prompt(zx, dt_raw, *ssd_params, bp, lp)
    yn, ssm_s = _ssd_scan_sample(zx, yn, dt_raw, *ssd_params, state_s, conv_pad, n_prompt // T, ls)
    x = _resid_matmul(yn, b_w_out, 0, x, None, mods, 1, 2, TN, cfg, "ssd_out")
    act = _ffn_in(x, mods, 1, norm_ffn_w, f_w_in, cfg)
    x = _resid_matmul(act, f_w_out, 1, x, None, mods, 1, 5, TN_NARROW, cfg, "ffn_out1")

    y_p, y_s = _final_norm(x, final_norm_w, n_prompt)

    zx4 = zx.reshape(zx.shape[0], m_all // ls, ls, TN)

    def tails(groups):
        t = groups[SSD_GROUPS:, :, ls - (SSD_CONV - 1):, :]
        return jnp.moveaxis(t, 0, 2).reshape(t.shape[1], SSD_CONV - 1, conv_dim)
    conv_p = tails(zx4[:, lp // ls - 1:n_prompt // ls:lp // ls])
    conv_s = tails(zx4[:, n_prompt // ls:])
    return (y_p.reshape(bp, lp, d), y_s.reshape(bs, ls, d),
            v_p.reshape(1, bp, GMLP_CHUNK, width), v_s.reshape(1, bs, ls, width),
            ssm_p.reshape(1, bp, heads, SSD_HEAD_DIM, SSD_STATE), ssm_s.reshape(1, bs, heads, SSD_HEAD_DIM, SSD_STATE),
            conv_p[None], conv_s[None])
```

```python
import functools
import math
from typing import NamedTuple

import jax
import jax.numpy as jnp
from jax import lax
from jax.experimental import pallas as pl
from jax.experimental.pallas import tpu as pltpu

F32 = jnp.float32
BF16 = jnp.bfloat16

NORM_EPS = 1e-6
LN_EPS = 1e-5

GMLP_GROUPS = 16
GMLP_CHUNK = 128
SSD_HEAD_DIM = 64
SSD_STATE = 128
SSD_GROUPS = 8
SSD_CONV = 4
SSD_CHUNK = 128

SUBLANES = 8
LANES = 128
VMEM_LIMIT_BYTES = 56 * 1024 * 1024

TM = 1024
TN = 512
TN_NARROW = 256
ROW_CHUNK = 256


class Cfg(NamedTuple):
    n_ptiles: int
    tiles_per_seq: int
    seq_len_s: int
    srow0: int


def _cparams(sem):
    return pltpu.CompilerParams(dimension_semantics=sem, vmem_limit_bytes=VMEM_LIMIT_BYTES)


def _silu(x):
    return x / (1.0 + jnp.exp(-x))


def _gelu(x):
    return 0.5 * x * (1.0 + lax.erf(x * (1.0 / math.sqrt(2.0))))


def _rms_mod(x, w, scale, shift):
    y = x * lax.rsqrt(jnp.mean(x * x, axis=-1, keepdims=True) + NORM_EPS)
    return (y * w) * (1.0 + scale) + shift


def _chunk_rows(c):
    return pl.ds(pl.multiple_of(c * ROW_CHUNK, ROW_CHUNK), ROW_CHUNK)


def _sample_rows(i, cfg, x_refs, mod_refs, fn, o_ref):
    rows = o_ref.shape[0]
    nseq = ROW_CHUNK // cfg.seq_len_s
    row0 = cfg.srow0 + (i - cfg.n_ptiles) * (rows // cfg.seq_len_s)

    def body(c, carry):
        rs = _chunk_rows(c)
        r0 = pl.multiple_of(row0 + c * nseq, SUBLANES)
        ms = [m[pl.ds(r0, nseq), :][:, None, :] for m in mod_refs]
        x3 = [x[rs, :].reshape(nseq, cfg.seq_len_s, x.shape[-1]) for x in x_refs]
        o_ref[rs, :] = fn(x3, ms).reshape(ROW_CHUNK, o_ref.shape[-1]).astype(o_ref.dtype)
        return carry
    lax.fori_loop(0, rows // ROW_CHUNK, body, 0)


def _per_seq(i, cfg, xp_refs, xs_refs, mod_refs, fn, o_ref):
    @pl.when(i < cfg.n_ptiles)
    def _():
        s = i // cfg.tiles_per_seq
        ms = [m[pl.ds(s, 1), :] for m in mod_refs]

        def body(c, carry):
            rs = _chunk_rows(c)
            o_ref[rs, :] = fn([x[rs, :] for x in xp_refs], ms).astype(o_ref.dtype)
            return carry
        lax.fori_loop(0, o_ref.shape[0] // ROW_CHUNK, body, 0)

    @pl.when(i >= cfg.n_ptiles)
    def _():
        _sample_rows(i, cfg, xs_refs, mod_refs, fn, o_ref)


def _prompt_block(cfg):
    return lambda i: jnp.minimum(i, cfg.n_ptiles - 1)


def _sample_block(cfg):
    return lambda i: jnp.maximum(i - cfg.n_ptiles, 0)


def _mod_body(c_ref, w_ref, b_ref, o_ref):
    sc = _silu(c_ref[...]).astype(BF16)
    o_ref[...] = jnp.dot(sc, w_ref[...].astype(BF16), preferred_element_type=F32) + b_ref[...]


def _mod_table(c_all, mod_w, mod_b):
    depth, d, n = mod_w.shape
    r = c_all.shape[0]
    tn = 1024
    return pl.pallas_call(
        _mod_body,
        grid=(depth, n // tn),
        in_specs=[
            pl.BlockSpec((r, d), lambda l, j: (0, 0)),
            pl.BlockSpec((None, d, tn), lambda l, j: (l, 0, j)),
            pl.BlockSpec((None, 1, tn), lambda l, j: (l, 0, j)),
        ],
        out_specs=pl.BlockSpec((None, r, tn), lambda l, j: (l, 0, j)),
        out_shape=jax.ShapeDtypeStruct((depth, r, n), F32),
        compiler_params=_cparams(("arbitrary", "arbitrary")),
        name="mod_table",
    )(c_all, mod_w, mod_b.reshape(depth, 1, n))


def _norm_prologue(i, j, cfg, xp_ref, xs_ref, nw_ref, sc_ref, sh_ref, h_ref):
    @pl.when(j == 0)
    def _():
        _per_seq(i, cfg, [xp_ref], [xs_ref], [sc_ref, sh_ref],
                 lambda xs, ms: _rms_mod(xs[0], nw_ref[...], ms[0], ms[1]), h_ref)


def _gmlp_in_body(xp_ref, xs_ref, nw_ref, sh_ref, sc_ref, wu_ref, wv_ref, bu_ref, bv_ref, u_ref, v_ref, h_ref, *, cfg):
    i, j = pl.program_id(0), pl.program_id(1)
    _norm_prologue(i, j, cfg, xp_ref, xs_ref, nw_ref, sc_ref, sh_ref, h_ref)
    h = h_ref[...]
    u_ref[...] = _gelu(jnp.dot(h, wu_ref[...].astype(BF16), preferred_element_type=F32) + bu_ref[...])
    v_ref[...] = _gelu(jnp.dot(h, wv_ref[...].astype(BF16), preferred_element_type=F32) + bv_ref[...])


def _gmlp_in(x_p, x_s, mods, layer, norm_w, w_in, b_in, j_layer, cfg):
    d = x_p.shape[1]
    m = x_p.shape[0] + x_s.shape[0]
    width = w_in.shape[-1] // 2
    tn = TN_NARROW
    nj = width // tn
    r = mods.shape[1]
    pb, sb = _prompt_block(cfg), _sample_block(cfg)
    b2 = b_in.reshape(b_in.shape[0], 1, -1)
    return pl.pallas_call(
        functools.partial(_gmlp_in_body, cfg=cfg),
        grid=(m // TM, nj),
        in_specs=[
            pl.BlockSpec((TM, d), lambda i, j: (pb(i), 0)),
            pl.BlockSpec((TM, d), lambda i, j: (sb(i), 0), pipeline_mode=pl.Buffered(1)),
            pl.BlockSpec((None, 1, d), lambda i, j: (layer, 0, 0)),
            pl.BlockSpec((None, r, d), lambda i, j: (layer, 0, 0)),
            pl.BlockSpec((None, r, d), lambda i, j: (layer, 0, 1)),
            pl.BlockSpec((None, d, tn), lambda i, j: (j_layer, 0, j)),
            pl.BlockSpec((None, d, tn), lambda i, j: (j_layer, 0, j + nj)),
            pl.BlockSpec((None, 1, tn), lambda i, j: (j_layer, 0, j)),
            pl.BlockSpec((None, 1, tn), lambda i, j: (j_layer, 0, j + nj)),
        ],
        out_specs=[pl.BlockSpec((TM, tn), lambda i, j: (i, j)), pl.BlockSpec((TM, tn), lambda i, j: (i, j))],
        out_shape=[jax.ShapeDtypeStruct((m, width), F32), jax.ShapeDtypeStruct((m, width), F32)],
        scratch_shapes=[pltpu.VMEM((TM, d), BF16)],
        compiler_params=_cparams(("arbitrary", "arbitrary")),
        name="gmlp_in",
    )(x_p, x_s, norm_w.reshape(-1, 1, d), mods, mods, w_in, w_in, b2, b2)


def _ffn_in_body(x_ref, nw_ref, sh_ref, sc_ref, wg_ref, wu_ref, a_ref, h_ref, *, cfg):
    i, j = pl.program_id(0), pl.program_id(1)
    _norm_prologue(i, j, cfg, x_ref, x_ref, nw_ref, sc_ref, sh_ref, h_ref)
    h = h_ref[...]
    gate = jnp.dot(h, wg_ref[...].astype(BF16), preferred_element_type=F32)
    up = jnp.dot(h, wu_ref[...].astype(BF16), preferred_element_type=F32)
    a_ref[...] = (_silu(gate) * up).astype(BF16)


def _ffn_in(x, mods, layer, norm_w, w_in, cfg):
    m, d = x.shape
    hidden = w_in.shape[-1] // 2
    nj = hidden // TN
    r = mods.shape[1]
    return pl.pallas_call(
        functools.partial(_ffn_in_body, cfg=cfg),
        grid=(m // TM, nj),
        in_specs=[
            pl.BlockSpec((TM, d), lambda i, j: (i, 0)),
            pl.BlockSpec((None, 1, d), lambda i, j: (layer, 0, 0)),
            pl.BlockSpec((None, r, d), lambda i, j: (layer, 0, 3)),
            pl.BlockSpec((None, r, d), lambda i, j: (layer, 0, 4)),
            pl.BlockSpec((None, d, TN), lambda i, j: (layer, 0, j)),
            pl.BlockSpec((None, d, TN), lambda i, j: (layer, 0, j + nj)),
        ],
        out_specs=pl.BlockSpec((TM, TN), lambda i, j: (i, j)),
        out_shape=jax.ShapeDtypeStruct((m, hidden), BF16),
        scratch_shapes=[pltpu.VMEM((TM, d), BF16)],
        compiler_params=_cparams(("arbitrary", "arbitrary")),
        name="ffn_in",
    )(x, norm_w.reshape(-1, 1, d), mods, mods, w_in, w_in)


_NT = (((1,), (1,)), ((), ()))


def _ssd_in_body(x_ref, nw_ref, sh_ref, sc_ref, w_ref, wdt_ref, o_ref, dt_ref, h_ref, *, cfg):
    i, j = pl.program_id(0), pl.program_id(1)
    _norm_prologue(i, j, cfg, x_ref, x_ref, nw_ref, sc_ref, sh_ref, h_ref)
    h = h_ref[...]
    o_ref[...] = lax.dot_general(h, w_ref[...].astype(BF16), _NT, preferred_element_type=F32)

    @pl.when(j == 0)
    def _():
        dt_ref[...] = lax.dot_general(h, wdt_ref[...].astype(BF16), _NT, preferred_element_type=F32)


def _ssd_in(x, mods, layer, norm_w, w_in_t, n_main, cfg):
    m, d = x.shape
    r = mods.shape[1]
    n_dt = w_in_t.shape[0] - n_main
    nj = n_main // TN
    return pl.pallas_call(
        functools.partial(_ssd_in_body, cfg=cfg),
        grid=(m // TM, nj),
        in_specs=[
            pl.BlockSpec((TM, d), lambda i, j: (i, 0)),
            pl.BlockSpec((None, 1, d), lambda i, j: (layer, 0, 0)),
            pl.BlockSpec((None, r, d), lambda i, j: (layer, 0, 0)),
            pl.BlockSpec((None, r, d), lambda i, j: (layer, 0, 1)),
            pl.BlockSpec((TN, d), lambda i, j: (j, 0)),
            pl.BlockSpec((n_dt, d), lambda i, j: (n_main // n_dt, 0)),
        ],
        out_specs=[pl.BlockSpec((None, TM, TN), lambda i, j: (j, i, 0)), pl.BlockSpec((TM, n_dt), lambda i, j: (i, 0))],
        out_shape=[jax.ShapeDtypeStruct((nj, m, TN), F32), jax.ShapeDtypeStruct((m, n_dt), F32)],
        scratch_shapes=[pltpu.VMEM((TM, d), BF16)],
        compiler_params=_cparams(("arbitrary", "arbitrary")),
        name="ssd_in",
    )(x, norm_w.reshape(-1, 1, d), mods, mods, w_in_t, w_in_t)


TK = 512


def _resid_body(a_ref, w_ref, *rest, cfg, nk):
    if len(rest) == 4:
        xp_ref, xs_ref, g_ref, o_ref = rest
    else:
        xp_ref, g_ref, o_ref = rest
        xs_ref = xp_ref
    i, k = pl.program_id(0), pl.program_id(1)
    d = o_ref.shape[1]

    def accumulate(first):
        a = a_ref[...]
        for c in range(d // TN):
            cols = slice(c * TN, (c + 1) * TN)
            part = jnp.dot(a, w_ref[:, cols].astype(BF16), preferred_element_type=F32)
            if first:
                o_ref[:, cols] = part
            else:
                o_ref[:, cols] += part

    @pl.when(k == 0)
    def _():
        accumulate(True)

    @pl.when(k > 0)
    def _():
        accumulate(False)

    @pl.when(k == nk - 1)
    def _():
        _per_seq(i, cfg, [xp_ref, o_ref], [xs_ref, o_ref], [g_ref], lambda xs, ms: xs[0] + ms[0] * xs[1], o_ref)


def _resid_matmul(a, w, w_layer, x_p, x_s, mods, layer, gate_chunk, cfg, name):
    m = a.shape[-2]
    d = x_p.shape[1]
    r = mods.shape[1]
    if a.ndim == 2:
        nk = a.shape[1] // TK
        a_spec = pl.BlockSpec((TM, TK), lambda i, k: (i, k))
    else:
        nk = a.shape[0]
        assert a.shape[2] == TK
        a_spec = pl.BlockSpec((None, TM, TK), lambda i, k: (k, i, 0))
    if x_s is None:
        x_specs, xs = [pl.BlockSpec((TM, d), lambda i, k: (i, 0))], [x_p]
    else:
        pb, sb = _prompt_block(cfg), _sample_block(cfg)
        x_specs = [pl.BlockSpec((TM, d), lambda i, k: (pb(i), 0)),
                   pl.BlockSpec((TM, d), lambda i, k: (sb(i), 0), pipeline_mode=pl.Buffered(1))]
        xs = [x_p, x_s]
    return pl.pallas_call(
        functools.partial(_resid_body, cfg=cfg, nk=nk),
        grid=(m // TM, nk),
        in_specs=[a_spec, pl.BlockSpec((None, TK, d), lambda i, k: (w_layer, k, 0))] + x_specs
        + [pl.BlockSpec((None, r, d), lambda i, k: (layer, 0, gate_chunk))],
        out_specs=pl.BlockSpec((TM, d), lambda i, k: (i, 0)),
        out_shape=jax.ShapeDtypeStruct((m, d), F32),
        compiler_params=_cparams(("arbitrary", "arbitrary")),
        name=name,
    )(a, w, *xs, mods)


MIX_ROWS = 2 * GMLP_CHUNK


def _gmlp_mix_body(u_ref, v_ref, lnw_ref, lnb_ref, ws_ref, mask_ref, bias_ref, g_ref, vp_ref, vs_ref,
                   vn_ref, *, n_prompt_steps, steps_per_seq):
    t = pl.program_id(0)
    v = v_ref[...]
    xc = v - jnp.mean(v, axis=-1, keepdims=True)
    vn = xc * lax.rsqrt(jnp.mean(xc * xc, axis=-1, keepdims=True) + LN_EPS) * lnw_ref[...] + lnb_ref[...]
    vn_ref[...] = vn

    @pl.when(jnp.logical_and(t < n_prompt_steps, t % steps_per_seq == steps_per_seq - 1))
    def _():
        vp_ref[...] = vn[MIX_ROWS - GMLP_CHUNK:, :]

    @pl.when(t >= n_prompt_steps)
    def _():
        vs_ref[...] = vn

    mask = mask_ref[...]
    for g in range(GMLP_GROUPS):
        wb = (ws_ref[g] * mask).astype(BF16)
        cols = slice(g * GMLP_CHUNK, (g + 1) * GMLP_CHUNK)
        for c in range(MIX_ROWS // GMLP_CHUNK):
            rows = slice(c * GMLP_CHUNK, (c + 1) * GMLP_CHUNK)
            s = jnp.dot(wb, vn_ref[rows, cols].astype(BF16), preferred_element_type=F32) + bias_ref[:, cols]
            g_ref[rows, cols] = (u_ref[rows, cols] * s).astype(BF16)


def _gmlp_mix(u, v, ln_w, ln_b, wmix, mask, bias, n_prompt_rows, seq_len, n_prompt_seq):
    m, width = u.shape
    n_prompt_steps = n_prompt_rows // MIX_ROWS
    steps_per_seq = seq_len // MIX_ROWS
    n_sample_rows = m - n_prompt_rows

    def variant(t):
        return jnp.where(t < n_prompt_steps, 0, 1)

    return pl.pallas_call(
        functools.partial(_gmlp_mix_body, n_prompt_steps=n_prompt_steps, steps_per_seq=steps_per_seq),
        grid=(m // MIX_ROWS,),
        in_specs=[
            pl.BlockSpec((MIX_ROWS, width), lambda t: (t, 0)),
            pl.BlockSpec((MIX_ROWS, width), lambda t: (t, 0)),
            pl.BlockSpec((1, width), lambda t: (0, 0)),
            pl.BlockSpec((1, width), lambda t: (0, 0)),
            pl.BlockSpec((None, GMLP_GROUPS, GMLP_CHUNK, GMLP_CHUNK), lambda t: (variant(t), 0, 0, 0)),
            pl.BlockSpec((None, GMLP_CHUNK, GMLP_CHUNK), lambda t: (variant(t), 0, 0)),
            pl.BlockSpec((None, GMLP_CHUNK, width), lambda t: (variant(t), 0, 0)),
        ],
        out_specs=[
            pl.BlockSpec((MIX_ROWS, width), lambda t: (t, 0)),
            pl.BlockSpec((GMLP_CHUNK, width), lambda t: (jnp.minimum(t // steps_per_seq, n_prompt_seq - 1), 0)),
            pl.BlockSpec((MIX_ROWS, width), lambda t: (jnp.maximum(t - n_prompt_steps, 0), 0)),
        ],
        out_shape=[
            jax.ShapeDtypeStruct((m, width), BF16),
            jax.ShapeDtypeStruct((n_prompt_seq * GMLP_CHUNK, width), F32),
            jax.ShapeDtypeStruct((n_sample_rows, width), F32),
        ],
        scratch_shapes=[pltpu.VMEM((MIX_ROWS, width), F32)],
        compiler_params=_cparams(("arbitrary",)),
        name="gmlp_mix",
    )(u, v, ln_w.reshape(1, width), ln_b.reshape(1, width), wmix, mask, bias)


GROUP_HEADS = 8
GROUP_W = GROUP_HEADS * SSD_HEAD_DIM
T = SSD_CHUNK
BC_PER_BLOCK = GROUP_W // SSD_STATE


def _split3(x):
    hi = x.astype(BF16)
    r = x - hi.astype(F32)
    mid = r.astype(BF16)
    lo = (r - mid.astype(F32)).astype(BF16)
    return hi, mid, lo


def _dot_exact_rhs(m_b, pieces):
    return sum(jnp.dot(m_b, p, preferred_element_type=F32) for p in pieces)


def _dot_exact_lhs(pieces, e_b):
    return sum(jnp.dot(p, e_b, preferred_element_type=F32) for p in pieces)


def _ssd_group(xs, bm, cm, z, dt_x, cs_x, cs_end_x, cs_row, d_a, states, n_seg, dsk, nw, keep):
    seg = T // n_seg
    xdt = xs * dt_x
    ecs_x = jnp.exp(cs_x)
    dte_x = jnp.exp(cs_end_x - cs_x)
    xdt_b = xdt.astype(BF16)
    xd_t = (xdt * dte_x).T.astype(BF16)
    bmb = bm.astype(BF16)
    cmb = cm.astype(BF16)
    cb = lax.dot_general(cmb, bmb, _NT, preferred_element_type=F32)

    row = lax.broadcasted_iota(jnp.int32, (T, SSD_STATE), 0)
    y_offs, new_states = [], []
    for s in range(n_seg):
        st = states[s]
        c_seg = cmb if n_seg == 1 else cm[s * seg:(s + 1) * seg].astype(BF16)
        y_offs.append(lax.dot_general(c_seg, st.astype(BF16), _NT, preferred_element_type=F32))
        if n_seg == 1:
            b_seg = bmb
        else:
            b_seg = jnp.where(jnp.logical_and(row >= s * seg, row < (s + 1) * seg), bm, 0.0).astype(BF16)
        upd = jnp.dot(xd_t, b_seg, preferred_element_type=F32)
        decayed = jnp.concatenate(
            [st[h * SSD_HEAD_DIM:(h + 1) * SSD_HEAD_DIM, :] * d_a(s, h) for h in range(GROUP_HEADS)], axis=0)
        new_states.append(decayed + upd)
    y_off = jnp.concatenate(y_offs, axis=0) if n_seg > 1 else y_offs[0]

    lane = lax.broadcasted_iota(jnp.int32, (T, LANES), 1)
    lo_half = lane < SSD_HEAD_DIM
    ys = []
    for q in range(GROUP_HEADS // 2):
        v = cs_x[:, q * LANES:(q + 1) * LANES]
        r = pltpu.roll(v, SSD_HEAD_DIM, 1)
        cols = (jnp.where(lo_half, v, r), jnp.where(lo_half, r, v))
        ws = []
        for e in range(2):
            diff = cols[e] - cs_row(2 * q + e)
            ws.append((cb * jnp.exp(jnp.where(keep, diff, -jnp.inf))).astype(BF16))
        xp = xdt_b[:, q * LANES:(q + 1) * LANES]
        zero = jnp.zeros_like(xp)
        xpair = jnp.concatenate([jnp.where(lo_half, xp, zero), jnp.where(lo_half, zero, xp)], axis=0)
        ys.append(jnp.dot(jnp.concatenate(ws, axis=1), xpair, preferred_element_type=F32))
    y_diag = jnp.concatenate(ys, axis=1)

    y = y_diag + y_off * ecs_x + dsk * xs
    y = y * _silu(z)
    yn = y * lax.rsqrt(jnp.mean(y * y, axis=-1, keepdims=True) + NORM_EPS) * nw
    return yn.astype(BF16), new_states


def _causal_keep(seq_len):
    r = lax.broadcasted_iota(jnp.int32, (T, T), 0)
    c = lax.broadcasted_iota(jnp.int32, (T, T), 1)
    keep = c <= r
    if seq_len < T:
        keep = jnp.logical_and(keep, (r // seq_len) == (c // seq_len))
    return keep


def _ssd_prompt_body(z_ref, xs_ref, bc_ref, dt_ref, cw_ref, cbias_ref, dtb_ref, alog_ref, dsk_ref, nw_ref,
                     m_ref, e_ref, yn_ref, st_ref,
                     cbuf_ref, xc_ref, bcs_ref, dt3_ref, cs3_ref, cst_ref):
    c = pl.program_id(1)
    n_x = xs_ref.shape[0]
    pad = SUBLANES
    taps = SSD_CONV - 1

    @pl.when(c == 0)
    def _():
        st_ref[...] = jnp.zeros_like(st_ref)
        cbuf_ref[:, 0:pad, :] = jnp.zeros((cbuf_ref.shape[0], pad, GROUP_W), F32)

    def conv(k, src):
        w = cw_ref[k]
        cbuf_ref[k, pad:pad + T, :] = src
        acc = cbias_ref[k] + src * w[taps:taps + 1, :]
        for kk in range(taps):
            acc = acc + cbuf_ref[k, pad - taps + kk:pad - taps + kk + T, :] * w[kk:kk + 1, :]
        cbuf_ref[k, pad - taps:pad, :] = cbuf_ref[k, pad + T - taps:pad + T, :]
        return _silu(acc)

    def conv_x(k, carry):
        xc_ref[k] = conv(k, xs_ref[k])
        return carry
    lax.fori_loop(0, n_x, conv_x, 0)

    def conv_bc(k, carry):
        out = conv(n_x + k, bc_ref[k])
        for gg in range(BC_PER_BLOCK):
            bcs_ref[k * BC_PER_BLOCK + gg] = out[:, gg * SSD_STATE:(gg + 1) * SSD_STATE]
        return carry
    lax.fori_loop(0, bc_ref.shape[0], conv_bc, 0)

    dt = jax.nn.softplus(dt_ref[...] + dtb_ref[...])
    a = dt * (-jnp.exp(alog_ref[...]))
    cs = _dot_exact_rhs(m_ref[...], _split3(a))
    cst_ref[...] = cs.T
    for p, (dt_p, cs_p) in enumerate(zip(_split3(dt), _split3(cs))):
        dt3_ref[p] = dt_p
        cs3_ref[p] = cs_p

    keep = _causal_keep(T)

    def spread(g):
        e_g = e_ref[g]
        return (_dot_exact_lhs([dt3_ref[p] for p in range(3)], e_g), _dot_exact_lhs([cs3_ref[p] for p in range(3)], e_g))

    def group(g, dt_x, cs_x):
        h0 = g * GROUP_HEADS
        rows = pl.ds(g * GROUP_W, GROUP_W)
        yn, new_states = _ssd_group(
            xc_ref[g], bcs_ref[g], bcs_ref[SSD_GROUPS + g], z_ref[g], dt_x, cs_x, cs_x[T - 1:T, :],
            lambda h: cst_ref[pl.ds(h0 + h, 1), :],
            lambda s, h: jnp.exp(cst_ref[pl.ds(h0 + h, 1), T - 1:T]),
            [st_ref[rows, :]], 1, dsk_ref[g], nw_ref[g], keep)
        yn_ref[g] = yn
        st_ref[rows, :] = new_states[0]

    spread_next = spread(0)
    for g in range(SSD_GROUPS):
        dt_x, cs_x = spread_next
        if g + 1 < SSD_GROUPS:
            spread_next = spread(g + 1)
        group(g, dt_x, cs_x)


def _ssd_sample_body(z_ref, xs_ref, b_ref, c_ref, px_ref, pb_ref, pc_ref, dt_ref, wx_ref, wb_ref, wc_ref,
                     bx_ref, bb_ref, bc_ref, dtb_ref, alog_ref, dsk_ref, nw_ref, m_ref, mseg_ref, e_ref,
                     st_in_ref, yn_in_ref, yn_ref, st_ref, cst_ref, cet_ref, *, seq_len):
    del yn_in_ref
    g = pl.program_id(1)
    n_seg = T // seq_len

    def conv(x_ref, p_ref, w_ref, bias_ref):
        x = x_ref[...]
        p = p_ref[...]
        tpos = lax.broadcasted_iota(jnp.int32, x.shape, 0) % seq_len
        acc = bias_ref[...] + x * w_ref[SSD_CONV - 1:SSD_CONV, :]
        for sh in range(1, SSD_CONV):
            shifted = jnp.where(tpos >= sh, pltpu.roll(x, sh, 0), pltpu.roll(p, T - seq_len + sh, 0))
            acc = acc + shifted * w_ref[SSD_CONV - 1 - sh:SSD_CONV - sh, :]
        return _silu(acc)

    xs = conv(xs_ref, px_ref, wx_ref, bx_ref)
    bm = conv(b_ref, pb_ref, wb_ref, bb_ref)
    cm = conv(c_ref, pc_ref, wc_ref, bc_ref)

    dt = jax.nn.softplus(dt_ref[...] + dtb_ref[...])
    a3 = _split3(dt * (-jnp.exp(alog_ref[...])))
    cs = _dot_exact_rhs(m_ref[...], a3)
    cs_end = _dot_exact_rhs(mseg_ref[...], a3)
    cst_ref[...] = cs.T
    cet_ref[...] = cs_end.T
    e_g = e_ref[...]
    h0 = g * GROUP_HEADS
    yn, new_states = _ssd_group(
        xs, bm, cm, z_ref[...], _dot_exact_lhs(_split3(dt), e_g), _dot_exact_lhs(_split3(cs), e_g),
        _dot_exact_lhs(_split3(cs_end), e_g),
        lambda h: cst_ref[pl.ds(h0 + h, 1), :],
        lambda s, h: jnp.exp(cet_ref[pl.ds(h0 + h, 1), s * seq_len:s * seq_len + 1]),
        [st_in_ref[s] for s in range(n_seg)], n_seg, dsk_ref[...], nw_ref[...], _causal_keep(seq_len))
    yn_ref[...] = yn
    for s in range(n_seg):
        st_ref[s] = new_states[s]


def _ssd_masks(seq_len):
    r = jnp.arange(T)
    same = (r[:, None] // seq_len) == (r[None, :] // seq_len)
    return jnp.logical_and(same, r[None, :] <= r[:, None]).astype(BF16), same.astype(BF16)


def _ssd_scan_prompt(zx, dt_raw, conv_w, conv_b, dtb, alog, dsk, nw, n_pseq, seq_len_p):
    _, m_all, _ = zx.shape
    heads = dt_raw.shape[1]
    inner = SSD_GROUPS * GROUP_W
    n_chunks = seq_len_p // T
    n_bc = 2 * SSD_GROUPS // BC_PER_BLOCK
    n_cblk = SSD_GROUPS + n_bc
    cw = conv_w.reshape(SSD_CONV, n_cblk, GROUP_W).transpose(1, 0, 2)
    cbias = conv_b.reshape(n_cblk, 1, GROUP_W)
    e_grp = (jnp.arange(GROUP_W)[None, None, :] // SSD_HEAD_DIM + GROUP_HEADS * jnp.arange(SSD_GROUPS)[:, None, None]
             == jnp.arange(heads)[None, :, None]).astype(BF16)
    m_p, _ = _ssd_masks(T)
    yn_shape = jax.ShapeDtypeStruct((SSD_GROUPS, m_all, GROUP_W), BF16)

    rp = lambda b, c: b * n_chunks + c
    full = lambda shape: pl.BlockSpec(shape, lambda b, c: (0,) * len(shape))
    yn, st_p = pl.pallas_call(
        _ssd_prompt_body,
        grid=(n_pseq, n_chunks),
        in_specs=[
            pl.BlockSpec((SSD_GROUPS, T, GROUP_W), lambda b, c: (0, rp(b, c), 0)),
            pl.BlockSpec((SSD_GROUPS, T, GROUP_W), lambda b, c: (1, rp(b, c), 0)),
            pl.BlockSpec((n_bc, T, GROUP_W), lambda b, c: (2 * SSD_GROUPS // n_bc, rp(b, c), 0)),
            pl.BlockSpec((T, heads), lambda b, c: (rp(b, c), 0)),
            full((n_cblk, SSD_CONV, GROUP_W)), full((n_cblk, 1, GROUP_W)), full((1, heads)), full((1, heads)),
            full((SSD_GROUPS, 1, GROUP_W)), full((SSD_GROUPS, 1, GROUP_W)), full((T, T)),
            full((SSD_GROUPS, heads, GROUP_W)),
        ],
        out_specs=[pl.BlockSpec((SSD_GROUPS, T, GROUP_W), lambda b, c: (0, rp(b, c), 0)),
                   pl.BlockSpec((None, inner, SSD_STATE), lambda b, c: (b, 0, 0))],
        out_shape=[yn_shape, jax.ShapeDtypeStruct((n_pseq, inner, SSD_STATE), F32)],
        scratch_shapes=[pltpu.VMEM((n_cblk, SUBLANES + T, GROUP_W), F32),
                        pltpu.VMEM((SSD_GROUPS, T, GROUP_W), F32),
                        pltpu.VMEM((2 * SSD_GROUPS, T, SSD_STATE), F32),
                        pltpu.VMEM((3, T, heads), BF16),
                        pltpu.VMEM((3, T, heads), BF16),
                        pltpu.VMEM((heads, T), F32)],
        compiler_params=_cparams(("arbitrary", "arbitrary")),
        name="ssd_scan_prompt",
    )(zx, zx, zx, dt_raw, cw, cbias, dtb, alog, dsk, nw, m_p, e_grp)
    return yn, st_p


def _ssd_scan_sample(zx, yn, dt_raw, conv_w, conv_b, dtb, alog, dsk, nw, state_s, conv_pad_s, row0, seq_len_s):
    heads = dt_raw.shape[1]
    inner = SSD_GROUPS * GROUP_W
    n_seg = T // seq_len_s
    n_sseq = state_s.shape[0]
    e_all = (jnp.arange(inner)[None, :] // SSD_HEAD_DIM == jnp.arange(heads)[:, None]).astype(BF16)
    m_s, mseg_s = _ssd_masks(seq_len_s)
    xoff = SSD_GROUPS
    boff = 2 * SSD_GROUPS
    coff = boff + SSD_GROUPS // BC_PER_BLOCK
    cb0 = inner // SSD_STATE
    cc0 = cb0 + SSD_GROUPS
    full = lambda shape: pl.BlockSpec(shape, lambda t, g: (0,) * len(shape))
    per_g = lambda shape: pl.BlockSpec(shape, lambda t, g: (g,) + (0,) * (len(shape) - 1))
    yn, st_s = pl.pallas_call(
        functools.partial(_ssd_sample_body, seq_len=seq_len_s),
        grid=(n_sseq // n_seg, SSD_GROUPS),
        in_specs=[
            pl.BlockSpec((None, T, GROUP_W), lambda t, g: (g, row0 + t, 0)),
            pl.BlockSpec((None, T, GROUP_W), lambda t, g: (xoff + g, row0 + t, 0)),
            pl.BlockSpec((None, T, SSD_STATE), lambda t, g: (boff + g // BC_PER_BLOCK, row0 + t, g % BC_PER_BLOCK)),
            pl.BlockSpec((None, T, SSD_STATE), lambda t, g: (coff + g // BC_PER_BLOCK, row0 + t, g % BC_PER_BLOCK)),
            pl.BlockSpec((T, GROUP_W), lambda t, g: (t, g)),
            pl.BlockSpec((T, SSD_STATE), lambda t, g: (t, cb0 + g)),
            pl.BlockSpec((T, SSD_STATE), lambda t, g: (t, cc0 + g)),
            pl.BlockSpec((T, heads), lambda t, g: (row0 + t, 0)),
            pl.BlockSpec((SSD_CONV, GROUP_W), lambda t, g: (0, g)),
            pl.BlockSpec((SSD_CONV, SSD_STATE), lambda t, g: (0, cb0 + g)),
            pl.BlockSpec((SSD_CONV, SSD_STATE), lambda t, g: (0, cc0 + g)),
            pl.BlockSpec((1, GROUP_W), lambda t, g: (0, g)),
            pl.BlockSpec((1, SSD_STATE), lambda t, g: (0, cb0 + g)),
            pl.BlockSpec((1, SSD_STATE), lambda t, g: (0, cc0 + g)),
            full((1, heads)), full((1, heads)),
            per_g((None, 1, GROUP_W)), per_g((None, 1, GROUP_W)),
            full((T, T)), full((T, T)),
            pl.BlockSpec((heads, GROUP_W), lambda t, g: (0, g)),
            pl.BlockSpec((n_seg, GROUP_W, SSD_STATE), lambda t, g: (t, g, 0)),
            pl.BlockSpec(memory_space=pl.ANY),
        ],
        out_specs=[pl.BlockSpec((None, T, GROUP_W), lambda t, g: (g, row0 + t, 0)),
                   pl.BlockSpec((n_seg, GROUP_W, SSD_STATE), lambda t, g: (t, g, 0))],
        out_shape=[jax.ShapeDtypeStruct(yn.shape, BF16), jax.ShapeDtypeStruct(state_s.shape, F32)],
        scratch_shapes=[pltpu.VMEM((heads, T), F32), pltpu.VMEM((heads, T), F32)],
        input_output_aliases={22: 0},
        compiler_params=_cparams(("arbitrary", "arbitrary")),
        name="ssd_scan_sample",
    )(zx, zx, zx, zx, conv_pad_s, conv_pad_s, conv_pad_s, dt_raw, conv_w, conv_w, conv_w, conv_b, conv_b, conv_b,
      dtb, alog, dsk, nw, m_s, mseg_s, e_all, state_s, yn)
    return yn, st_s


def _final_body(x_ref, w_ref, yp_ref, ys_ref, *, n_ptiles):
    i = pl.program_id(0)
    x = x_ref[...]
    y = x * lax.rsqrt(jnp.mean(x * x, axis=-1, keepdims=True) + NORM_EPS) * w_ref[...]

    @pl.when(i < n_ptiles)
    def _():
        yp_ref[...] = y

    @pl.when(i >= n_ptiles)
    def _():
        ys_ref[...] = y


def _final_norm(x, w, n_prompt_rows):
    m, d = x.shape
    tm = 512
    n_ptiles = n_prompt_rows // tm
    return pl.pallas_call(
        functools.partial(_final_body, n_ptiles=n_ptiles),
        grid=(m // tm,),
        in_specs=[pl.BlockSpec((tm, d), lambda i: (i, 0)), pl.BlockSpec((1, d), lambda i: (0, 0))],
        out_specs=[pl.BlockSpec((tm, d), lambda i: (jnp.minimum(i, n_ptiles - 1), 0)),
                   pl.BlockSpec((tm, d), lambda i: (jnp.maximum(i - n_ptiles, 0), 0))],
        out_shape=[jax.ShapeDtypeStruct((n_prompt_rows, d), F32), jax.ShapeDtypeStruct((m - n_prompt_rows, d), F32)],
        compiler_params=_cparams(("arbitrary",)),
        name="final_norm",
    )(x, w.reshape(1, d))


def kernel(x_prompt, x_sample, c_prompt, c_sample, state_ssm, state_conv, mod_w, mod_b, norm_mix_w, norm_ffn_w,
           a_w_in, a_b_in, a_ln_w, a_ln_b, a_w_s, a_b_s, a_w_out,
           b_w_in, b_conv_w, b_conv_b, b_dt_bias, b_a_log, b_d, b_norm_w, b_w_out,
           f_w_in, f_w_out, final_norm_w):
    bp, lp, d = x_prompt.shape
    bs, ls, _ = x_sample.shape
    depth = mod_w.shape[0]
    n_prompt = bp * lp
    n_sample = bs * ls
    m_all = n_prompt + n_sample
    assert lp % TM == 0 and n_sample % TM == 0 and TM % ls == 0 and bp <= SUBLANES
    assert ls >= SSD_CONV - 1 and T % ls == 0 and lp % T == 0 and GMLP_CHUNK % ls == 0 and lp % MIX_ROWS == 0
    assert depth == 2 and a_w_in.shape[0] == 1 and b_w_in.shape[0] == 1
    cfg = Cfg(n_ptiles=n_prompt // TM, tiles_per_seq=lp // TM, seq_len_s=ls, srow0=SUBLANES)

    x_p = x_prompt.reshape(n_prompt, d)
    x_s = x_sample.reshape(n_sample, d)
    c_all = jnp.concatenate([c_prompt, jnp.zeros((SUBLANES - bp, d), F32), c_sample], axis=0)
    mods = _mod_table(c_all, mod_w, mod_b)

    u, v = _gmlp_in(x_p, x_s, mods, 0, norm_mix_w, a_w_in, a_b_in, 0, cfg)
    r = jnp.arange(GMLP_CHUNK)
    tril = r[None, :] <= r[:, None]
    mask = jnp.stack([tril, jnp.logical_and(tril, (r[:, None] // ls) == (r[None, :] // ls))]).astype(F32)
    rep = GMLP_CHUNK // ls
    wmix = jnp.stack([a_w_s[0], jnp.tile(a_w_s[0, :, :ls, :ls], (1, rep, rep))])
    width = a_w_in.shape[-1] // 2
    bias = jnp.stack([jnp.repeat(a_b_s[0].T, width // GMLP_GROUPS, axis=1),
                      jnp.repeat(jnp.tile(a_b_s[0, :, :ls].T, (rep, 1)), width // GMLP_GROUPS, axis=1)])
    gated, v_p, v_s = _gmlp_mix(u, v, a_ln_w[0], a_ln_b[0], wmix, mask, bias, n_prompt, lp, bp)
    x = _resid_matmul(gated, a_w_out, 0, x_p, x_s, mods, 0, 2, cfg, "gmlp_out")
    act = _ffn_in(x, mods, 0, norm_ffn_w, f_w_in, cfg)
    x = _resid_matmul(act, f_w_out, 0, x, None, mods, 0, 5, cfg, "ffn_out0")

    inner = b_w_out.shape[1]
    conv_dim = b_conv_w.shape[-1]
    n_main = inner + conv_dim
    heads = inner // SSD_HEAD_DIM
    zx, dt_raw = _ssd_in(x, mods, 1, norm_mix_w, jnp.swapaxes(b_w_in, 1, 2)[0], n_main, cfg)
    conv_pad = jnp.pad(state_conv[0], ((0, 0), (ls - (SSD_CONV - 1), 0), (0, 0))).reshape(n_sample, conv_dim)
    ssd_params = (b_conv_w[0], b_conv_b, b_dt_bias, b_a_log,
                  jnp.repeat(b_d[0], SSD_HEAD_DIM).reshape(SSD_GROUPS, 1, GROUP_W),
                  b_norm_w.reshape(SSD_GROUPS, 1, GROUP_W))
    state_s = state_ssm[0].reshape(bs, inner, SSD_STATE)
    yn, ssm_p = _ssd_scan_prompt(zx, dt_raw, *ssd_params, bp, lp)
    yn, ssm_s = _ssd_scan_sample(zx, yn, dt_raw, *ssd_params, state_s, conv_pad, n_prompt // T, ls)
    x = _resid_matmul(yn, b_w_out, 0, x, None, mods, 1, 2, cfg, "ssd_out")
    act = _ffn_in(x, mods, 1, norm_ffn_w, f_w_in, cfg)
    x = _resid_matmul(act, f_w_out, 1, x, None, mods, 1, 5, cfg, "ffn_out1")

    y_p, y_s = _final_norm(x, final_norm_w, n_prompt)

    zx4 = zx.reshape(zx.shape[0], m_all // ls, ls, TN)

    def tails(groups):
        t = groups[SSD_GROUPS:, :, ls - (SSD_CONV - 1):, :]
        return jnp.moveaxis(t, 0, 2).reshape(t.shape[1], SSD_CONV - 1, conv_dim)
    conv_p = tails(zx4[:, lp // ls - 1:n_prompt // ls:lp // ls])
    conv_s = tails(zx4[:, n_prompt // ls:])
    return (y_p.reshape(bp, lp, d), y_s.reshape(bs, ls, d),
            v_p.reshape(1, bp, GMLP_CHUNK, width), v_s.reshape(1, bs, ls, width),
            ssm_p.reshape(1, bp, heads, SSD_HEAD_DIM, SSD_STATE), ssm_s.reshape(1, bs, heads, SSD_HEAD_DIM, SSD_STATE),
            conv_p[None], conv_s[None])
```

```python
import functools
import math
from typing import NamedTuple

import jax
import jax.numpy as jnp
from jax import lax
from jax.experimental import pallas as pl
from jax.experimental.pallas import tpu as pltpu

F32 = jnp.float32
BF16 = jnp.bfloat16

NORM_EPS = 1e-6
LN_EPS = 1e-5

GMLP_GROUPS = 16
GMLP_CHUNK = 128
SSD_HEAD_DIM = 64
SSD_STATE = 128
SSD_GROUPS = 8
SSD_CONV = 4
SSD_CHUNK = 128

SUBLANES = 8
LANES = 128
VMEM_LIMIT_BYTES = 56 * 1024 * 1024

TM = 1024
TN = 512
TN_NARROW = 256
ROW_CHUNK = 256


class Cfg(NamedTuple):
    n_ptiles: int
    tiles_per_seq: int
    seq_len_s: int
    srow0: int


def _cparams(sem):
    return pltpu.CompilerParams(dimension_semantics=sem, vmem_limit_bytes=VMEM_LIMIT_BYTES)


def _silu(x):
    return x / (1.0 + jnp.exp(-x))


def _gelu(x):
    return 0.5 * x * (1.0 + lax.erf(x * (1.0 / math.sqrt(2.0))))


def _rms_mod(x, w, scale, shift):
    y = x * lax.rsqrt(jnp.mean(x * x, axis=-1, keepdims=True) + NORM_EPS)
    return (y * w) * (1.0 + scale) + shift


def _chunk_rows(c):
    return pl.ds(pl.multiple_of(c * ROW_CHUNK, ROW_CHUNK), ROW_CHUNK)


def _sample_rows(i, cfg, x_refs, mod_refs, fn, o_ref):
    rows = o_ref.shape[0]
    nseq = ROW_CHUNK // cfg.seq_len_s
    row0 = cfg.srow0 + (i - cfg.n_ptiles) * (rows // cfg.seq_len_s)

    def body(c, carry):
        rs = _chunk_rows(c)
        r0 = pl.multiple_of(row0 + c * nseq, SUBLANES)
        ms = [m[pl.ds(r0, nseq), :][:, None, :] for m in mod_refs]
        x3 = [x[rs, :].reshape(nseq, cfg.seq_len_s, x.shape[-1]) for x in x_refs]
        o_ref[rs, :] = fn(x3, ms).reshape(ROW_CHUNK, o_ref.shape[-1]).astype(o_ref.dtype)
        return carry
    lax.fori_loop(0, rows // ROW_CHUNK, body, 0)


def _per_seq(i, cfg, xp_refs, xs_refs, mod_refs, fn, o_ref):
    @pl.when(i < cfg.n_ptiles)
    def _():
        s = i // cfg.tiles_per_seq
        ms = [m[pl.ds(s, 1), :] for m in mod_refs]

        def body(c, carry):
            rs = _chunk_rows(c)
            o_ref[rs, :] = fn([x[rs, :] for x in xp_refs], ms).astype(o_ref.dtype)
            return carry
        lax.fori_loop(0, o_ref.shape[0] // ROW_CHUNK, body, 0)

    @pl.when(i >= cfg.n_ptiles)
    def _():
        _sample_rows(i, cfg, xs_refs, mod_refs, fn, o_ref)


def _prompt_block(cfg):
    return lambda i: jnp.minimum(i, cfg.n_ptiles - 1)


def _sample_block(cfg):
    return lambda i: jnp.maximum(i - cfg.n_ptiles, 0)


def _mod_body(c_ref, w_ref, b_ref, o_ref):
    sc = _silu(c_ref[...]).astype(BF16)
    o_ref[...] = jnp.dot(sc, w_ref[...].astype(BF16), preferred_element_type=F32) + b_ref[...]


def _mod_table(c_all, mod_w, mod_b):
    depth, d, n = mod_w.shape
    r = c_all.shape[0]
    tn = 1024
    return pl.pallas_call(
        _mod_body,
        grid=(depth, n // tn),
        in_specs=[
            pl.BlockSpec((r, d), lambda l, j: (0, 0)),
            pl.BlockSpec((None, d, tn), lambda l, j: (l, 0, j)),
            pl.BlockSpec((None, 1, tn), lambda l, j: (l, 0, j)),
        ],
        out_specs=pl.BlockSpec((None, r, tn), lambda l, j: (l, 0, j)),
        out_shape=jax.ShapeDtypeStruct((depth, r, n), F32),
        compiler_params=_cparams(("arbitrary", "arbitrary")),
        name="mod_table",
    )(c_all, mod_w, mod_b.reshape(depth, 1, n))


def _norm_prologue(i, j, cfg, xp_ref, xs_ref, nw_ref, sc_ref, sh_ref, h_ref):
    @pl.when(j == 0)
    def _():
        _per_seq(i, cfg, [xp_ref], [xs_ref], [sc_ref, sh_ref],
                 lambda xs, ms: _rms_mod(xs[0], nw_ref[...], ms[0], ms[1]), h_ref)


def _gmlp_in_body(xp_ref, xs_ref, nw_ref, sh_ref, sc_ref, wu_ref, wv_ref, bu_ref, bv_ref, u_ref, v_ref, h_ref, *, cfg):
    i, j = pl.program_id(0), pl.program_id(1)
    _norm_prologue(i, j, cfg, xp_ref, xs_ref, nw_ref, sc_ref, sh_ref, h_ref)
    h = h_ref[...]
    u_ref[...] = _gelu(jnp.dot(h, wu_ref[...].astype(BF16), preferred_element_type=F32) + bu_ref[...])
    v_ref[...] = _gelu(jnp.dot(h, wv_ref[...].astype(BF16), preferred_element_type=F32) + bv_ref[...])


def _gmlp_in(x_p, x_s, mods, layer, norm_w, w_in, b_in, j_layer, cfg):
    d = x_p.shape[1]
    m = x_p.shape[0] + x_s.shape[0]
    width = w_in.shape[-1] // 2
    tn = TN_NARROW
    nj = width // tn
    r = mods.shape[1]
    pb, sb = _prompt_block(cfg), _sample_block(cfg)
    b2 = b_in.reshape(b_in.shape[0], 1, -1)
    return pl.pallas_call(
        functools.partial(_gmlp_in_body, cfg=cfg),
        grid=(m // TM, nj),
        in_specs=[
            pl.BlockSpec((TM, d), lambda i, j: (pb(i), 0)),
            pl.BlockSpec((TM, d), lambda i, j: (sb(i), 0), pipeline_mode=pl.Buffered(1)),
            pl.BlockSpec((None, 1, d), lambda i, j: (layer, 0, 0)),
            pl.BlockSpec((None, r, d), lambda i, j: (layer, 0, 0)),
            pl.BlockSpec((None, r, d), lambda i, j: (layer, 0, 1)),
            pl.BlockSpec((None, d, tn), lambda i, j: (j_layer, 0, j)),
            pl.BlockSpec((None, d, tn), lambda i, j: (j_layer, 0, j + nj)),
            pl.BlockSpec((None, 1, tn), lambda i, j: (j_layer, 0, j)),
            pl.BlockSpec((None, 1, tn), lambda i, j: (j_layer, 0, j + nj)),
        ],
        out_specs=[pl.BlockSpec((TM, tn), lambda i, j: (i, j)), pl.BlockSpec((TM, tn), lambda i, j: (i, j))],
        out_shape=[jax.ShapeDtypeStruct((m, width), F32), jax.ShapeDtypeStruct((m, width), F32)],
        scratch_shapes=[pltpu.VMEM((TM, d), BF16)],
        compiler_params=_cparams(("arbitrary", "arbitrary")),
        name="gmlp_in",
    )(x_p, x_s, norm_w.reshape(-1, 1, d), mods, mods, w_in, w_in, b2, b2)


def _ffn_in_body(x_ref, nw_ref, sh_ref, sc_ref, wg_ref, wu_ref, a_ref, h_ref, *, cfg):
    i, j = pl.program_id(0), pl.program_id(1)
    _norm_prologue(i, j, cfg, x_ref, x_ref, nw_ref, sc_ref, sh_ref, h_ref)
    h = h_ref[...]
    gate = jnp.dot(h, wg_ref[...].astype(BF16), preferred_element_type=F32)
    up = jnp.dot(h, wu_ref[...].astype(BF16), preferred_element_type=F32)
    a_ref[...] = (_silu(gate) * up).astype(BF16)


def _ffn_in(x, mods, layer, norm_w, w_in, cfg):
    m, d = x.shape
    hidden = w_in.shape[-1] // 2
    nj = hidden // TN
    r = mods.shape[1]
    return pl.pallas_call(
        functools.partial(_ffn_in_body, cfg=cfg),
        grid=(m // TM, nj),
        in_specs=[
            pl.BlockSpec((TM, d), lambda i, j: (i, 0)),
            pl.BlockSpec((None, 1, d), lambda i, j: (layer, 0, 0)),
            pl.BlockSpec((None, r, d), lambda i, j: (layer, 0, 3)),
            pl.BlockSpec((None, r, d), lambda i, j: (layer, 0, 4)),
            pl.BlockSpec((None, d, TN), lambda i, j: (layer, 0, j)),
            pl.BlockSpec((None, d, TN), lambda i, j: (layer, 0, j + nj)),
        ],
        out_specs=pl.BlockSpec((TM, TN), lambda i, j: (i, j)),
        out_shape=jax.ShapeDtypeStruct((m, hidden), BF16),
        scratch_shapes=[pltpu.VMEM((TM, d), BF16)],
        compiler_params=_cparams(("arbitrary", "arbitrary")),
        name="ffn_in",
    )(x, norm_w.reshape(-1, 1, d), mods, mods, w_in, w_in)


_NT = (((1,), (1,)), ((), ()))


def _ssd_in_body(x_ref, nw_ref, sh_ref, sc_ref, w_ref, wdt_ref, o_ref, dt_ref, h_ref, *, cfg):
    i, j = pl.program_id(0), pl.program_id(1)
    _norm_prologue(i, j, cfg, x_ref, x_ref, nw_ref, sc_ref, sh_ref, h_ref)
    h = h_ref[...]
    o_ref[...] = lax.dot_general(h, w_ref[...].astype(BF16), _NT, preferred_element_type=F32)

    @pl.when(j == 0)
    def _():
        dt_ref[...] = lax.dot_general(h, wdt_ref[...].astype(BF16), _NT, preferred_element_type=F32)


def _ssd_in(x, mods, layer, norm_w, w_in_t, n_main, cfg):
    m, d = x.shape
    r = mods.shape[1]
    n_dt = w_in_t.shape[0] - n_main
    nj = n_main // TN
    return pl.pallas_call(
        functools.partial(_ssd_in_body, cfg=cfg),
        grid=(m // TM, nj),
        in_specs=[
            pl.BlockSpec((TM, d), lambda i, j: (i, 0)),
            pl.BlockSpec((None, 1, d), lambda i, j: (layer, 0, 0)),
            pl.BlockSpec((None, r, d), lambda i, j: (layer, 0, 0)),
            pl.BlockSpec((None, r, d), lambda i, j: (layer, 0, 1)),
            pl.BlockSpec((TN, d), lambda i, j: (j, 0)),
            pl.BlockSpec((n_dt, d), lambda i, j: (n_main // n_dt, 0)),
        ],
        out_specs=[pl.BlockSpec((None, TM, TN), lambda i, j: (j, i, 0)), pl.BlockSpec((TM, n_dt), lambda i, j: (i, 0))],
        out_shape=[jax.ShapeDtypeStruct((nj, m, TN), F32), jax.ShapeDtypeStruct((m, n_dt), F32)],
        scratch_shapes=[pltpu.VMEM((TM, d), BF16)],
        compiler_params=_cparams(("arbitrary", "arbitrary")),
        name="ssd_in",
    )(x, norm_w.reshape(-1, 1, d), mods, mods, w_in_t, w_in_t)


TK = 512


def _resid_body(a_ref, *rest, cfg, nk, two_a):
    as_ref = None
    if two_a:
        as_ref, *rest = rest
    w_ref, *rest = rest
    if len(rest) == 4:
        xp_ref, xs_ref, g_ref, o_ref = rest
    else:
        xp_ref, g_ref, o_ref = rest
        xs_ref = xp_ref
    i, k = pl.program_id(0), pl.program_id(1)
    d = o_ref.shape[1]

    def accumulate(first):
        a = a_ref[...] if as_ref is None else jnp.where(i < cfg.n_ptiles, a_ref[...], as_ref[...])
        for c in range(d // TN):
            cols = slice(c * TN, (c + 1) * TN)
            part = jnp.dot(a, w_ref[:, cols].astype(BF16), preferred_element_type=F32)
            if first:
                o_ref[:, cols] = part
            else:
                o_ref[:, cols] += part

    @pl.when(k == 0)
    def _():
        accumulate(True)

    @pl.when(k > 0)
    def _():
        accumulate(False)

    @pl.when(k == nk - 1)
    def _():
        _per_seq(i, cfg, [xp_ref, o_ref], [xs_ref, o_ref], [g_ref], lambda xs, ms: xs[0] + ms[0] * xs[1], o_ref)


def _resid_matmul(a, w, w_layer, x_p, x_s, mods, layer, gate_chunk, cfg, name, a_s=None):
    m = a.shape[-2] + (0 if a_s is None else a_s.shape[-2])
    d = x_p.shape[1]
    r = mods.shape[1]
    pb, sb = _prompt_block(cfg), _sample_block(cfg)
    if a.ndim == 2:
        nk = a.shape[1] // TK
        a_specs, a_args = [pl.BlockSpec((TM, TK), lambda i, k: (i, k))], [a]
    else:
        nk = a.shape[0]
        assert a.shape[2] == TK
        if a_s is None:
            a_specs, a_args = [pl.BlockSpec((None, TM, TK), lambda i, k: (k, i, 0))], [a]
        else:
            a_specs = [pl.BlockSpec((None, TM, TK), lambda i, k: (k, pb(i), 0)),
                       pl.BlockSpec((None, TM, TK), lambda i, k: (k, sb(i), 0))]
            a_args = [a, a_s]
    if x_s is None:
        x_specs, xs = [pl.BlockSpec((TM, d), lambda i, k: (i, 0))], [x_p]
    else:
        x_specs = [pl.BlockSpec((TM, d), lambda i, k: (pb(i), 0)),
                   pl.BlockSpec((TM, d), lambda i, k: (sb(i), 0), pipeline_mode=pl.Buffered(1))]
        xs = [x_p, x_s]
    return pl.pallas_call(
        functools.partial(_resid_body, cfg=cfg, nk=nk, two_a=a_s is not None),
        grid=(m // TM, nk),
        in_specs=a_specs + [pl.BlockSpec((None, TK, d), lambda i, k: (w_layer, k, 0))] + x_specs
        + [pl.BlockSpec((None, r, d), lambda i, k: (layer, 0, gate_chunk))],
        out_specs=pl.BlockSpec((TM, d), lambda i, k: (i, 0)),
        out_shape=jax.ShapeDtypeStruct((m, d), F32),
        compiler_params=_cparams(("arbitrary", "arbitrary")),
        name=name,
    )(*a_args, w, *xs, mods)


MIX_ROWS = 2 * GMLP_CHUNK


def _gmlp_mix_body(u_ref, v_ref, lnw_ref, lnb_ref, ws_ref, mask_ref, bias_ref, g_ref, vp_ref, vs_ref,
                   vn_ref, *, n_prompt_steps, steps_per_seq):
    t = pl.program_id(0)
    v = v_ref[...]
    xc = v - jnp.mean(v, axis=-1, keepdims=True)
    vn = xc * lax.rsqrt(jnp.mean(xc * xc, axis=-1, keepdims=True) + LN_EPS) * lnw_ref[...] + lnb_ref[...]
    vn_ref[...] = vn

    @pl.when(jnp.logical_and(t < n_prompt_steps, t % steps_per_seq == steps_per_seq - 1))
    def _():
        vp_ref[...] = vn[MIX_ROWS - GMLP_CHUNK:, :]

    @pl.when(t >= n_prompt_steps)
    def _():
        vs_ref[...] = vn

    mask = mask_ref[...]
    for g in range(GMLP_GROUPS):
        wb = (ws_ref[g] * mask).astype(BF16)
        cols = slice(g * GMLP_CHUNK, (g + 1) * GMLP_CHUNK)
        for c in range(MIX_ROWS // GMLP_CHUNK):
            rows = slice(c * GMLP_CHUNK, (c + 1) * GMLP_CHUNK)
            s = jnp.dot(wb, vn_ref[rows, cols].astype(BF16), preferred_element_type=F32) + bias_ref[:, cols]
            g_ref[rows, cols] = (u_ref[rows, cols] * s).astype(BF16)


def _gmlp_mix(u, v, ln_w, ln_b, wmix, mask, bias, n_prompt_rows, seq_len, n_prompt_seq):
    m, width = u.shape
    n_prompt_steps = n_prompt_rows // MIX_ROWS
    steps_per_seq = seq_len // MIX_ROWS
    n_sample_rows = m - n_prompt_rows

    def variant(t):
        return jnp.where(t < n_prompt_steps, 0, 1)

    return pl.pallas_call(
        functools.partial(_gmlp_mix_body, n_prompt_steps=n_prompt_steps, steps_per_seq=steps_per_seq),
        grid=(m // MIX_ROWS,),
        in_specs=[
            pl.BlockSpec((MIX_ROWS, width), lambda t: (t, 0)),
            pl.BlockSpec((MIX_ROWS, width), lambda t: (t, 0)),
            pl.BlockSpec((1, width), lambda t: (0, 0)),
            pl.BlockSpec((1, width), lambda t: (0, 0)),
            pl.BlockSpec((None, GMLP_GROUPS, GMLP_CHUNK, GMLP_CHUNK), lambda t: (variant(t), 0, 0, 0)),
            pl.BlockSpec((None, GMLP_CHUNK, GMLP_CHUNK), lambda t: (variant(t), 0, 0)),
            pl.BlockSpec((None, GMLP_CHUNK, width), lambda t: (variant(t), 0, 0)),
        ],
        out_specs=[
            pl.BlockSpec((MIX_ROWS, width), lambda t: (t, 0)),
            pl.BlockSpec((GMLP_CHUNK, width), lambda t: (jnp.minimum(t // steps_per_seq, n_prompt_seq - 1), 0)),
            pl.BlockSpec((MIX_ROWS, width), lambda t: (jnp.maximum(t - n_prompt_steps, 0), 0)),
        ],
        out_shape=[
            jax.ShapeDtypeStruct((m, width), BF16),
            jax.ShapeDtypeStruct((n_prompt_seq * GMLP_CHUNK, width), F32),
            jax.ShapeDtypeStruct((n_sample_rows, width), F32),
        ],
        scratch_shapes=[pltpu.VMEM((MIX_ROWS, width), F32)],
        compiler_params=_cparams(("arbitrary",)),
        name="gmlp_mix",
    )(u, v, ln_w.reshape(1, width), ln_b.reshape(1, width), wmix, mask, bias)


GROUP_HEADS = 8
GROUP_W = GROUP_HEADS * SSD_HEAD_DIM
T = SSD_CHUNK
BC_PER_BLOCK = GROUP_W // SSD_STATE


def _split3(x):
    hi = x.astype(BF16)
    r = x - hi.astype(F32)
    mid = r.astype(BF16)
    lo = (r - mid.astype(F32)).astype(BF16)
    return hi, mid, lo


def _dot_exact_rhs(m_b, pieces):
    return sum(jnp.dot(m_b, p, preferred_element_type=F32) for p in pieces)


def _dot_exact_lhs(pieces, e_b):
    return sum(jnp.dot(p, e_b, preferred_element_type=F32) for p in pieces)


def _ssd_group(xs, bm, cm, z, dt_x, cs_x, cs_end_x, cs_row, d_a, states, n_seg, dsk, nw, keep):
    seg = T // n_seg
    xdt = xs * dt_x
    ecs_x = jnp.exp(cs_x)
    dte_x = jnp.exp(cs_end_x - cs_x)
    xdt_b = xdt.astype(BF16)
    xd_t = (xdt * dte_x).T.astype(BF16)
    bmb = bm.astype(BF16)
    cmb = cm.astype(BF16)
    cb = lax.dot_general(cmb, bmb, _NT, preferred_element_type=F32)

    row = lax.broadcasted_iota(jnp.int32, (T, SSD_STATE), 0)
    y_offs, new_states = [], []
    for s in range(n_seg):
        st = states[s]
        c_seg = cmb if n_seg == 1 else cm[s * seg:(s + 1) * seg].astype(BF16)
        y_offs.append(lax.dot_general(c_seg, st.astype(BF16), _NT, preferred_element_type=F32))
        if n_seg == 1:
            b_seg = bmb
        else:
            b_seg = jnp.where(jnp.logical_and(row >= s * seg, row < (s + 1) * seg), bm, 0.0).astype(BF16)
        upd = jnp.dot(xd_t, b_seg, preferred_element_type=F32)
        decayed = jnp.concatenate(
            [st[h * SSD_HEAD_DIM:(h + 1) * SSD_HEAD_DIM, :] * d_a(s, h) for h in range(GROUP_HEADS)], axis=0)
        new_states.append(decayed + upd)
    y_off = jnp.concatenate(y_offs, axis=0) if n_seg > 1 else y_offs[0]

    lane = lax.broadcasted_iota(jnp.int32, (T, LANES), 1)
    lo_half = lane < SSD_HEAD_DIM
    ys = []
    for q in range(GROUP_HEADS // 2):
        v = cs_x[:, q * LANES:(q + 1) * LANES]
        r = pltpu.roll(v, SSD_HEAD_DIM, 1)
        cols = (jnp.where(lo_half, v, r), jnp.where(lo_half, r, v))
        ws = []
        for e in range(2):
            diff = cols[e] - cs_row(2 * q + e)
            ws.append((cb * jnp.exp(jnp.where(keep, diff, -jnp.inf))).astype(BF16))
        xp = xdt_b[:, q * LANES:(q + 1) * LANES]
        zero = jnp.zeros_like(xp)
        xpair = jnp.concatenate([jnp.where(lo_half, xp, zero), jnp.where(lo_half, zero, xp)], axis=0)
        ys.append(jnp.dot(jnp.concatenate(ws, axis=1), xpair, preferred_element_type=F32))
    y_diag = jnp.concatenate(ys, axis=1)

    y = y_diag + y_off * ecs_x + dsk * xs
    y = y * _silu(z)
    yn = y * lax.rsqrt(jnp.mean(y * y, axis=-1, keepdims=True) + NORM_EPS) * nw
    return yn.astype(BF16), new_states


def _causal_keep(seq_len):
    r = lax.broadcasted_iota(jnp.int32, (T, T), 0)
    c = lax.broadcasted_iota(jnp.int32, (T, T), 1)
    keep = c <= r
    if seq_len < T:
        keep = jnp.logical_and(keep, (r // seq_len) == (c // seq_len))
    return keep


def _ssd_prompt_body(z_ref, xs_ref, bc_ref, dt_ref, cw_ref, cbias_ref, dtb_ref, alog_ref, dsk_ref, nw_ref,
                     m_ref, e_ref, yn_ref, st_ref,
                     cbuf_ref, xc_ref, bcs_ref, dt3_ref, cs3_ref, cst_ref, *, c):
    n_x = xs_ref.shape[0]
    pad = SUBLANES
    taps = SSD_CONV - 1

    @pl.when(c == 0)
    def _():
        st_ref[...] = jnp.zeros_like(st_ref)
        cbuf_ref[:, 0:pad, :] = jnp.zeros((cbuf_ref.shape[0], pad, GROUP_W), F32)

    def conv(k, src):
        w = cw_ref[k]
        cbuf_ref[k, pad:pad + T, :] = src
        acc = cbias_ref[k] + src * w[taps:taps + 1, :]
        for kk in range(taps):
            acc = acc + cbuf_ref[k, pad - taps + kk:pad - taps + kk + T, :] * w[kk:kk + 1, :]
        cbuf_ref[k, pad - taps:pad, :] = cbuf_ref[k, pad + T - taps:pad + T, :]
        return _silu(acc)

    def conv_x(k, carry):
        xc_ref[k] = conv(k, xs_ref[k])
        return carry
    lax.fori_loop(0, n_x, conv_x, 0)

    def conv_bc(k, carry):
        out = conv(n_x + k, bc_ref[k])
        for gg in range(BC_PER_BLOCK):
            bcs_ref[k * BC_PER_BLOCK + gg] = out[:, gg * SSD_STATE:(gg + 1) * SSD_STATE]
        return carry
    lax.fori_loop(0, bc_ref.shape[0], conv_bc, 0)

    dt = jax.nn.softplus(dt_ref[...] + dtb_ref[...])
    a = dt * (-jnp.exp(alog_ref[...]))
    cs = _dot_exact_rhs(m_ref[...], _split3(a))
    cst_ref[...] = cs.T
    for p, (dt_p, cs_p) in enumerate(zip(_split3(dt), _split3(cs))):
        dt3_ref[p] = dt_p
        cs3_ref[p] = cs_p

    keep = _causal_keep(T)

    def spread(g):
        e_g = e_ref[g]
        return (_dot_exact_lhs([dt3_ref[p] for p in range(3)], e_g), _dot_exact_lhs([cs3_ref[p] for p in range(3)], e_g))

    def group(g, dt_x, cs_x):
        h0 = g * GROUP_HEADS
        rows = pl.ds(g * GROUP_W, GROUP_W)
        yn, new_states = _ssd_group(
            xc_ref[g], bcs_ref[g], bcs_ref[SSD_GROUPS + g], z_ref[g], dt_x, cs_x, cs_x[T - 1:T, :],
            lambda h: cst_ref[pl.ds(h0 + h, 1), :],
            lambda s, h: jnp.exp(cst_ref[pl.ds(h0 + h, 1), T - 1:T]),
            [st_ref[rows, :]], 1, dsk_ref[g], nw_ref[g], keep)
        yn_ref[g] = yn
        st_ref[rows, :] = new_states[0]

    spread_next = spread(0)
    for g in range(SSD_GROUPS):
        dt_x, cs_x = spread_next
        if g + 1 < SSD_GROUPS:
            spread_next = spread(g + 1)
        group(g, dt_x, cs_x)


def _ssd_sample_body(z_ref, xs_ref, b_ref, c_ref, px_ref, pb_ref, pc_ref, dt_ref, wx_ref, wb_ref, wc_ref,
                     bx_ref, bb_ref, bc_ref, dtb_ref, alog_ref, dsk_ref, nw_ref, m_ref, mseg_ref, e_ref,
                     st_in_ref, yn_ref, st_ref, cst_ref, cet_ref, *, seq_len, g):
    n_seg = T // seq_len

    def conv(x_ref, p_ref, w_ref, bias_ref):
        x = x_ref[...]
        p = p_ref[...]
        tpos = lax.broadcasted_iota(jnp.int32, x.shape, 0) % seq_len
        acc = bias_ref[...] + x * w_ref[SSD_CONV - 1:SSD_CONV, :]
        for sh in range(1, SSD_CONV):
            shifted = jnp.where(tpos >= sh, pltpu.roll(x, sh, 0), pltpu.roll(p, T - seq_len + sh, 0))
            acc = acc + shifted * w_ref[SSD_CONV - 1 - sh:SSD_CONV - sh, :]
        return _silu(acc)

    xs = conv(xs_ref, px_ref, wx_ref, bx_ref)
    bm = conv(b_ref, pb_ref, wb_ref, bb_ref)
    cm = conv(c_ref, pc_ref, wc_ref, bc_ref)

    dt = jax.nn.softplus(dt_ref[...] + dtb_ref[...])
    a3 = _split3(dt * (-jnp.exp(alog_ref[...])))
    cs = _dot_exact_rhs(m_ref[...], a3)
    cs_end = _dot_exact_rhs(mseg_ref[...], a3)
    cst_ref[...] = cs.T
    cet_ref[...] = cs_end.T
    e_g = e_ref[...]
    h0 = g * GROUP_HEADS
    yn, new_states = _ssd_group(
        xs, bm, cm, z_ref[...], _dot_exact_lhs(_split3(dt), e_g), _dot_exact_lhs(_split3(cs), e_g),
        _dot_exact_lhs(_split3(cs_end), e_g),
        lambda h: cst_ref[pl.ds(h0 + h, 1), :],
        lambda s, h: jnp.exp(cet_ref[pl.ds(h0 + h, 1), s * seq_len:s * seq_len + 1]),
        [st_in_ref[s] for s in range(n_seg)], n_seg, dsk_ref[...], nw_ref[...], _causal_keep(seq_len))
    yn_ref[...] = yn
    for s in range(n_seg):
        st_ref[s] = new_states[s]


def _ssd_masks(seq_len):
    r = jnp.arange(T)
    same = (r[:, None] // seq_len) == (r[None, :] // seq_len)
    return jnp.logical_and(same, r[None, :] <= r[:, None]).astype(BF16), same.astype(BF16)


N_PROMPT_IN, N_SAMPLE_IN = 12, 22


def _ssd_scan_body(*refs, n_chunks, seq_len_s):
    s = pl.program_id(0)
    p_in = refs[:N_PROMPT_IN]
    s_in = refs[N_PROMPT_IN:N_PROMPT_IN + N_SAMPLE_IN]
    yn_p, st_p, yn_s, st_s = refs[N_PROMPT_IN + N_SAMPLE_IN:N_PROMPT_IN + N_SAMPLE_IN + 4]
    scratch = refs[N_PROMPT_IN + N_SAMPLE_IN + 4:]
    _ssd_prompt_body(*p_in, yn_p, st_p, *scratch[:6], c=s % n_chunks)
    _ssd_sample_body(*s_in, yn_s, st_s, *scratch[6:], seq_len=seq_len_s, g=s % SSD_GROUPS)


def _ssd_scan(zx, dt_raw, conv_w, conv_b, dtb, alog, dsk, nw, state_s, conv_pad_s, n_pseq, seq_len_p, seq_len_s):
    heads = dt_raw.shape[1]
    inner = SSD_GROUPS * GROUP_W
    n_chunks = seq_len_p // T
    m_p = n_pseq * seq_len_p
    n_seg = T // seq_len_s
    n_sseq = state_s.shape[0]
    row0 = m_p // T
    n_steps = n_pseq * n_chunks
    assert n_steps == (n_sseq // n_seg) * SSD_GROUPS
    n_bc = 2 * SSD_GROUPS // BC_PER_BLOCK
    n_cblk = SSD_GROUPS + n_bc
    cw = conv_w.reshape(SSD_CONV, n_cblk, GROUP_W).transpose(1, 0, 2)
    cbias = conv_b.reshape(n_cblk, 1, GROUP_W)
    e_all = (jnp.arange(inner)[None, :] // SSD_HEAD_DIM == jnp.arange(heads)[:, None]).astype(BF16)
    e_grp = e_all.reshape(heads, SSD_GROUPS, GROUP_W).transpose(1, 0, 2)
    m_tril, _ = _ssd_masks(T)
    m_s, mseg_s = _ssd_masks(seq_len_s)
    xoff = SSD_GROUPS
    boff = 2 * SSD_GROUPS
    coff = boff + SSD_GROUPS // BC_PER_BLOCK
    cb0 = inner // SSD_STATE
    cc0 = cb0 + SSD_GROUPS

    full = lambda shape: pl.BlockSpec(shape, lambda s: (0,) * len(shape))
    bp = lambda s: s // n_chunks
    ts = lambda s: s // SSD_GROUPS
    gs = lambda s: s % SSD_GROUPS
    prompt_in = [
        pl.BlockSpec((SSD_GROUPS, T, GROUP_W), lambda s: (0, s, 0)),
        pl.BlockSpec((SSD_GROUPS, T, GROUP_W), lambda s: (1, s, 0)),
        pl.BlockSpec((n_bc, T, GROUP_W), lambda s: (2 * SSD_GROUPS // n_bc, s, 0)),
        pl.BlockSpec((T, heads), lambda s: (s, 0)),
        full((n_cblk, SSD_CONV, GROUP_W)), full((n_cblk, 1, GROUP_W)), full((1, heads)), full((1, heads)),
        full((SSD_GROUPS, 1, GROUP_W)), full((SSD_GROUPS, 1, GROUP_W)), full((T, T)),
        full((SSD_GROUPS, heads, GROUP_W)),
    ]
    per_g = lambda shape: pl.BlockSpec(shape, lambda s: (gs(s),) + (0,) * (len(shape) - 1))
    sample_in = [
        pl.BlockSpec((None, T, GROUP_W), lambda s: (gs(s), row0 + ts(s), 0)),
        pl.BlockSpec((None, T, GROUP_W), lambda s: (xoff + gs(s), row0 + ts(s), 0)),
        pl.BlockSpec((None, T, SSD_STATE),
                     lambda s: (boff + gs(s) // BC_PER_BLOCK, row0 + ts(s), gs(s) % BC_PER_BLOCK)),
        pl.BlockSpec((None, T, SSD_STATE),
                     lambda s: (coff + gs(s) // BC_PER_BLOCK, row0 + ts(s), gs(s) % BC_PER_BLOCK)),
        pl.BlockSpec((T, GROUP_W), lambda s: (ts(s), gs(s))),
        pl.BlockSpec((T, SSD_STATE), lambda s: (ts(s), cb0 + gs(s))),
        pl.BlockSpec((T, SSD_STATE), lambda s: (ts(s), cc0 + gs(s))),
        pl.BlockSpec((T, heads), lambda s: (row0 + ts(s), 0)),
        pl.BlockSpec((SSD_CONV, GROUP_W), lambda s: (0, gs(s))),
        pl.BlockSpec((SSD_CONV, SSD_STATE), lambda s: (0, cb0 + gs(s))),
        pl.BlockSpec((SSD_CONV, SSD_STATE), lambda s: (0, cc0 + gs(s))),
        pl.BlockSpec((1, GROUP_W), lambda s: (0, gs(s))),
        pl.BlockSpec((1, SSD_STATE), lambda s: (0, cb0 + gs(s))),
        pl.BlockSpec((1, SSD_STATE), lambda s: (0, cc0 + gs(s))),
        full((1, heads)), full((1, heads)),
        per_g((None, 1, GROUP_W)), per_g((None, 1, GROUP_W)),
        full((T, T)), full((T, T)),
        pl.BlockSpec((heads, GROUP_W), lambda s: (0, gs(s))),
        pl.BlockSpec((n_seg, GROUP_W, SSD_STATE), lambda s: (ts(s), gs(s), 0)),
    ]
    assert len(prompt_in) == N_PROMPT_IN and len(sample_in) == N_SAMPLE_IN
    return pl.pallas_call(
        functools.partial(_ssd_scan_body, n_chunks=n_chunks, seq_len_s=seq_len_s),
        grid=(n_steps,),
        in_specs=prompt_in + sample_in,
        out_specs=[pl.BlockSpec((SSD_GROUPS, T, GROUP_W), lambda s: (0, s, 0)),
                   pl.BlockSpec((None, inner, SSD_STATE), lambda s: (bp(s), 0, 0)),
                   pl.BlockSpec((None, T, GROUP_W), lambda s: (gs(s), ts(s), 0)),
                   pl.BlockSpec((n_seg, GROUP_W, SSD_STATE), lambda s: (ts(s), gs(s), 0))],
        out_shape=[jax.ShapeDtypeStruct((SSD_GROUPS, m_p, GROUP_W), BF16),
                   jax.ShapeDtypeStruct((n_pseq, inner, SSD_STATE), F32),
                   jax.ShapeDtypeStruct((SSD_GROUPS, n_sseq * seq_len_s, GROUP_W), BF16),
                   jax.ShapeDtypeStruct(state_s.shape, F32)],
        scratch_shapes=[pltpu.VMEM((n_cblk, SUBLANES + T, GROUP_W), F32),
                        pltpu.VMEM((SSD_GROUPS, T, GROUP_W), F32),
                        pltpu.VMEM((2 * SSD_GROUPS, T, SSD_STATE), F32),
                        pltpu.VMEM((3, T, heads), BF16),
                        pltpu.VMEM((3, T, heads), BF16),
                        pltpu.VMEM((heads, T), F32),
                        pltpu.VMEM((heads, T), F32), pltpu.VMEM((heads, T), F32)],
        compiler_params=_cparams(("arbitrary",)),
        name="ssd_scan",
    )(zx, zx, zx, dt_raw, cw, cbias, dtb, alog, dsk, nw, m_tril, e_grp,
      zx, zx, zx, zx, conv_pad_s, conv_pad_s, conv_pad_s, dt_raw, conv_w, conv_w, conv_w, conv_b, conv_b, conv_b,
      dtb, alog, dsk, nw, m_s, mseg_s, e_all, state_s)


def _final_body(x_ref, w_ref, yp_ref, ys_ref, *, n_ptiles):
    i = pl.program_id(0)
    x = x_ref[...]
    y = x * lax.rsqrt(jnp.mean(x * x, axis=-1, keepdims=True) + NORM_EPS) * w_ref[...]

    @pl.when(i < n_ptiles)
    def _():
        yp_ref[...] = y

    @pl.when(i >= n_ptiles)
    def _():
        ys_ref[...] = y


def _final_norm(x, w, n_prompt_rows):
    m, d = x.shape
    tm = 512
    n_ptiles = n_prompt_rows // tm
    return pl.pallas_call(
        functools.partial(_final_body, n_ptiles=n_ptiles),
        grid=(m // tm,),
        in_specs=[pl.BlockSpec((tm, d), lambda i: (i, 0)), pl.BlockSpec((1, d), lambda i: (0, 0))],
        out_specs=[pl.BlockSpec((tm, d), lambda i: (jnp.minimum(i, n_ptiles - 1), 0)),
                   pl.BlockSpec((tm, d), lambda i: (jnp.maximum(i - n_ptiles, 0), 0))],
        out_shape=[jax.ShapeDtypeStruct((n_prompt_rows, d), F32), jax.ShapeDtypeStruct((m - n_prompt_rows, d), F32)],
        compiler_params=_cparams(("arbitrary",)),
        name="final_norm",
    )(x, w.reshape(1, d))


def kernel(x_prompt, x_sample, c_prompt, c_sample, state_ssm, state_conv, mod_w, mod_b, norm_mix_w, norm_ffn_w,
           a_w_in, a_b_in, a_ln_w, a_ln_b, a_w_s, a_b_s, a_w_out,
           b_w_in, b_conv_w, b_conv_b, b_dt_bias, b_a_log, b_d, b_norm_w, b_w_out,
           f_w_in, f_w_out, final_norm_w):
    bp, lp, d = x_prompt.shape
    bs, ls, _ = x_sample.shape
    depth = mod_w.shape[0]
    n_prompt = bp * lp
    n_sample = bs * ls
    m_all = n_prompt + n_sample
    assert lp % TM == 0 and n_sample % TM == 0 and TM % ls == 0 and bp <= SUBLANES
    assert ls >= SSD_CONV - 1 and T % ls == 0 and lp % T == 0 and GMLP_CHUNK % ls == 0 and lp % MIX_ROWS == 0
    assert depth == 2 and a_w_in.shape[0] == 1 and b_w_in.shape[0] == 1
    cfg = Cfg(n_ptiles=n_prompt // TM, tiles_per_seq=lp // TM, seq_len_s=ls, srow0=SUBLANES)

    x_p = x_prompt.reshape(n_prompt, d)
    x_s = x_sample.reshape(n_sample, d)
    c_all = jnp.concatenate([c_prompt, jnp.zeros((SUBLANES - bp, d), F32), c_sample], axis=0)
    mods = _mod_table(c_all, mod_w, mod_b)

    u, v = _gmlp_in(x_p, x_s, mods, 0, norm_mix_w, a_w_in, a_b_in, 0, cfg)
    r = jnp.arange(GMLP_CHUNK)
    tril = r[None, :] <= r[:, None]
    mask = jnp.stack([tril, jnp.logical_and(tril, (r[:, None] // ls) == (r[None, :] // ls))]).astype(F32)
    rep = GMLP_CHUNK // ls
    wmix = jnp.stack([a_w_s[0], jnp.tile(a_w_s[0, :, :ls, :ls], (1, rep, rep))])
    width = a_w_in.shape[-1] // 2
    bias = jnp.stack([jnp.repeat(a_b_s[0].T, width // GMLP_GROUPS, axis=1),
                      jnp.repeat(jnp.tile(a_b_s[0, :, :ls].T, (rep, 1)), width // GMLP_GROUPS, axis=1)])
    gated, v_p, v_s = _gmlp_mix(u, v, a_ln_w[0], a_ln_b[0], wmix, mask, bias, n_prompt, lp, bp)
    x = _resid_matmul(gated, a_w_out, 0, x_p, x_s, mods, 0, 2, cfg, "gmlp_out")
    act = _ffn_in(x, mods, 0, norm_ffn_w, f_w_in, cfg)
    x = _resid_matmul(act, f_w_out, 0, x, None, mods, 0, 5, cfg, "ffn_out0")

    inner = b_w_out.shape[1]
    conv_dim = b_conv_w.shape[-1]
    n_main = inner + conv_dim
    heads = inner // SSD_HEAD_DIM
    zx, dt_raw = _ssd_in(x, mods, 1, norm_mix_w, jnp.swapaxes(b_w_in, 1, 2)[0], n_main, cfg)
    conv_pad = jnp.pad(state_conv[0], ((0, 0), (ls - (SSD_CONV - 1), 0), (0, 0))).reshape(n_sample, conv_dim)
    ssd_params = (b_conv_w[0], b_conv_b, b_dt_bias, b_a_log,
                  jnp.repeat(b_d[0], SSD_HEAD_DIM).reshape(SSD_GROUPS, 1, GROUP_W),
                  b_norm_w.reshape(SSD_GROUPS, 1, GROUP_W))
    state_s = state_ssm[0].reshape(bs, inner, SSD_STATE)
    yn_p, ssm_p, yn_s, ssm_s = _ssd_scan(zx, dt_raw, *ssd_params, state_s, conv_pad, bp, lp, ls)
    x = _resid_matmul(yn_p, b_w_out, 0, x, None, mods, 1, 2, cfg, "ssd_out", a_s=yn_s)
    act = _ffn_in(x, mods, 1, norm_ffn_w, f_w_in, cfg)
    x = _resid_matmul(act, f_w_out, 1, x, None, mods, 1, 5, cfg, "ffn_out1")

    y_p, y_s = _final_norm(x, final_norm_w, n_prompt)

    zx4 = zx.reshape(zx.shape[0], m_all // ls, ls, TN)

    def tails(groups):
        t = groups[SSD_GROUPS:, :, ls - (SSD_CONV - 1):, :]
        return jnp.moveaxis(t, 0, 2).reshape(t.shape[1], SSD_CONV - 1, conv_dim)
    conv_p = tails(zx4[:, lp // ls - 1:n_prompt // ls:lp // ls])
    conv_s = tails(zx4[:, n_prompt // ls:])
    return (y_p.reshape(bp, lp, d), y_s.reshape(bs, ls, d),
            v_p.reshape(1, bp, GMLP_CHUNK, width), v_s.reshape(1, bs, ls, width),
            ssm_p.reshape(1, bp, heads, SSD_HEAD_DIM, SSD_STATE), ssm_s.reshape(1, bs, heads, SSD_HEAD_DIM, SSD_STATE),
            conv_p[None], conv_s[None])
```

```python
import functools
import math
from typing import NamedTuple

import jax
import jax.numpy as jnp
from jax import lax
from jax.experimental import pallas as pl
from jax.experimental.pallas import tpu as pltpu

F32 = jnp.float32
BF16 = jnp.bfloat16

NORM_EPS = 1e-6
LN_EPS = 1e-5

GMLP_GROUPS = 16
GMLP_CHUNK = 128
SSD_HEAD_DIM = 64
SSD_STATE = 128
SSD_GROUPS = 8
SSD_CONV = 4
SSD_CHUNK = 128

SUBLANES = 8
LANES = 128
VMEM_LIMIT_BYTES = 56 * 1024 * 1024

TM = 1024
TN = 512
TN_NARROW = 256
ROW_CHUNK = 256
SUB_ROWS = 16


class Cfg(NamedTuple):
    n_ptiles: int
    tiles_per_seq: int
    seq_len_s: int
    srow0: int


def _cparams(sem):
    return pltpu.CompilerParams(dimension_semantics=sem, vmem_limit_bytes=VMEM_LIMIT_BYTES)


def _silu(x):
    return x / (1.0 + jnp.exp(-x))


def _gelu(x):
    return 0.5 * x * (1.0 + lax.erf(x * (1.0 / math.sqrt(2.0))))


def _rms_mod(x, w, scale, shift):
    y = x * lax.rsqrt(jnp.mean(x * x, axis=-1, keepdims=True) + NORM_EPS)
    return (y * w) * (1.0 + scale) + shift


def _piece_rows(c, q, piece):
    return pl.ds(pl.multiple_of(c * ROW_CHUNK + q * piece, piece), piece)


def _sample_rows(i, cfg, x_refs, mod_refs, fn, o_ref, piece=ROW_CHUNK):
    rows = o_ref.shape[0]
    nseq = ROW_CHUNK // cfg.seq_len_s
    sub_seq = piece // cfg.seq_len_s
    row0 = cfg.srow0 + (i - cfg.n_ptiles) * (rows // cfg.seq_len_s)

    def body(c, carry):
        r0 = pl.multiple_of(row0 + c * nseq, SUBLANES)
        ms_chunk = [m[pl.ds(r0, nseq), :] for m in mod_refs]
        for q in range(ROW_CHUNK // piece):
            rs = _piece_rows(c, q, piece)
            ms = [mc[q * sub_seq:(q + 1) * sub_seq][:, None, :] for mc in ms_chunk]
            x3 = [x[rs, :].reshape(sub_seq, cfg.seq_len_s, x.shape[-1]) for x in x_refs]
            o_ref[rs, :] = fn(x3, ms).reshape(piece, o_ref.shape[-1]).astype(o_ref.dtype)
        return carry
    lax.fori_loop(0, rows // ROW_CHUNK, body, 0)


def _per_seq(i, cfg, xp_refs, xs_refs, mod_refs, fn, o_ref, piece=ROW_CHUNK):
    @pl.when(i < cfg.n_ptiles)
    def _():
        s = i // cfg.tiles_per_seq
        ms = [m[pl.ds(s, 1), :] for m in mod_refs]

        def body(c, carry):
            for q in range(ROW_CHUNK // piece):
                rs = _piece_rows(c, q, piece)
                o_ref[rs, :] = fn([x[rs, :] for x in xp_refs], ms).astype(o_ref.dtype)
            return carry
        lax.fori_loop(0, o_ref.shape[0] // ROW_CHUNK, body, 0)

    @pl.when(i >= cfg.n_ptiles)
    def _():
        _sample_rows(i, cfg, xs_refs, mod_refs, fn, o_ref, piece)


def _prompt_block(cfg):
    return lambda i: jnp.minimum(i, cfg.n_ptiles - 1)


def _sample_block(cfg):
    return lambda i: jnp.maximum(i - cfg.n_ptiles, 0)


def _mod_body(c_ref, w_ref, b_ref, o_ref):
    sc = _silu(c_ref[...]).astype(BF16)
    o_ref[...] = jnp.dot(sc, w_ref[...].astype(BF16), preferred_element_type=F32) + b_ref[...]


def _mod_table(c_all, mod_w, mod_b):
    depth, d, n = mod_w.shape
    r = c_all.shape[0]
    tn = 1024
    return pl.pallas_call(
        _mod_body,
        grid=(depth, n // tn),
        in_specs=[
            pl.BlockSpec((r, d), lambda l, j: (0, 0)),
            pl.BlockSpec((None, d, tn), lambda l, j: (l, 0, j)),
            pl.BlockSpec((None, 1, tn), lambda l, j: (l, 0, j)),
        ],
        out_specs=pl.BlockSpec((None, r, tn), lambda l, j: (l, 0, j)),
        out_shape=jax.ShapeDtypeStruct((depth, r, n), F32),
        compiler_params=_cparams(("arbitrary", "arbitrary")),
        name="mod_table",
    )(c_all, mod_w, mod_b.reshape(depth, 1, n))


def _norm_prologue(i, j, cfg, xp_ref, xs_ref, nw_ref, sc_ref, sh_ref, h_ref):
    @pl.when(j == 0)
    def _():
        _per_seq(i, cfg, [xp_ref], [xs_ref], [sc_ref, sh_ref],
                 lambda xs, ms: _rms_mod(xs[0], nw_ref[...], ms[0], ms[1]), h_ref, piece=SUB_ROWS)


def _gmlp_in_body(xp_ref, xs_ref, nw_ref, sh_ref, sc_ref, wu_ref, wv_ref, bu_ref, bv_ref, u_ref, v_ref, h_ref, *, cfg):
    i, j = pl.program_id(0), pl.program_id(1)
    _norm_prologue(i, j, cfg, xp_ref, xs_ref, nw_ref, sc_ref, sh_ref, h_ref)
    h = h_ref[...]
    u_ref[...] = _gelu(jnp.dot(h, wu_ref[...].astype(BF16), preferred_element_type=F32) + bu_ref[...])
    v_ref[...] = _gelu(jnp.dot(h, wv_ref[...].astype(BF16), preferred_element_type=F32) + bv_ref[...])


def _gmlp_in(x_p, x_s, mods, layer, norm_w, w_in, b_in, j_layer, cfg):
    d = x_p.shape[1]
    m = x_p.shape[0] + x_s.shape[0]
    width = w_in.shape[-1] // 2
    tn = TN_NARROW
    nj = width // tn
    r = mods.shape[1]
    pb, sb = _prompt_block(cfg), _sample_block(cfg)
    b2 = b_in.reshape(b_in.shape[0], 1, -1)
    return pl.pallas_call(
        functools.partial(_gmlp_in_body, cfg=cfg),
        grid=(m // TM, nj),
        in_specs=[
            pl.BlockSpec((TM, d), lambda i, j: (pb(i), 0)),
            pl.BlockSpec((TM, d), lambda i, j: (sb(i), 0), pipeline_mode=pl.Buffered(1)),
            pl.BlockSpec((None, 1, d), lambda i, j: (layer, 0, 0)),
            pl.BlockSpec((None, r, d), lambda i, j: (layer, 0, 0)),
            pl.BlockSpec((None, r, d), lambda i, j: (layer, 0, 1)),
            pl.BlockSpec((None, d, tn), lambda i, j: (j_layer, 0, j)),
            pl.BlockSpec((None, d, tn), lambda i, j: (j_layer, 0, j + nj)),
            pl.BlockSpec((None, 1, tn), lambda i, j: (j_layer, 0, j)),
            pl.BlockSpec((None, 1, tn), lambda i, j: (j_layer, 0, j + nj)),
        ],
        out_specs=[pl.BlockSpec((TM, tn), lambda i, j: (i, j)), pl.BlockSpec((TM, tn), lambda i, j: (i, j))],
        out_shape=[jax.ShapeDtypeStruct((m, width), F32), jax.ShapeDtypeStruct((m, width), F32)],
        scratch_shapes=[pltpu.VMEM((TM, d), BF16)],
        compiler_params=_cparams(("arbitrary", "arbitrary")),
        name="gmlp_in",
    )(x_p, x_s, norm_w.reshape(-1, 1, d), mods, mods, w_in, w_in, b2, b2)


def _ffn_in_body(x_ref, nw_ref, sh_ref, sc_ref, wg_ref, wu_ref, a_ref, h_ref, *, cfg):
    i, j = pl.program_id(0), pl.program_id(1)
    _norm_prologue(i, j, cfg, x_ref, x_ref, nw_ref, sc_ref, sh_ref, h_ref)
    h = h_ref[...]
    gate = jnp.dot(h, wg_ref[...].astype(BF16), preferred_element_type=F32)
    up = jnp.dot(h, wu_ref[...].astype(BF16), preferred_element_type=F32)
    a_ref[...] = (_silu(gate) * up).astype(BF16)


def _ffn_in(x, mods, layer, norm_w, w_in, cfg):
    m, d = x.shape
    hidden = w_in.shape[-1] // 2
    nj = hidden // TN
    r = mods.shape[1]
    return pl.pallas_call(
        functools.partial(_ffn_in_body, cfg=cfg),
        grid=(m // TM, nj),
        in_specs=[
            pl.BlockSpec((TM, d), lambda i, j: (i, 0)),
            pl.BlockSpec((None, 1, d), lambda i, j: (layer, 0, 0)),
            pl.BlockSpec((None, r, d), lambda i, j: (layer, 0, 3)),
            pl.BlockSpec((None, r, d), lambda i, j: (layer, 0, 4)),
            pl.BlockSpec((None, d, TN), lambda i, j: (layer, 0, j)),
            pl.BlockSpec((None, d, TN), lambda i, j: (layer, 0, j + nj)),
        ],
        out_specs=pl.BlockSpec((TM, TN), lambda i, j: (i, j)),
        out_shape=jax.ShapeDtypeStruct((m, hidden), BF16),
        scratch_shapes=[pltpu.VMEM((TM, d), BF16)],
        compiler_params=_cparams(("arbitrary", "arbitrary")),
        name="ffn_in",
    )(x, norm_w.reshape(-1, 1, d), mods, mods, w_in, w_in)


_NT = (((1,), (1,)), ((), ()))


def _ssd_in_body(x_ref, nw_ref, sh_ref, sc_ref, w_ref, wdt_ref, o_ref, dt_ref, h_ref, *, cfg):
    i, j = pl.program_id(0), pl.program_id(1)
    _norm_prologue(i, j, cfg, x_ref, x_ref, nw_ref, sc_ref, sh_ref, h_ref)
    h = h_ref[...]
    o_ref[...] = lax.dot_general(h, w_ref[...].astype(BF16), _NT, preferred_element_type=F32)

    @pl.when(j == 0)
    def _():
        dt_ref[...] = lax.dot_general(h, wdt_ref[...].astype(BF16), _NT, preferred_element_type=F32)


def _ssd_in(x, mods, layer, norm_w, w_in_t, n_main, cfg):
    m, d = x.shape
    r = mods.shape[1]
    n_dt = w_in_t.shape[0] - n_main
    nj = n_main // TN
    return pl.pallas_call(
        functools.partial(_ssd_in_body, cfg=cfg),
        grid=(m // TM, nj),
        in_specs=[
            pl.BlockSpec((TM, d), lambda i, j: (i, 0)),
            pl.BlockSpec((None, 1, d), lambda i, j: (layer, 0, 0)),
            pl.BlockSpec((None, r, d), lambda i, j: (layer, 0, 0)),
            pl.BlockSpec((None, r, d), lambda i, j: (layer, 0, 1)),
            pl.BlockSpec((TN, d), lambda i, j: (j, 0)),
            pl.BlockSpec((n_dt, d), lambda i, j: (n_main // n_dt, 0)),
        ],
        out_specs=[pl.BlockSpec((None, TM, TN), lambda i, j: (j, i, 0)), pl.BlockSpec((TM, n_dt), lambda i, j: (i, 0))],
        out_shape=[jax.ShapeDtypeStruct((nj, m, TN), F32), jax.ShapeDtypeStruct((m, n_dt), F32)],
        scratch_shapes=[pltpu.VMEM((TM, d), BF16)],
        compiler_params=_cparams(("arbitrary", "arbitrary")),
        name="ssd_in",
    )(x, norm_w.reshape(-1, 1, d), mods, mods, w_in_t, w_in_t)


TK = 512


def _resid_body(a_ref, *rest, cfg, nk, two_a):
    as_ref = None
    if two_a:
        as_ref, *rest = rest
    w_ref, *rest = rest
    if len(rest) == 4:
        xp_ref, xs_ref, g_ref, o_ref = rest
    else:
        xp_ref, g_ref, o_ref = rest
        xs_ref = xp_ref
    i, k = pl.program_id(0), pl.program_id(1)
    d = o_ref.shape[1]

    def accumulate(first):
        a = a_ref[...] if as_ref is None else jnp.where(i < cfg.n_ptiles, a_ref[...], as_ref[...])
        for c in range(d // TN):
            cols = slice(c * TN, (c + 1) * TN)
            part = jnp.dot(a, w_ref[:, cols].astype(BF16), preferred_element_type=F32)
            if first:
                o_ref[:, cols] = part
            else:
                o_ref[:, cols] += part

    @pl.when(k == 0)
    def _():
        accumulate(True)

    @pl.when(k > 0)
    def _():
        accumulate(False)

    @pl.when(k == nk - 1)
    def _():
        _per_seq(i, cfg, [xp_ref, o_ref], [xs_ref, o_ref], [g_ref], lambda xs, ms: xs[0] + ms[0] * xs[1], o_ref)


def _resid_matmul(a, w, w_layer, x_p, x_s, mods, layer, gate_chunk, cfg, name, a_s=None):
    m = a.shape[-2] + (0 if a_s is None else a_s.shape[-2])
    d = x_p.shape[1]
    r = mods.shape[1]
    pb, sb = _prompt_block(cfg), _sample_block(cfg)
    if a.ndim == 2:
        nk = a.shape[1] // TK
        a_specs, a_args = [pl.BlockSpec((TM, TK), lambda i, k: (i, k))], [a]
    else:
        nk = a.shape[0]
        assert a.shape[2] == TK
        if a_s is None:
            a_specs, a_args = [pl.BlockSpec((None, TM, TK), lambda i, k: (k, i, 0))], [a]
        else:
            a_specs = [pl.BlockSpec((None, TM, TK), lambda i, k: (k, pb(i), 0)),
                       pl.BlockSpec((None, TM, TK), lambda i, k: (k, sb(i), 0))]
            a_args = [a, a_s]
    if x_s is None:
        x_specs, xs = [pl.BlockSpec((TM, d), lambda i, k: (i, 0))], [x_p]
    else:
        x_specs = [pl.BlockSpec((TM, d), lambda i, k: (pb(i), 0)),
                   pl.BlockSpec((TM, d), lambda i, k: (sb(i), 0), pipeline_mode=pl.Buffered(1))]
        xs = [x_p, x_s]
    return pl.pallas_call(
        functools.partial(_resid_body, cfg=cfg, nk=nk, two_a=a_s is not None),
        grid=(m // TM, nk),
        in_specs=a_specs + [pl.BlockSpec((None, TK, d), lambda i, k: (w_layer, k, 0))] + x_specs
        + [pl.BlockSpec((None, r, d), lambda i, k: (layer, 0, gate_chunk))],
        out_specs=pl.BlockSpec((TM, d), lambda i, k: (i, 0)),
        out_shape=jax.ShapeDtypeStruct((m, d), F32),
        compiler_params=_cparams(("arbitrary", "arbitrary")),
        name=name,
    )(*a_args, w, *xs, mods)


MIX_ROWS = 2 * GMLP_CHUNK


def _gmlp_mix_body(u_ref, v_ref, lnw_ref, lnb_ref, ws_ref, mask_ref, bias_ref, g_ref, vp_ref, vs_ref,
                   vn_ref, *, n_prompt_steps, steps_per_seq):
    t = pl.program_id(0)
    v = v_ref[...]
    xc = v - jnp.mean(v, axis=-1, keepdims=True)
    vn = xc * lax.rsqrt(jnp.mean(xc * xc, axis=-1, keepdims=True) + LN_EPS) * lnw_ref[...] + lnb_ref[...]
    vn_ref[...] = vn

    @pl.when(jnp.logical_and(t < n_prompt_steps, t % steps_per_seq == steps_per_seq - 1))
    def _():
        vp_ref[...] = vn[MIX_ROWS - GMLP_CHUNK:, :]

    @pl.when(t >= n_prompt_steps)
    def _():
        vs_ref[...] = vn

    mask = mask_ref[...]
    for g in range(GMLP_GROUPS):
        wb = (ws_ref[g] * mask).astype(BF16)
        cols = slice(g * GMLP_CHUNK, (g + 1) * GMLP_CHUNK)
        for c in range(MIX_ROWS // GMLP_CHUNK):
            rows = slice(c * GMLP_CHUNK, (c + 1) * GMLP_CHUNK)
            s = jnp.dot(wb, vn_ref[rows, cols].astype(BF16), preferred_element_type=F32) + bias_ref[:, cols]
            g_ref[rows, cols] = (u_ref[rows, cols] * s).astype(BF16)


def _gmlp_mix(u, v, ln_w, ln_b, wmix, mask, bias, n_prompt_rows, seq_len, n_prompt_seq):
    m, width = u.shape
    n_prompt_steps = n_prompt_rows // MIX_ROWS
    steps_per_seq = seq_len // MIX_ROWS
    n_sample_rows = m - n_prompt_rows

    def variant(t):
        return jnp.where(t < n_prompt_steps, 0, 1)

    return pl.pallas_call(
        functools.partial(_gmlp_mix_body, n_prompt_steps=n_prompt_steps, steps_per_seq=steps_per_seq),
        grid=(m // MIX_ROWS,),
        in_specs=[
            pl.BlockSpec((MIX_ROWS, width), lambda t: (t, 0)),
            pl.BlockSpec((MIX_ROWS, width), lambda t: (t, 0)),
            pl.BlockSpec((1, width), lambda t: (0, 0)),
            pl.BlockSpec((1, width), lambda t: (0, 0)),
            pl.BlockSpec((None, GMLP_GROUPS, GMLP_CHUNK, GMLP_CHUNK), lambda t: (variant(t), 0, 0, 0)),
            pl.BlockSpec((None, GMLP_CHUNK, GMLP_CHUNK), lambda t: (variant(t), 0, 0)),
            pl.BlockSpec((None, GMLP_CHUNK, width), lambda t: (variant(t), 0, 0)),
        ],
        out_specs=[
            pl.BlockSpec((MIX_ROWS, width), lambda t: (t, 0)),
            pl.BlockSpec((GMLP_CHUNK, width), lambda t: (jnp.minimum(t // steps_per_seq, n_prompt_seq - 1), 0)),
            pl.BlockSpec((MIX_ROWS, width), lambda t: (jnp.maximum(t - n_prompt_steps, 0), 0)),
        ],
        out_shape=[
            jax.ShapeDtypeStruct((m, width), BF16),
            jax.ShapeDtypeStruct((n_prompt_seq * GMLP_CHUNK, width), F32),
            jax.ShapeDtypeStruct((n_sample_rows, width), F32),
        ],
        scratch_shapes=[pltpu.VMEM((MIX_ROWS, width), F32)],
        compiler_params=_cparams(("arbitrary",)),
        name="gmlp_mix",
    )(u, v, ln_w.reshape(1, width), ln_b.reshape(1, width), wmix, mask, bias)


GROUP_HEADS = 8
GROUP_W = GROUP_HEADS * SSD_HEAD_DIM
T = SSD_CHUNK
BC_PER_BLOCK = GROUP_W // SSD_STATE


def _split3(x):
    hi = x.astype(BF16)
    r = x - hi.astype(F32)
    mid = r.astype(BF16)
    lo = (r - mid.astype(F32)).astype(BF16)
    return hi, mid, lo


def _dot_exact_rhs(m_b, pieces):
    return sum(jnp.dot(m_b, p, preferred_element_type=F32) for p in pieces)


def _dot_exact_lhs(pieces, e_b):
    return sum(jnp.dot(p, e_b, preferred_element_type=F32) for p in pieces)


def _ssd_group(xs, bm, cm, z, dt_x, cs_x, cs_end_x, cs_row, d_a, states, n_seg, dsk, nw, keep):
    seg = T // n_seg
    xdt = xs * dt_x
    ecs_x = jnp.exp(cs_x)
    dte_x = jnp.exp(cs_end_x - cs_x)
    xdt_b = xdt.astype(BF16)
    xd_t = (xdt * dte_x).T.astype(BF16)
    bmb = bm.astype(BF16)
    cmb = cm.astype(BF16)
    cb = lax.dot_general(cmb, bmb, _NT, preferred_element_type=F32)

    row = lax.broadcasted_iota(jnp.int32, (T, SSD_STATE), 0)
    y_offs, new_states = [], []
    for s in range(n_seg):
        st = states[s]
        c_seg = cmb if n_seg == 1 else cm[s * seg:(s + 1) * seg].astype(BF16)
        y_offs.append(lax.dot_general(c_seg, st.astype(BF16), _NT, preferred_element_type=F32))
        if n_seg == 1:
            b_seg = bmb
        else:
            b_seg = jnp.where(jnp.logical_and(row >= s * seg, row < (s + 1) * seg), bm, 0.0).astype(BF16)
        upd = jnp.dot(xd_t, b_seg, preferred_element_type=F32)
        decayed = jnp.concatenate(
            [st[h * SSD_HEAD_DIM:(h + 1) * SSD_HEAD_DIM, :] * d_a(s, h) for h in range(GROUP_HEADS)], axis=0)
        new_states.append(decayed + upd)
    y_off = jnp.concatenate(y_offs, axis=0) if n_seg > 1 else y_offs[0]

    lane = lax.broadcasted_iota(jnp.int32, (T, LANES), 1)
    lo_half = lane < SSD_HEAD_DIM
    ys = []
    for q in range(GROUP_HEADS // 2):
        v = cs_x[:, q * LANES:(q + 1) * LANES]
        r = pltpu.roll(v, SSD_HEAD_DIM, 1)
        cols = (jnp.where(lo_half, v, r), jnp.where(lo_half, r, v))
        ws = []
        for e in range(2):
            diff = cols[e] - cs_row(2 * q + e)
            ws.append((cb * jnp.exp(jnp.where(keep, diff, -jnp.inf))).astype(BF16))
        xp = xdt_b[:, q * LANES:(q + 1) * LANES]
        zero = jnp.zeros_like(xp)
        xpair = jnp.concatenate([jnp.where(lo_half, xp, zero), jnp.where(lo_half, zero, xp)], axis=0)
        ys.append(jnp.dot(jnp.concatenate(ws, axis=1), xpair, preferred_element_type=F32))
    y_diag = jnp.concatenate(ys, axis=1)

    y = y_diag + y_off * ecs_x + dsk * xs
    y = y * _silu(z)
    yn = y * lax.rsqrt(jnp.mean(y * y, axis=-1, keepdims=True) + NORM_EPS) * nw
    return yn.astype(BF16), new_states


def _causal_keep(seq_len):
    r = lax.broadcasted_iota(jnp.int32, (T, T), 0)
    c = lax.broadcasted_iota(jnp.int32, (T, T), 1)
    keep = c <= r
    if seq_len < T:
        keep = jnp.logical_and(keep, (r // seq_len) == (c // seq_len))
    return keep


def _ssd_prompt_body(z_ref, xs_ref, bc_ref, dt_ref, cw_ref, cbias_ref, dtb_ref, alog_ref, dsk_ref, nw_ref,
                     m_ref, e_ref, yn_ref, st_ref,
                     cbuf_ref, xc_ref, bcs_ref, dt3_ref, cs3_ref, cst_ref, *, c):
    n_x = xs_ref.shape[0]
    pad = SUBLANES
    taps = SSD_CONV - 1

    @pl.when(c == 0)
    def _():
        st_ref[...] = jnp.zeros_like(st_ref)
        cbuf_ref[:, 0:pad, :] = jnp.zeros((cbuf_ref.shape[0], pad, GROUP_W), F32)

    def conv(k, src):
        w = cw_ref[k]
        cbuf_ref[k, pad:pad + T, :] = src
        acc = cbias_ref[k] + src * w[taps:taps + 1, :]
        for kk in range(taps):
            acc = acc + cbuf_ref[k, pad - taps + kk:pad - taps + kk + T, :] * w[kk:kk + 1, :]
        cbuf_ref[k, pad - taps:pad, :] = cbuf_ref[k, pad + T - taps:pad + T, :]
        return _silu(acc)

    def conv_x(k, carry):
        xc_ref[k] = conv(k, xs_ref[k])
        return carry
    lax.fori_loop(0, n_x, conv_x, 0)

    def conv_bc(k, carry):
        out = conv(n_x + k, bc_ref[k])
        for gg in range(BC_PER_BLOCK):
            bcs_ref[k * BC_PER_BLOCK + gg] = out[:, gg * SSD_STATE:(gg + 1) * SSD_STATE]
        return carry
    lax.fori_loop(0, bc_ref.shape[0], conv_bc, 0)

    dt = jax.nn.softplus(dt_ref[...] + dtb_ref[...])
    a = dt * (-jnp.exp(alog_ref[...]))
    cs = _dot_exact_rhs(m_ref[...], _split3(a))
    cst_ref[...] = cs.T
    for p, (dt_p, cs_p) in enumerate(zip(_split3(dt), _split3(cs))):
        dt3_ref[p] = dt_p
        cs3_ref[p] = cs_p

    keep = _causal_keep(T)

    def spread(g):
        e_g = e_ref[g]
        return (_dot_exact_lhs([dt3_ref[p] for p in range(3)], e_g), _dot_exact_lhs([cs3_ref[p] for p in range(3)], e_g))

    def group(g, dt_x, cs_x):
        h0 = g * GROUP_HEADS
        rows = pl.ds(g * GROUP_W, GROUP_W)
        yn, new_states = _ssd_group(
            xc_ref[g], bcs_ref[g], bcs_ref[SSD_GROUPS + g], z_ref[g], dt_x, cs_x, cs_x[T - 1:T, :],
            lambda h: cst_ref[pl.ds(h0 + h, 1), :],
            lambda s, h: jnp.exp(cst_ref[pl.ds(h0 + h, 1), T - 1:T]),
            [st_ref[rows, :]], 1, dsk_ref[g], nw_ref[g], keep)
        yn_ref[g] = yn
        st_ref[rows, :] = new_states[0]

    spread_next = spread(0)
    for g in range(SSD_GROUPS):
        dt_x, cs_x = spread_next
        if g + 1 < SSD_GROUPS:
            spread_next = spread(g + 1)
        group(g, dt_x, cs_x)


def _ssd_sample_body(z_ref, xs_ref, b_ref, c_ref, px_ref, pb_ref, pc_ref, dt_ref, wx_ref, wb_ref, wc_ref,
                     bx_ref, bb_ref, bc_ref, dtb_ref, alog_ref, dsk_ref, nw_ref, m_ref, mseg_ref, e_ref,
                     st_in_ref, yn_ref, st_ref, cst_ref, cet_ref, *, seq_len, g):
    n_seg = T // seq_len

    def conv(x_ref, p_ref, w_ref, bias_ref):
        x = x_ref[...]
        width = x.shape[-1]
        taps = SSD_CONV - 1
        prev = [jnp.broadcast_to(p_ref[k][:, None, :], (n_seg, seq_len, width)).reshape(T, width) for k in range(taps)]
        tpos = lax.broadcasted_iota(jnp.int32, x.shape, 0) % seq_len
        acc = bias_ref[...] + x * w_ref[taps:SSD_CONV, :]
        for sh in range(1, SSD_CONV):
            hist = prev[taps - 1]
            for t in range(sh - 2, -1, -1):
                hist = jnp.where(tpos == t, prev[taps - sh + t], hist)
            shifted = jnp.where(tpos >= sh, pltpu.roll(x, sh, 0), hist)
            acc = acc + shifted * w_ref[taps - sh:SSD_CONV - sh, :]
        return _silu(acc)

    xs = conv(xs_ref, px_ref, wx_ref, bx_ref)
    bm = conv(b_ref, pb_ref, wb_ref, bb_ref)
    cm = conv(c_ref, pc_ref, wc_ref, bc_ref)

    dt = jax.nn.softplus(dt_ref[...] + dtb_ref[...])
    a3 = _split3(dt * (-jnp.exp(alog_ref[...])))
    cs = _dot_exact_rhs(m_ref[...], a3)
    cs_end = _dot_exact_rhs(mseg_ref[...], a3)
    cst_ref[...] = cs.T
    cet_ref[...] = cs_end.T
    e_g = e_ref[...]
    h0 = g * GROUP_HEADS
    yn, new_states = _ssd_group(
        xs, bm, cm, z_ref[...], _dot_exact_lhs(_split3(dt), e_g), _dot_exact_lhs(_split3(cs), e_g),
        _dot_exact_lhs(_split3(cs_end), e_g),
        lambda h: cst_ref[pl.ds(h0 + h, 1), :],
        lambda s, h: jnp.exp(cet_ref[pl.ds(h0 + h, 1), s * seq_len:s * seq_len + 1]),
        [st_in_ref[s] for s in range(n_seg)], n_seg, dsk_ref[...], nw_ref[...], _causal_keep(seq_len))
    yn_ref[...] = yn
    for s in range(n_seg):
        st_ref[s] = new_states[s]


def _ssd_masks(seq_len):
    r = jnp.arange(T)
    same = (r[:, None] // seq_len) == (r[None, :] // seq_len)
    return jnp.logical_and(same, r[None, :] <= r[:, None]).astype(BF16), same.astype(BF16)


N_PROMPT_IN, N_SAMPLE_IN = 12, 22


def _ssd_scan_body(*refs, n_chunks, seq_len_s):
    s = pl.program_id(0)
    p_in = refs[:N_PROMPT_IN]
    s_in = refs[N_PROMPT_IN:N_PROMPT_IN + N_SAMPLE_IN]
    yn_p, st_p, yn_s, st_s = refs[N_PROMPT_IN + N_SAMPLE_IN:N_PROMPT_IN + N_SAMPLE_IN + 4]
    scratch = refs[N_PROMPT_IN + N_SAMPLE_IN + 4:]
    _ssd_prompt_body(*p_in, yn_p, st_p, *scratch[:6], c=s % n_chunks)
    _ssd_sample_body(*s_in, yn_s, st_s, *scratch[6:], seq_len=seq_len_s, g=s % SSD_GROUPS)


def _ssd_scan(zx, dt_raw, conv_w, conv_b, dtb, alog, dsk, nw, state_s, conv_hist_s, n_pseq, seq_len_p, seq_len_s):
    heads = dt_raw.shape[1]
    inner = SSD_GROUPS * GROUP_W
    n_chunks = seq_len_p // T
    m_p = n_pseq * seq_len_p
    n_seg = T // seq_len_s
    n_sseq = state_s.shape[0]
    row0 = m_p // T
    n_steps = n_pseq * n_chunks
    assert n_steps == (n_sseq // n_seg) * SSD_GROUPS
    n_bc = 2 * SSD_GROUPS // BC_PER_BLOCK
    n_cblk = SSD_GROUPS + n_bc
    cw = conv_w.reshape(SSD_CONV, n_cblk, GROUP_W).transpose(1, 0, 2)
    cbias = conv_b.reshape(n_cblk, 1, GROUP_W)
    e_all = (jnp.arange(inner)[None, :] // SSD_HEAD_DIM == jnp.arange(heads)[:, None]).astype(BF16)
    e_grp = e_all.reshape(heads, SSD_GROUPS, GROUP_W).transpose(1, 0, 2)
    m_tril, _ = _ssd_masks(T)
    m_s, mseg_s = _ssd_masks(seq_len_s)
    xoff = SSD_GROUPS
    boff = 2 * SSD_GROUPS
    coff = boff + SSD_GROUPS // BC_PER_BLOCK
    cb0 = inner // SSD_STATE
    cc0 = cb0 + SSD_GROUPS

    full = lambda shape: pl.BlockSpec(shape, lambda s: (0,) * len(shape))
    bp = lambda s: s // n_chunks
    ts = lambda s: s // SSD_GROUPS
    gs = lambda s: s % SSD_GROUPS
    prompt_in = [
        pl.BlockSpec((SSD_GROUPS, T, GROUP_W), lambda s: (0, s, 0)),
        pl.BlockSpec((SSD_GROUPS, T, GROUP_W), lambda s: (1, s, 0)),
        pl.BlockSpec((n_bc, T, GROUP_W), lambda s: (2 * SSD_GROUPS // n_bc, s, 0)),
        pl.BlockSpec((T, heads), lambda s: (s, 0)),
        full((n_cblk, SSD_CONV, GROUP_W)), full((n_cblk, 1, GROUP_W)), full((1, heads)), full((1, heads)),
        full((SSD_GROUPS, 1, GROUP_W)), full((SSD_GROUPS, 1, GROUP_W)), full((T, T)),
        full((SSD_GROUPS, heads, GROUP_W)),
    ]
    per_g = lambda shape: pl.BlockSpec(shape, lambda s: (gs(s),) + (0,) * (len(shape) - 1))
    sample_in = [
        pl.BlockSpec((None, T, GROUP_W), lambda s: (gs(s), row0 + ts(s), 0)),
        pl.BlockSpec((None, T, GROUP_W), lambda s: (xoff + gs(s), row0 + ts(s), 0)),
        pl.BlockSpec((None, T, SSD_STATE),
                     lambda s: (boff + gs(s) // BC_PER_BLOCK, row0 + ts(s), gs(s) % BC_PER_BLOCK)),
        pl.BlockSpec((None, T, SSD_STATE),
                     lambda s: (coff + gs(s) // BC_PER_BLOCK, row0 + ts(s), gs(s) % BC_PER_BLOCK)),
        pl.BlockSpec((SSD_CONV - 1, n_seg, GROUP_W), lambda s: (0, ts(s), gs(s))),
        pl.BlockSpec((SSD_CONV - 1, n_seg, SSD_STATE), lambda s: (0, ts(s), cb0 + gs(s))),
        pl.BlockSpec((SSD_CONV - 1, n_seg, SSD_STATE), lambda s: (0, ts(s), cc0 + gs(s))),
        pl.BlockSpec((T, heads), lambda s: (row0 + ts(s), 0)),
        pl.BlockSpec((SSD_CONV, GROUP_W), lambda s: (0, gs(s))),
        pl.BlockSpec((SSD_CONV, SSD_STATE), lambda s: (0, cb0 + gs(s))),
        pl.BlockSpec((SSD_CONV, SSD_STATE), lambda s: (0, cc0 + gs(s))),
        pl.BlockSpec((1, GROUP_W), lambda s: (0, gs(s))),
        pl.BlockSpec((1, SSD_STATE), lambda s: (0, cb0 + gs(s))),
        pl.BlockSpec((1, SSD_STATE), lambda s: (0, cc0 + gs(s))),
        full((1, heads)), full((1, heads)),
        per_g((None, 1, GROUP_W)), per_g((None, 1, GROUP_W)),
        full((T, T)), full((T, T)),
        pl.BlockSpec((heads, GROUP_W), lambda s: (0, gs(s))),
        pl.BlockSpec((n_seg, GROUP_W, SSD_STATE), lambda s: (ts(s), gs(s), 0)),
    ]
    assert len(prompt_in) == N_PROMPT_IN and len(sample_in) == N_SAMPLE_IN
    return pl.pallas_call(
        functools.partial(_ssd_scan_body, n_chunks=n_chunks, seq_len_s=seq_len_s),
        grid=(n_steps,),
        in_specs=prompt_in + sample_in,
        out_specs=[pl.BlockSpec((SSD_GROUPS, T, GROUP_W), lambda s: (0, s, 0)),
                   pl.BlockSpec((None, inner, SSD_STATE), lambda s: (bp(s), 0, 0)),
                   pl.BlockSpec((None, T, GROUP_W), lambda s: (gs(s), ts(s), 0)),
                   pl.BlockSpec((n_seg, GROUP_W, SSD_STATE), lambda s: (ts(s), gs(s), 0))],
        out_shape=[jax.ShapeDtypeStruct((SSD_GROUPS, m_p, GROUP_W), BF16),
                   jax.ShapeDtypeStruct((n_pseq, inner, SSD_STATE), F32),
                   jax.ShapeDtypeStruct((SSD_GROUPS, n_sseq * seq_len_s, GROUP_W), BF16),
                   jax.ShapeDtypeStruct(state_s.shape, F32)],
        scratch_shapes=[pltpu.VMEM((n_cblk, SUBLANES + T, GROUP_W), F32),
                        pltpu.VMEM((SSD_GROUPS, T, GROUP_W), F32),
                        pltpu.VMEM((2 * SSD_GROUPS, T, SSD_STATE), F32),
                        pltpu.VMEM((3, T, heads), BF16),
                        pltpu.VMEM((3, T, heads), BF16),
                        pltpu.VMEM((heads, T), F32),
                        pltpu.VMEM((heads, T), F32), pltpu.VMEM((heads, T), F32)],
        compiler_params=_cparams(("arbitrary",)),
        name="ssd_scan",
    )(zx, zx, zx, dt_raw, cw, cbias, dtb, alog, dsk, nw, m_tril, e_grp,
      zx, zx, zx, zx, conv_hist_s, conv_hist_s, conv_hist_s, dt_raw, conv_w, conv_w, conv_w, conv_b, conv_b, conv_b,
      dtb, alog, dsk, nw, m_s, mseg_s, e_all, state_s)


def _final_body(x_ref, w_ref, yp_ref, ys_ref, *, n_ptiles):
    i = pl.program_id(0)
    x = x_ref[...]
    y = x * lax.rsqrt(jnp.mean(x * x, axis=-1, keepdims=True) + NORM_EPS) * w_ref[...]

    @pl.when(i < n_ptiles)
    def _():
        yp_ref[...] = y

    @pl.when(i >= n_ptiles)
    def _():
        ys_ref[...] = y


def _final_norm(x, w, n_prompt_rows):
    m, d = x.shape
    tm = 512
    n_ptiles = n_prompt_rows // tm
    return pl.pallas_call(
        functools.partial(_final_body, n_ptiles=n_ptiles),
        grid=(m // tm,),
        in_specs=[pl.BlockSpec((tm, d), lambda i: (i, 0)), pl.BlockSpec((1, d), lambda i: (0, 0))],
        out_specs=[pl.BlockSpec((tm, d), lambda i: (jnp.minimum(i, n_ptiles - 1), 0)),
                   pl.BlockSpec((tm, d), lambda i: (jnp.maximum(i - n_ptiles, 0), 0))],
        out_shape=[jax.ShapeDtypeStruct((n_prompt_rows, d), F32), jax.ShapeDtypeStruct((m - n_prompt_rows, d), F32)],
        compiler_params=_cparams(("arbitrary",)),
        name="final_norm",
    )(x, w.reshape(1, d))


def kernel(x_prompt, x_sample, c_prompt, c_sample, state_ssm, state_conv, mod_w, mod_b, norm_mix_w, norm_ffn_w,
           a_w_in, a_b_in, a_ln_w, a_ln_b, a_w_s, a_b_s, a_w_out,
           b_w_in, b_conv_w, b_conv_b, b_dt_bias, b_a_log, b_d, b_norm_w, b_w_out,
           f_w_in, f_w_out, final_norm_w):
    bp, lp, d = x_prompt.shape
    bs, ls, _ = x_sample.shape
    depth = mod_w.shape[0]
    n_prompt = bp * lp
    n_sample = bs * ls
    m_all = n_prompt + n_sample
    assert lp % TM == 0 and n_sample % TM == 0 and TM % ls == 0 and bp <= SUBLANES
    assert ls >= SSD_CONV - 1 and T % ls == 0 and lp % T == 0 and GMLP_CHUNK % ls == 0 and lp % MIX_ROWS == 0
    assert depth == 2 and a_w_in.shape[0] == 1 and b_w_in.shape[0] == 1
    cfg = Cfg(n_ptiles=n_prompt // TM, tiles_per_seq=lp // TM, seq_len_s=ls, srow0=SUBLANES)

    x_p = x_prompt.reshape(n_prompt, d)
    x_s = x_sample.reshape(n_sample, d)
    c_all = jnp.concatenate([c_prompt, jnp.zeros((SUBLANES - bp, d), F32), c_sample], axis=0)
    mods = _mod_table(c_all, mod_w, mod_b)

    u, v = _gmlp_in(x_p, x_s, mods, 0, norm_mix_w, a_w_in, a_b_in, 0, cfg)
    r = jnp.arange(GMLP_CHUNK)
    tril = r[None, :] <= r[:, None]
    mask = jnp.stack([tril, jnp.logical_and(tril, (r[:, None] // ls) == (r[None, :] // ls))]).astype(F32)
    rep = GMLP_CHUNK // ls
    wmix = jnp.stack([a_w_s[0], jnp.tile(a_w_s[0, :, :ls, :ls], (1, rep, rep))])
    width = a_w_in.shape[-1] // 2
    bias = jnp.stack([jnp.repeat(a_b_s[0].T, width // GMLP_GROUPS, axis=1),
                      jnp.repeat(jnp.tile(a_b_s[0, :, :ls].T, (rep, 1)), width // GMLP_GROUPS, axis=1)])
    gated, v_p, v_s = _gmlp_mix(u, v, a_ln_w[0], a_ln_b[0], wmix, mask, bias, n_prompt, lp, bp)
    x = _resid_matmul(gated, a_w_out, 0, x_p, x_s, mods, 0, 2, cfg, "gmlp_out")
    act = _ffn_in(x, mods, 0, norm_ffn_w, f_w_in, cfg)
    x = _resid_matmul(act, f_w_out, 0, x, None, mods, 0, 5, cfg, "ffn_out0")

    inner = b_w_out.shape[1]
    conv_dim = b_conv_w.shape[-1]
    n_main = inner + conv_dim
    heads = inner // SSD_HEAD_DIM
    zx, dt_raw = _ssd_in(x, mods, 1, norm_mix_w, jnp.swapaxes(b_w_in, 1, 2)[0], n_main, cfg)
    conv_hist = jnp.swapaxes(state_conv[0], 0, 1)
    ssd_params = (b_conv_w[0], b_conv_b, b_dt_bias, b_a_log,
                  jnp.repeat(b_d[0], SSD_HEAD_DIM).reshape(SSD_GROUPS, 1, GROUP_W),
                  b_norm_w.reshape(SSD_GROUPS, 1, GROUP_W))
    state_s = state_ssm[0].reshape(bs, inner, SSD_STATE)
    yn_p, ssm_p, yn_s, ssm_s = _ssd_scan(zx, dt_raw, *ssd_params, state_s, conv_hist, bp, lp, ls)
    x = _resid_matmul(yn_p, b_w_out, 0, x, None, mods, 1, 2, cfg, "ssd_out", a_s=yn_s)
    act = _ffn_in(x, mods, 1, norm_ffn_w, f_w_in, cfg)
    x = _resid_matmul(act, f_w_out, 1, x, None, mods, 1, 5, cfg, "ffn_out1")

    y_p, y_s = _final_norm(x, final_norm_w, n_prompt)

    zx4 = zx.reshape(zx.shape[0], m_all // ls, ls, TN)

    def tails(groups):
        t = groups[SSD_GROUPS:, :, ls - (SSD_CONV - 1):, :]
        return jnp.moveaxis(t, 0, 2).reshape(t.shape[1], SSD_CONV - 1, conv_dim)
    conv_p = tails(zx4[:, lp // ls - 1:n_prompt // ls:lp // ls])
    conv_s = tails(zx4[:, n_prompt // ls:])
    return (y_p.reshape(bp, lp, d), y_s.reshape(bs, ls, d),
            v_p.reshape(1, bp, GMLP_CHUNK, width), v_s.reshape(1, bs, ls, width),
            ssm_p.reshape(1, bp, heads, SSD_HEAD_DIM, SSD_STATE), ssm_s.reshape(1, bs, heads, SSD_HEAD_DIM, SSD_STATE),
            conv_p[None], conv_s[None])
```

```python
import functools
import math
from typing import NamedTuple

import jax
import jax.numpy as jnp
from jax import lax
from jax.experimental import pallas as pl
from jax.experimental.pallas import tpu as pltpu

F32 = jnp.float32
BF16 = jnp.bfloat16

NORM_EPS = 1e-6
LN_EPS = 1e-5

GMLP_GROUPS = 16
GMLP_CHUNK = 128
SSD_HEAD_DIM = 64
SSD_STATE = 128
SSD_GROUPS = 8
SSD_CONV = 4
SSD_CHUNK = 128

SUBLANES = 8
LANES = 128
VMEM_LIMIT_BYTES = 56 * 1024 * 1024

TM = 1024
TN = 512
TN_NARROW = 256
ROW_CHUNK = 256
SUB_ROWS = 16


class Cfg(NamedTuple):
    n_ptiles: int
    tiles_per_seq: int
    seq_len_s: int
    srow0: int


def _cparams(sem):
    return pltpu.CompilerParams(dimension_semantics=sem, vmem_limit_bytes=VMEM_LIMIT_BYTES)


def _silu(x):
    return x / (1.0 + jnp.exp(-x))


def _gelu(x):
    return 0.5 * x * (1.0 + lax.erf(x * (1.0 / math.sqrt(2.0))))


def _rms_mod(x, w, scale, shift):
    y = x * lax.rsqrt(jnp.mean(x * x, axis=-1, keepdims=True) + NORM_EPS)
    return (y * w) * (1.0 + scale) + shift


def _piece_rows(c, q, piece):
    return pl.ds(pl.multiple_of(c * ROW_CHUNK + q * piece, piece), piece)


def _sample_rows(i, cfg, x_refs, mod_refs, fn, o_ref, piece=ROW_CHUNK):
    rows = o_ref.shape[0]
    nseq = ROW_CHUNK // cfg.seq_len_s
    sub_seq = piece // cfg.seq_len_s
    row0 = cfg.srow0 + (i - cfg.n_ptiles) * (rows // cfg.seq_len_s)

    def body(c, carry):
        r0 = pl.multiple_of(row0 + c * nseq, SUBLANES)
        ms_chunk = [m[pl.ds(r0, nseq), :] for m in mod_refs]
        for q in range(ROW_CHUNK // piece):
            rs = _piece_rows(c, q, piece)
            ms = [mc[q * sub_seq:(q + 1) * sub_seq][:, None, :] for mc in ms_chunk]
            x3 = [x[rs, :].reshape(sub_seq, cfg.seq_len_s, x.shape[-1]) for x in x_refs]
            o_ref[rs, :] = fn(x3, ms).reshape(piece, o_ref.shape[-1]).astype(o_ref.dtype)
        return carry
    lax.fori_loop(0, rows // ROW_CHUNK, body, 0)


def _per_seq(i, cfg, xp_refs, xs_refs, mod_refs, fn, o_ref, piece=ROW_CHUNK):
    @pl.when(i < cfg.n_ptiles)
    def _():
        s = i // cfg.tiles_per_seq
        ms = [m[pl.ds(s, 1), :] for m in mod_refs]

        def body(c, carry):
            for q in range(ROW_CHUNK // piece):
                rs = _piece_rows(c, q, piece)
                o_ref[rs, :] = fn([x[rs, :] for x in xp_refs], ms).astype(o_ref.dtype)
            return carry
        lax.fori_loop(0, o_ref.shape[0] // ROW_CHUNK, body, 0)

    @pl.when(i >= cfg.n_ptiles)
    def _():
        _sample_rows(i, cfg, xs_refs, mod_refs, fn, o_ref, piece)


def _prompt_block(cfg):
    return lambda i: jnp.minimum(i, cfg.n_ptiles - 1)


def _sample_block(cfg):
    return lambda i: jnp.maximum(i - cfg.n_ptiles, 0)


def _mod_body(c_ref, w_ref, b_ref, o_ref):
    sc = _silu(c_ref[...]).astype(BF16)
    o_ref[...] = jnp.dot(sc, w_ref[...].astype(BF16), preferred_element_type=F32) + b_ref[...]


def _mod_table(c_all, mod_w, mod_b, layer):
    _, d, n = mod_w.shape
    r = c_all.shape[0]
    tn = 1024
    return pl.pallas_call(
        _mod_body,
        grid=(n // tn,),
        in_specs=[
            pl.BlockSpec((r, d), lambda j: (0, 0)),
            pl.BlockSpec((None, d, tn), lambda j: (layer, 0, j)),
            pl.BlockSpec((None, 1, tn), lambda j: (layer, 0, j)),
        ],
        out_specs=pl.BlockSpec((r, tn), lambda j: (0, j)),
        out_shape=jax.ShapeDtypeStruct((r, n), F32),
        compiler_params=_cparams(("arbitrary",)),
        name="mod_table",
    )(c_all, mod_w, mod_b.reshape(-1, 1, n))


def _norm_prologue(i, j, cfg, xp_ref, xs_ref, nw_ref, sc_ref, sh_ref, h_ref):
    @pl.when(j == 0)
    def _():
        _per_seq(i, cfg, [xp_ref], [xs_ref], [sc_ref, sh_ref],
                 lambda xs, ms: _rms_mod(xs[0], nw_ref[...], ms[0], ms[1]), h_ref, piece=SUB_ROWS)


def _gmlp_in_body(xp_ref, xs_ref, nw_ref, sh_ref, sc_ref, wu_ref, wv_ref, bu_ref, bv_ref, u_ref, v_ref, h_ref, *, cfg):
    i, j = pl.program_id(0), pl.program_id(1)
    _norm_prologue(i, j, cfg, xp_ref, xs_ref, nw_ref, sc_ref, sh_ref, h_ref)
    h = h_ref[...]
    u_ref[...] = _gelu(jnp.dot(h, wu_ref[...].astype(BF16), preferred_element_type=F32) + bu_ref[...])
    v_ref[...] = _gelu(jnp.dot(h, wv_ref[...].astype(BF16), preferred_element_type=F32) + bv_ref[...])


def _gmlp_in(x_p, x_s, mods, layer, norm_w, w_in, b_in, j_layer, cfg):
    d = x_p.shape[1]
    m = x_p.shape[0] + x_s.shape[0]
    width = w_in.shape[-1] // 2
    tn = TN_NARROW
    nj = width // tn
    r = mods.shape[0]
    pb, sb = _prompt_block(cfg), _sample_block(cfg)
    b2 = b_in.reshape(b_in.shape[0], 1, -1)
    return pl.pallas_call(
        functools.partial(_gmlp_in_body, cfg=cfg),
        grid=(m // TM, nj),
        in_specs=[
            pl.BlockSpec((TM, d), lambda i, j: (pb(i), 0)),
            pl.BlockSpec((TM, d), lambda i, j: (sb(i), 0), pipeline_mode=pl.Buffered(1)),
            pl.BlockSpec((None, 1, d), lambda i, j: (layer, 0, 0)),
            pl.BlockSpec((r, d), lambda i, j: (0, 0)),
            pl.BlockSpec((r, d), lambda i, j: (0, 1)),
            pl.BlockSpec((None, d, tn), lambda i, j: (j_layer, 0, j)),
            pl.BlockSpec((None, d, tn), lambda i, j: (j_layer, 0, j + nj)),
            pl.BlockSpec((None, 1, tn), lambda i, j: (j_layer, 0, j)),
            pl.BlockSpec((None, 1, tn), lambda i, j: (j_layer, 0, j + nj)),
        ],
        out_specs=[pl.BlockSpec((TM, tn), lambda i, j: (i, j)), pl.BlockSpec((TM, tn), lambda i, j: (i, j))],
        out_shape=[jax.ShapeDtypeStruct((m, width), F32), jax.ShapeDtypeStruct((m, width), F32)],
        scratch_shapes=[pltpu.VMEM((TM, d), BF16)],
        compiler_params=_cparams(("arbitrary", "arbitrary")),
        name="gmlp_in",
    )(x_p, x_s, norm_w.reshape(-1, 1, d), mods, mods, w_in, w_in, b2, b2)


def _ffn_in_body(x_ref, nw_ref, sh_ref, sc_ref, wg_ref, wu_ref, a_ref, h_ref, *, cfg):
    i, j = pl.program_id(0), pl.program_id(1)
    _norm_prologue(i, j, cfg, x_ref, x_ref, nw_ref, sc_ref, sh_ref, h_ref)
    h = h_ref[...]
    gate = jnp.dot(h, wg_ref[...].astype(BF16), preferred_element_type=F32)
    up = jnp.dot(h, wu_ref[...].astype(BF16), preferred_element_type=F32)
    a_ref[...] = (_silu(gate) * up).astype(BF16)


def _ffn_in(x, mods, layer, norm_w, w_in, cfg):
    m, d = x.shape
    hidden = w_in.shape[-1] // 2
    nj = hidden // TN
    r = mods.shape[0]
    return pl.pallas_call(
        functools.partial(_ffn_in_body, cfg=cfg),
        grid=(m // TM, nj),
        in_specs=[
            pl.BlockSpec((TM, d), lambda i, j: (i, 0)),
            pl.BlockSpec((None, 1, d), lambda i, j: (layer, 0, 0)),
            pl.BlockSpec((r, d), lambda i, j: (0, 3)),
            pl.BlockSpec((r, d), lambda i, j: (0, 4)),
            pl.BlockSpec((None, d, TN), lambda i, j: (layer, 0, j)),
            pl.BlockSpec((None, d, TN), lambda i, j: (layer, 0, j + nj)),
        ],
        out_specs=pl.BlockSpec((TM, TN), lambda i, j: (i, j)),
        out_shape=jax.ShapeDtypeStruct((m, hidden), BF16),
        scratch_shapes=[pltpu.VMEM((TM, d), BF16)],
        compiler_params=_cparams(("arbitrary", "arbitrary")),
        name="ffn_in",
    )(x, norm_w.reshape(-1, 1, d), mods, mods, w_in, w_in)


_NT = (((1,), (1,)), ((), ()))


def _ssd_in_body(x_ref, nw_ref, sh_ref, sc_ref, w_ref, wdt_ref, o_ref, dt_ref, h_ref, *, cfg):
    i, j = pl.program_id(0), pl.program_id(1)
    _norm_prologue(i, j, cfg, x_ref, x_ref, nw_ref, sc_ref, sh_ref, h_ref)
    h = h_ref[...]
    o_ref[...] = lax.dot_general(h, w_ref[...].astype(BF16), _NT, preferred_element_type=F32)

    @pl.when(j == 0)
    def _():
        dt_ref[...] = lax.dot_general(h, wdt_ref[...].astype(BF16), _NT, preferred_element_type=F32)


def _ssd_in(x, mods, layer, norm_w, w_in_t, n_main, cfg):
    m, d = x.shape
    r = mods.shape[0]
    n_dt = w_in_t.shape[0] - n_main
    nj = n_main // TN
    return pl.pallas_call(
        functools.partial(_ssd_in_body, cfg=cfg),
        grid=(m // TM, nj),
        in_specs=[
            pl.BlockSpec((TM, d), lambda i, j: (i, 0)),
            pl.BlockSpec((None, 1, d), lambda i, j: (layer, 0, 0)),
            pl.BlockSpec((r, d), lambda i, j: (0, 0)),
            pl.BlockSpec((r, d), lambda i, j: (0, 1)),
            pl.BlockSpec((TN, d), lambda i, j: (j, 0)),
            pl.BlockSpec((n_dt, d), lambda i, j: (n_main // n_dt, 0)),
        ],
        out_specs=[pl.BlockSpec((None, TM, TN), lambda i, j: (j, i, 0)), pl.BlockSpec((TM, n_dt), lambda i, j: (i, 0))],
        out_shape=[jax.ShapeDtypeStruct((nj, m, TN), F32), jax.ShapeDtypeStruct((m, n_dt), F32)],
        scratch_shapes=[pltpu.VMEM((TM, d), BF16)],
        compiler_params=_cparams(("arbitrary", "arbitrary")),
        name="ssd_in",
    )(x, norm_w.reshape(-1, 1, d), mods, mods, w_in_t, w_in_t)


TK = 512


def _resid_body(a_ref, *rest, cfg, nk, two_a):
    as_ref = None
    if two_a:
        as_ref, *rest = rest
    w_ref, *rest = rest
    if len(rest) == 4:
        xp_ref, xs_ref, g_ref, o_ref = rest
    else:
        xp_ref, g_ref, o_ref = rest
        xs_ref = xp_ref
    i, k = pl.program_id(0), pl.program_id(1)
    d = o_ref.shape[1]

    assert nk > 1
    prompt = i < cfg.n_ptiles
    last = k == nk - 1

    def accumulate(mode):
        a = a_ref[...] if as_ref is None else jnp.where(prompt, a_ref[...], as_ref[...])
        for c in range(d // TN):
            cols = slice(c * TN, (c + 1) * TN)
            part = jnp.dot(a, w_ref[:, cols].astype(BF16), preferred_element_type=F32)
            if mode == "first":
                o_ref[:, cols] = part
            elif mode == "add":
                o_ref[:, cols] += part
            else:
                gate = g_ref[pl.ds(i // cfg.tiles_per_seq, 1), cols]
                o_ref[:, cols] = xp_ref[:, cols] + gate * (o_ref[:, cols] + part)

    @pl.when(k == 0)
    def _():
        accumulate("first")

    @pl.when(jnp.logical_and(k > 0, jnp.logical_not(jnp.logical_and(last, prompt))))
    def _():
        accumulate("add")

    @pl.when(jnp.logical_and(last, prompt))
    def _():
        accumulate("finish")

    @pl.when(jnp.logical_and(last, jnp.logical_not(prompt)))
    def _():
        _sample_rows(i, cfg, [xs_ref, o_ref], [g_ref], lambda xs, ms: xs[0] + ms[0] * xs[1], o_ref)


def _resid_matmul(a, w, w_layer, x_p, x_s, mods, gate_chunk, cfg, name, a_s=None):
    m = a.shape[-2] + (0 if a_s is None else a_s.shape[-2])
    d = x_p.shape[1]
    r = mods.shape[0]
    pb, sb = _prompt_block(cfg), _sample_block(cfg)
    if a.ndim == 2:
        nk = a.shape[1] // TK
        a_specs, a_args = [pl.BlockSpec((TM, TK), lambda i, k: (i, k))], [a]
    else:
        nk = a.shape[0]
        assert a.shape[2] == TK
        if a_s is None:
            a_specs, a_args = [pl.BlockSpec((None, TM, TK), lambda i, k: (k, i, 0))], [a]
        else:
            a_specs = [pl.BlockSpec((None, TM, TK), lambda i, k: (jnp.where(i < cfg.n_ptiles, k, 0), pb(i), 0)),
                       pl.BlockSpec((None, TM, TK), lambda i, k: (jnp.where(i < cfg.n_ptiles, 0, k), sb(i), 0))]
            a_args = [a, a_s]
    if x_s is None:
        x_specs, xs = [pl.BlockSpec((TM, d), lambda i, k: (i, 0))], [x_p]
    else:
        x_specs = [pl.BlockSpec((TM, d), lambda i, k: (pb(i), 0)),
                   pl.BlockSpec((TM, d), lambda i, k: (sb(i), 0), pipeline_mode=pl.Buffered(1))]
        xs = [x_p, x_s]
    return pl.pallas_call(
        functools.partial(_resid_body, cfg=cfg, nk=nk, two_a=a_s is not None),
        grid=(m // TM, nk),
        in_specs=a_specs + [pl.BlockSpec((None, TK, d), lambda i, k: (w_layer, k, 0))] + x_specs
        + [pl.BlockSpec((r, d), lambda i, k: (0, gate_chunk))],
        out_specs=pl.BlockSpec((TM, d), lambda i, k: (i, 0)),
        out_shape=jax.ShapeDtypeStruct((m, d), F32),
        compiler_params=_cparams(("arbitrary", "arbitrary")),
        name=name,
    )(*a_args, w, *xs, mods)


MIX_ROWS = 2 * GMLP_CHUNK


def _gmlp_mix_body(u_ref, v_ref, lnw_ref, lnb_ref, ws_ref, mask_ref, bias_ref, g_ref, vp_ref, vs_ref,
                   vn_ref, *, n_prompt_steps, steps_per_seq):
    t = pl.program_id(0)
    v = v_ref[...]
    xc = v - jnp.mean(v, axis=-1, keepdims=True)
    vn = xc * lax.rsqrt(jnp.mean(xc * xc, axis=-1, keepdims=True) + LN_EPS) * lnw_ref[...] + lnb_ref[...]
    vn_ref[...] = vn

    @pl.when(jnp.logical_and(t < n_prompt_steps, t % steps_per_seq == steps_per_seq - 1))
    def _():
        vp_ref[...] = vn[MIX_ROWS - GMLP_CHUNK:, :]

    @pl.when(t >= n_prompt_steps)
    def _():
        vs_ref[...] = vn

    mask = mask_ref[...]
    for g in range(GMLP_GROUPS):
        wb = (ws_ref[g] * mask).astype(BF16)
        cols = slice(g * GMLP_CHUNK, (g + 1) * GMLP_CHUNK)
        for c in range(MIX_ROWS // GMLP_CHUNK):
            rows = slice(c * GMLP_CHUNK, (c + 1) * GMLP_CHUNK)
            s = jnp.dot(wb, vn_ref[rows, cols].astype(BF16), preferred_element_type=F32) + bias_ref[:, cols]
            g_ref[rows, cols] = (u_ref[rows, cols] * s).astype(BF16)


def _gmlp_mix(u, v, ln_w, ln_b, wmix, mask, bias, n_prompt_rows, seq_len, n_prompt_seq):
    m, width = u.shape
    n_prompt_steps = n_prompt_rows // MIX_ROWS
    steps_per_seq = seq_len // MIX_ROWS
    n_sample_rows = m - n_prompt_rows

    def variant(t):
        return jnp.where(t < n_prompt_steps, 0, 1)

    return pl.pallas_call(
        functools.partial(_gmlp_mix_body, n_prompt_steps=n_prompt_steps, steps_per_seq=steps_per_seq),
        grid=(m // MIX_ROWS,),
        in_specs=[
            pl.BlockSpec((MIX_ROWS, width), lambda t: (t, 0)),
            pl.BlockSpec((MIX_ROWS, width), lambda t: (t, 0)),
            pl.BlockSpec((1, width), lambda t: (0, 0)),
            pl.BlockSpec((1, width), lambda t: (0, 0)),
            pl.BlockSpec((None, GMLP_GROUPS, GMLP_CHUNK, GMLP_CHUNK), lambda t: (variant(t), 0, 0, 0)),
            pl.BlockSpec((None, GMLP_CHUNK, GMLP_CHUNK), lambda t: (variant(t), 0, 0)),
            pl.BlockSpec((None, GMLP_CHUNK, width), lambda t: (variant(t), 0, 0)),
        ],
        out_specs=[
            pl.BlockSpec((MIX_ROWS, width), lambda t: (t, 0)),
            pl.BlockSpec((GMLP_CHUNK, width), lambda t: (jnp.minimum(t // steps_per_seq, n_prompt_seq - 1), 0)),
            pl.BlockSpec((MIX_ROWS, width), lambda t: (jnp.maximum(t - n_prompt_steps, 0), 0)),
        ],
        out_shape=[
            jax.ShapeDtypeStruct((m, width), BF16),
            jax.ShapeDtypeStruct((n_prompt_seq * GMLP_CHUNK, width), F32),
            jax.ShapeDtypeStruct((n_sample_rows, width), F32),
        ],
        scratch_shapes=[pltpu.VMEM((MIX_ROWS, width), F32)],
        compiler_params=_cparams(("arbitrary",)),
        name="gmlp_mix",
    )(u, v, ln_w.reshape(1, width), ln_b.reshape(1, width), wmix, mask, bias)


GROUP_HEADS = 8
GROUP_W = GROUP_HEADS * SSD_HEAD_DIM
T = SSD_CHUNK
BC_PER_BLOCK = GROUP_W // SSD_STATE


def _split3(x):
    hi = x.astype(BF16)
    r = x - hi.astype(F32)
    mid = r.astype(BF16)
    lo = (r - mid.astype(F32)).astype(BF16)
    return hi, mid, lo


def _dot_exact_rhs(m_b, pieces):
    return sum(jnp.dot(m_b, p, preferred_element_type=F32) for p in pieces)


def _dot_exact_lhs(pieces, e_b):
    return sum(jnp.dot(p, e_b, preferred_element_type=F32) for p in pieces)


def _ssd_group(xs, bm, cm, z, dt_x, cs_x, cs_end_x, cs_row, d_a, states, n_seg, dsk, nw, keep):
    seg = T // n_seg
    xdt = xs * dt_x
    ecs_x = jnp.exp(cs_x)
    dte_x = jnp.exp(cs_end_x - cs_x)
    xdt_b = xdt.astype(BF16)
    xd_t = (xdt * dte_x).T.astype(BF16)
    bmb = bm.astype(BF16)
    cmb = cm.astype(BF16)
    cb = lax.dot_general(cmb, bmb, _NT, preferred_element_type=F32)

    row = lax.broadcasted_iota(jnp.int32, (T, SSD_STATE), 0)
    y_offs, new_states = [], []
    for s in range(n_seg):
        st = states[s]
        c_seg = cmb if n_seg == 1 else cm[s * seg:(s + 1) * seg].astype(BF16)
        y_offs.append(lax.dot_general(c_seg, st.astype(BF16), _NT, preferred_element_type=F32))
        if n_seg == 1:
            b_seg = bmb
        else:
            b_seg = jnp.where(jnp.logical_and(row >= s * seg, row < (s + 1) * seg), bm, 0.0).astype(BF16)
        upd = jnp.dot(xd_t, b_seg, preferred_element_type=F32)
        decayed = jnp.concatenate(
            [st[h * SSD_HEAD_DIM:(h + 1) * SSD_HEAD_DIM, :] * d_a(s, h) for h in range(GROUP_HEADS)], axis=0)
        new_states.append(decayed + upd)
    y_off = jnp.concatenate(y_offs, axis=0) if n_seg > 1 else y_offs[0]

    lane = lax.broadcasted_iota(jnp.int32, (T, LANES), 1)
    lo_half = lane < SSD_HEAD_DIM
    ys = []
    for q in range(GROUP_HEADS // 2):
        v = cs_x[:, q * LANES:(q + 1) * LANES]
        r = pltpu.roll(v, SSD_HEAD_DIM, 1)
        cols = (jnp.where(lo_half, v, r), jnp.where(lo_half, r, v))
        ws = []
        for e in range(2):
            diff = cols[e] - cs_row(2 * q + e)
            ws.append((cb * jnp.exp(jnp.where(keep, diff, -jnp.inf))).astype(BF16))
        xp = xdt_b[:, q * LANES:(q + 1) * LANES]
        zero = jnp.zeros_like(xp)
        xpair = jnp.concatenate([jnp.where(lo_half, xp, zero), jnp.where(lo_half, zero, xp)], axis=0)
        ys.append(jnp.dot(jnp.concatenate(ws, axis=1), xpair, preferred_element_type=F32))
    y_diag = jnp.concatenate(ys, axis=1)

    y = y_diag + y_off * ecs_x + dsk * xs
    y = y * _silu(z)
    yn = y * lax.rsqrt(jnp.mean(y * y, axis=-1, keepdims=True) + NORM_EPS) * nw
    return yn.astype(BF16), new_states


def _causal_keep(seq_len):
    r = lax.broadcasted_iota(jnp.int32, (T, T), 0)
    c = lax.broadcasted_iota(jnp.int32, (T, T), 1)
    keep = c <= r
    if seq_len < T:
        keep = jnp.logical_and(keep, (r // seq_len) == (c // seq_len))
    return keep


def _ssd_prompt_body(z_ref, xs_ref, bc_ref, dt_ref, cw_ref, cbias_ref, dtb_ref, alog_ref, dsk_ref, nw_ref,
                     m_ref, e_ref, yn_ref, st_ref,
                     cbuf_ref, xc_ref, bcs_ref, dt3_ref, cs3_ref, cst_ref, *, c):
    n_x = xs_ref.shape[0]
    pad = SUBLANES
    taps = SSD_CONV - 1

    @pl.when(c == 0)
    def _():
        st_ref[...] = jnp.zeros_like(st_ref)
        cbuf_ref[:, 0:pad, :] = jnp.zeros((cbuf_ref.shape[0], pad, GROUP_W), F32)

    def conv(k, src):
        w = cw_ref[k]
        cbuf_ref[k, pad:pad + T, :] = src
        acc = cbias_ref[k] + src * w[taps:taps + 1, :]
        for kk in range(taps):
            acc = acc + cbuf_ref[k, pad - taps + kk:pad - taps + kk + T, :] * w[kk:kk + 1, :]
        cbuf_ref[k, pad - taps:pad, :] = cbuf_ref[k, pad + T - taps:pad + T, :]
        return _silu(acc)

    def conv_x(k, carry):
        xc_ref[k] = conv(k, xs_ref[k])
        return carry
    lax.fori_loop(0, n_x, conv_x, 0)

    def conv_bc(k, carry):
        out = conv(n_x + k, bc_ref[k])
        for gg in range(BC_PER_BLOCK):
            bcs_ref[k * BC_PER_BLOCK + gg] = out[:, gg * SSD_STATE:(gg + 1) * SSD_STATE]
        return carry
    lax.fori_loop(0, bc_ref.shape[0], conv_bc, 0)

    dt = jax.nn.softplus(dt_ref[...] + dtb_ref[...])
    a = dt * (-jnp.exp(alog_ref[...]))
    cs = _dot_exact_rhs(m_ref[...], _split3(a))
    cst_ref[...] = cs.T
    for p, (dt_p, cs_p) in enumerate(zip(_split3(dt), _split3(cs))):
        dt3_ref[p] = dt_p
        cs3_ref[p] = cs_p

    keep = _causal_keep(T)

    def spread(g):
        e_g = e_ref[g]
        return (_dot_exact_lhs([dt3_ref[p] for p in range(3)], e_g), _dot_exact_lhs([cs3_ref[p] for p in range(3)], e_g))

    def group(g, dt_x, cs_x):
        h0 = g * GROUP_HEADS
        rows = pl.ds(g * GROUP_W, GROUP_W)
        yn, new_states = _ssd_group(
            xc_ref[g], bcs_ref[g], bcs_ref[SSD_GROUPS + g], z_ref[g], dt_x, cs_x, cs_x[T - 1:T, :],
            lambda h: cst_ref[pl.ds(h0 + h, 1), :],
            lambda s, h: jnp.exp(cst_ref[pl.ds(h0 + h, 1), T - 1:T]),
            [st_ref[rows, :]], 1, dsk_ref[g], nw_ref[g], keep)
        yn_ref[g] = yn
        st_ref[rows, :] = new_states[0]

    spread_next = spread(0)
    for g in range(SSD_GROUPS):
        dt_x, cs_x = spread_next
        if g + 1 < SSD_GROUPS:
            spread_next = spread(g + 1)
        group(g, dt_x, cs_x)


def _ssd_sample_body(z_ref, xs_ref, b_ref, c_ref, px_ref, pb_ref, pc_ref, dt_ref, wx_ref, wb_ref, wc_ref,
                     bx_ref, bb_ref, bc_ref, dtb_ref, alog_ref, dsk_ref, nw_ref, m_ref, mseg_ref, e_ref,
                     st_in_ref, yn_ref, st_ref, cst_ref, cet_ref, *, seq_len, g):
    n_seg = T // seq_len

    def conv(x_ref, p_ref, w_ref, bias_ref):
        x = x_ref[...]
        width = x.shape[-1]
        taps = SSD_CONV - 1
        prev = [jnp.broadcast_to(p_ref[k][:, None, :], (n_seg, seq_len, width)).reshape(T, width) for k in range(taps)]
        tpos = lax.broadcasted_iota(jnp.int32, x.shape, 0) % seq_len
        acc = bias_ref[...] + x * w_ref[taps:SSD_CONV, :]
        for sh in range(1, SSD_CONV):
            hist = prev[taps - 1]
            for t in range(sh - 2, -1, -1):
                hist = jnp.where(tpos == t, prev[taps - sh + t], hist)
            shifted = jnp.where(tpos >= sh, pltpu.roll(x, sh, 0), hist)
            acc = acc + shifted * w_ref[taps - sh:SSD_CONV - sh, :]
        return _silu(acc)

    xs = conv(xs_ref, px_ref, wx_ref, bx_ref)
    bm = conv(b_ref, pb_ref, wb_ref, bb_ref)
    cm = conv(c_ref, pc_ref, wc_ref, bc_ref)

    dt = jax.nn.softplus(dt_ref[...] + dtb_ref[...])
    a3 = _split3(dt * (-jnp.exp(alog_ref[...])))
    cs = _dot_exact_rhs(m_ref[...], a3)
    cs_end = _dot_exact_rhs(mseg_ref[...], a3)
    cst_ref[...] = cs.T
    cet_ref[...] = cs_end.T
    e_g = e_ref[...]
    h0 = g * GROUP_HEADS
    yn, new_states = _ssd_group(
        xs, bm, cm, z_ref[...], _dot_exact_lhs(_split3(dt), e_g), _dot_exact_lhs(_split3(cs), e_g),
        _dot_exact_lhs(_split3(cs_end), e_g),
        lambda h: cst_ref[pl.ds(h0 + h, 1), :],
        lambda s, h: jnp.exp(cet_ref[pl.ds(h0 + h, 1), s * seq_len:s * seq_len + 1]),
        [st_in_ref[s] for s in range(n_seg)], n_seg, dsk_ref[...], nw_ref[...], _causal_keep(seq_len))
    yn_ref[...] = yn
    for s in range(n_seg):
        st_ref[s] = new_states[s]


def _ssd_masks(seq_len):
    r = jnp.arange(T)
    same = (r[:, None] // seq_len) == (r[None, :] // seq_len)
    return jnp.logical_and(same, r[None, :] <= r[:, None]).astype(BF16), same.astype(BF16)


N_PROMPT_IN, N_SAMPLE_IN = 12, 22


def _ssd_scan_body(*refs, n_chunks, seq_len_s):
    s = pl.program_id(0)
    p_in = refs[:N_PROMPT_IN]
    s_in = refs[N_PROMPT_IN:N_PROMPT_IN + N_SAMPLE_IN]
    yn_p, st_p, yn_s, st_s = refs[N_PROMPT_IN + N_SAMPLE_IN:N_PROMPT_IN + N_SAMPLE_IN + 4]
    scratch = refs[N_PROMPT_IN + N_SAMPLE_IN + 4:]
    _ssd_prompt_body(*p_in, yn_p, st_p, *scratch[:6], c=s % n_chunks)
    _ssd_sample_body(*s_in, yn_s, st_s, *scratch[6:], seq_len=seq_len_s, g=s % SSD_GROUPS)


def _ssd_scan(zx, dt_raw, conv_w, conv_b, dtb, alog, dsk, nw, state_s, conv_hist_s, n_pseq, seq_len_p, seq_len_s):
    heads = dt_raw.shape[1]
    inner = SSD_GROUPS * GROUP_W
    n_chunks = seq_len_p // T
    m_p = n_pseq * seq_len_p
    n_seg = T // seq_len_s
    n_sseq = state_s.shape[0]
    row0 = m_p // T
    n_steps = n_pseq * n_chunks
    assert n_steps == (n_sseq // n_seg) * SSD_GROUPS
    n_bc = 2 * SSD_GROUPS // BC_PER_BLOCK
    n_cblk = SSD_GROUPS + n_bc
    cw = conv_w.reshape(SSD_CONV, n_cblk, GROUP_W).transpose(1, 0, 2)
    cbias = conv_b.reshape(n_cblk, 1, GROUP_W)
    e_all = (jnp.arange(inner)[None, :] // SSD_HEAD_DIM == jnp.arange(heads)[:, None]).astype(BF16)
    e_grp = e_all.reshape(heads, SSD_GROUPS, GROUP_W).transpose(1, 0, 2)
    m_tril, _ = _ssd_masks(T)
    m_s, mseg_s = _ssd_masks(seq_len_s)
    xoff = SSD_GROUPS
    boff = 2 * SSD_GROUPS
    coff = boff + SSD_GROUPS // BC_PER_BLOCK
    cb0 = inner // SSD_STATE
    cc0 = cb0 + SSD_GROUPS

    full = lambda shape: pl.BlockSpec(shape, lambda s: (0,) * len(shape))
    bp = lambda s: s // n_chunks
    ts = lambda s: s // SSD_GROUPS
    gs = lambda s: s % SSD_GROUPS
    prompt_in = [
        pl.BlockSpec((SSD_GROUPS, T, GROUP_W), lambda s: (0, s, 0)),
        pl.BlockSpec((SSD_GROUPS, T, GROUP_W), lambda s: (1, s, 0)),
        pl.BlockSpec((n_bc, T, GROUP_W), lambda s: (2 * SSD_GROUPS // n_bc, s, 0)),
        pl.BlockSpec((T, heads), lambda s: (s, 0)),
        full((n_cblk, SSD_CONV, GROUP_W)), full((n_cblk, 1, GROUP_W)), full((1, heads)), full((1, heads)),
        full((SSD_GROUPS, 1, GROUP_W)), full((SSD_GROUPS, 1, GROUP_W)), full((T, T)),
        full((SSD_GROUPS, heads, GROUP_W)),
    ]
    per_g = lambda shape: pl.BlockSpec(shape, lambda s: (gs(s),) + (0,) * (len(shape) - 1))
    sample_in = [
        pl.BlockSpec((None, T, GROUP_W), lambda s: (gs(s), row0 + ts(s), 0)),
        pl.BlockSpec((None, T, GROUP_W), lambda s: (xoff + gs(s), row0 + ts(s), 0)),
        pl.BlockSpec((None, T, SSD_STATE),
                     lambda s: (boff + gs(s) // BC_PER_BLOCK, row0 + ts(s), gs(s) % BC_PER_BLOCK)),
        pl.BlockSpec((None, T, SSD_STATE),
                     lambda s: (coff + gs(s) // BC_PER_BLOCK, row0 + ts(s), gs(s) % BC_PER_BLOCK)),
        pl.BlockSpec((SSD_CONV - 1, n_seg, GROUP_W), lambda s: (0, ts(s), gs(s))),
        pl.BlockSpec((SSD_CONV - 1, n_seg, SSD_STATE), lambda s: (0, ts(s), cb0 + gs(s))),
        pl.BlockSpec((SSD_CONV - 1, n_seg, SSD_STATE), lambda s: (0, ts(s), cc0 + gs(s))),
        pl.BlockSpec((T, heads), lambda s: (row0 + ts(s), 0)),
        pl.BlockSpec((SSD_CONV, GROUP_W), lambda s: (0, gs(s))),
        pl.BlockSpec((SSD_CONV, SSD_STATE), lambda s: (0, cb0 + gs(s))),
        pl.BlockSpec((SSD_CONV, SSD_STATE), lambda s: (0, cc0 + gs(s))),
        pl.BlockSpec((1, GROUP_W), lambda s: (0, gs(s))),
        pl.BlockSpec((1, SSD_STATE), lambda s: (0, cb0 + gs(s))),
        pl.BlockSpec((1, SSD_STATE), lambda s: (0, cc0 + gs(s))),
        full((1, heads)), full((1, heads)),
        per_g((None, 1, GROUP_W)), per_g((None, 1, GROUP_W)),
        full((T, T)), full((T, T)),
        pl.BlockSpec((heads, GROUP_W), lambda s: (0, gs(s))),
        pl.BlockSpec((n_seg, GROUP_W, SSD_STATE), lambda s: (ts(s), gs(s), 0)),
    ]
    assert len(prompt_in) == N_PROMPT_IN and len(sample_in) == N_SAMPLE_IN
    return pl.pallas_call(
        functools.partial(_ssd_scan_body, n_chunks=n_chunks, seq_len_s=seq_len_s),
        grid=(n_steps,),
        in_specs=prompt_in + sample_in,
        out_specs=[pl.BlockSpec((SSD_GROUPS, T, GROUP_W), lambda s: (0, s, 0)),
                   pl.BlockSpec((None, inner, SSD_STATE), lambda s: (bp(s), 0, 0)),
                   pl.BlockSpec((None, T, GROUP_W), lambda s: (gs(s), ts(s), 0)),
                   pl.BlockSpec((n_seg, GROUP_W, SSD_STATE), lambda s: (ts(s), gs(s), 0))],
        out_shape=[jax.ShapeDtypeStruct((SSD_GROUPS, m_p, GROUP_W), BF16),
                   jax.ShapeDtypeStruct((n_pseq, inner, SSD_STATE), F32),
                   jax.ShapeDtypeStruct((SSD_GROUPS, n_sseq * seq_len_s, GROUP_W), BF16),
                   jax.ShapeDtypeStruct(state_s.shape, F32)],
        scratch_shapes=[pltpu.VMEM((n_cblk, SUBLANES + T, GROUP_W), F32),
                        pltpu.VMEM((SSD_GROUPS, T, GROUP_W), F32),
                        pltpu.VMEM((2 * SSD_GROUPS, T, SSD_STATE), F32),
                        pltpu.VMEM((3, T, heads), BF16),
                        pltpu.VMEM((3, T, heads), BF16),
                        pltpu.VMEM((heads, T), F32),
                        pltpu.VMEM((heads, T), F32), pltpu.VMEM((heads, T), F32)],
        compiler_params=_cparams(("arbitrary",)),
        name="ssd_scan",
    )(zx, zx, zx, dt_raw, cw, cbias, dtb, alog, dsk, nw, m_tril, e_grp,
      zx, zx, zx, zx, conv_hist_s, conv_hist_s, conv_hist_s, dt_raw, conv_w, conv_w, conv_w, conv_b, conv_b, conv_b,
      dtb, alog, dsk, nw, m_s, mseg_s, e_all, state_s)


def _final_body(x_ref, w_ref, yp_ref, ys_ref, *, n_ptiles):
    i = pl.program_id(0)
    x = x_ref[...]
    y = x * lax.rsqrt(jnp.mean(x * x, axis=-1, keepdims=True) + NORM_EPS) * w_ref[...]

    @pl.when(i < n_ptiles)
    def _():
        yp_ref[...] = y

    @pl.when(i >= n_ptiles)
    def _():
        ys_ref[...] = y


def _final_norm(x, w, n_prompt_rows):
    m, d = x.shape
    tm = 512
    n_ptiles = n_prompt_rows // tm
    return pl.pallas_call(
        functools.partial(_final_body, n_ptiles=n_ptiles),
        grid=(m // tm,),
        in_specs=[pl.BlockSpec((tm, d), lambda i: (i, 0)), pl.BlockSpec((1, d), lambda i: (0, 0))],
        out_specs=[pl.BlockSpec((tm, d), lambda i: (jnp.minimum(i, n_ptiles - 1), 0)),
                   pl.BlockSpec((tm, d), lambda i: (jnp.maximum(i - n_ptiles, 0), 0))],
        out_shape=[jax.ShapeDtypeStruct((n_prompt_rows, d), F32), jax.ShapeDtypeStruct((m - n_prompt_rows, d), F32)],
        compiler_params=_cparams(("arbitrary",)),
        name="final_norm",
    )(x, w.reshape(1, d))


def kernel(x_prompt, x_sample, c_prompt, c_sample, state_ssm, state_conv, mod_w, mod_b, norm_mix_w, norm_ffn_w,
           a_w_in, a_b_in, a_ln_w, a_ln_b, a_w_s, a_b_s, a_w_out,
           b_w_in, b_conv_w, b_conv_b, b_dt_bias, b_a_log, b_d, b_norm_w, b_w_out,
           f_w_in, f_w_out, final_norm_w):
    bp, lp, d = x_prompt.shape
    bs, ls, _ = x_sample.shape
    depth = mod_w.shape[0]
    n_prompt = bp * lp
    n_sample = bs * ls
    m_all = n_prompt + n_sample
    assert lp % TM == 0 and n_sample % TM == 0 and TM % ls == 0 and bp <= SUBLANES
    assert ls >= SSD_CONV - 1 and T % ls == 0 and lp % T == 0 and GMLP_CHUNK % ls == 0 and lp % MIX_ROWS == 0
    assert depth == 2 and a_w_in.shape[0] == 1 and b_w_in.shape[0] == 1
    cfg = Cfg(n_ptiles=n_prompt // TM, tiles_per_seq=lp // TM, seq_len_s=ls, srow0=SUBLANES)

    x_p = x_prompt.reshape(n_prompt, d)
    x_s = x_sample.reshape(n_sample, d)
    c_all = jnp.concatenate([c_prompt, jnp.zeros((SUBLANES - bp, d), F32), c_sample], axis=0)
    mods = _mod_table(c_all, mod_w, mod_b, 0)

    u, v = _gmlp_in(x_p, x_s, mods, 0, norm_mix_w, a_w_in, a_b_in, 0, cfg)
    r = jnp.arange(GMLP_CHUNK)
    tril = r[None, :] <= r[:, None]
    mask = jnp.stack([tril, jnp.logical_and(tril, (r[:, None] // ls) == (r[None, :] // ls))]).astype(F32)
    rep = GMLP_CHUNK // ls
    wmix = jnp.stack([a_w_s[0], jnp.tile(a_w_s[0, :, :ls, :ls], (1, rep, rep))])
    width = a_w_in.shape[-1] // 2
    bias = jnp.stack([jnp.repeat(a_b_s[0].T, width // GMLP_GROUPS, axis=1),
                      jnp.repeat(jnp.tile(a_b_s[0, :, :ls].T, (rep, 1)), width // GMLP_GROUPS, axis=1)])
    gated, v_p, v_s = _gmlp_mix(u, v, a_ln_w[0], a_ln_b[0], wmix, mask, bias, n_prompt, lp, bp)
    x = _resid_matmul(gated, a_w_out, 0, x_p, x_s, mods, 2, cfg, "gmlp_out")
    act = _ffn_in(x, mods, 0, norm_ffn_w, f_w_in, cfg)
    x = _resid_matmul(act, f_w_out, 0, x, None, mods, 5, cfg, "ffn_out0")
    mods = _mod_table(c_all, mod_w, mod_b, 1)

    inner = b_w_out.shape[1]
    conv_dim = b_conv_w.shape[-1]
    n_main = inner + conv_dim
    heads = inner // SSD_HEAD_DIM
    zx, dt_raw = _ssd_in(x, mods, 1, norm_mix_w, jnp.swapaxes(b_w_in, 1, 2)[0], n_main, cfg)
    conv_hist = jnp.swapaxes(state_conv[0], 0, 1)
    ssd_params = (b_conv_w[0], b_conv_b, b_dt_bias, b_a_log,
                  jnp.repeat(b_d[0], SSD_HEAD_DIM).reshape(SSD_GROUPS, 1, GROUP_W),
                  b_norm_w.reshape(SSD_GROUPS, 1, GROUP_W))
    state_s = state_ssm[0].reshape(bs, inner, SSD_STATE)
    yn_p, ssm_p, yn_s, ssm_s = _ssd_scan(zx, dt_raw, *ssd_params, state_s, conv_hist, bp, lp, ls)
    x = _resid_matmul(yn_p, b_w_out, 0, x, None, mods, 2, cfg, "ssd_out", a_s=yn_s)
    act = _ffn_in(x, mods, 1, norm_ffn_w, f_w_in, cfg)
    x = _resid_matmul(act, f_w_out, 1, x, None, mods, 5, cfg, "ffn_out1")

    y_p, y_s = _final_norm(x, final_norm_w, n_prompt)

    zx4 = zx.reshape(zx.shape[0], m_all // ls, ls, TN)

    def tails(groups):
        t = groups[SSD_GROUPS:, :, ls - (SSD_CONV - 1):, :]
        return jnp.moveaxis(t, 0, 2).reshape(t.shape[1], SSD_CONV - 1, conv_dim)
    conv_p = tails(zx4[:, lp // ls - 1:n_prompt // ls:lp // ls])
    conv_s = tails(zx4[:, n_prompt // ls:])
    return (y_p.reshape(bp, lp, d), y_s.reshape(bs, ls, d),
            v_p.reshape(1, bp, GMLP_CHUNK, width), v_s.reshape(1, bs, ls, width),
            ssm_p.reshape(1, bp, heads, SSD_HEAD_DIM, SSD_STATE), ssm_s.reshape(1, bs, heads, SSD_HEAD_DIM, SSD_STATE),
            conv_p[None], conv_s[None])
```

```python
import functools
import math
from typing import NamedTuple

import jax
import jax.numpy as jnp
from jax import lax
from jax.experimental import pallas as pl
from jax.experimental.pallas import tpu as pltpu

F32 = jnp.float32
BF16 = jnp.bfloat16

NORM_EPS = 1e-6
LN_EPS = 1e-5

GMLP_GROUPS = 16
GMLP_CHUNK = 128
SSD_HEAD_DIM = 64
SSD_STATE = 128
SSD_GROUPS = 8
SSD_CONV = 4
SSD_CHUNK = 128

SUBLANES = 8
LANES = 128
VMEM_LIMIT_BYTES = 56 * 1024 * 1024

TM = 1024
TN = 512
TN_NARROW = 256
ROW_CHUNK = 256
SUB_ROWS = 16


class Cfg(NamedTuple):
    n_ptiles: int
    tiles_per_seq: int
    seq_len_s: int
    srow0: int


def _cparams(sem):
    return pltpu.CompilerParams(dimension_semantics=sem, vmem_limit_bytes=VMEM_LIMIT_BYTES)


def _silu(x):
    return x / (1.0 + jnp.exp(-x))


def _gelu(x):
    return 0.5 * x * (1.0 + lax.erf(x * (1.0 / math.sqrt(2.0))))


def _rms_mod(x, w, scale, shift):
    y = x * lax.rsqrt(jnp.mean(x * x, axis=-1, keepdims=True) + NORM_EPS)
    return (y * w) * (1.0 + scale) + shift


def _piece_rows(c, q, piece):
    return pl.ds(pl.multiple_of(c * ROW_CHUNK + q * piece, piece), piece)


def _sample_rows(i, cfg, x_refs, mod_refs, fn, o_ref, piece=ROW_CHUNK):
    rows = o_ref.shape[0]
    nseq = ROW_CHUNK // cfg.seq_len_s
    sub_seq = piece // cfg.seq_len_s
    row0 = cfg.srow0 + (i - cfg.n_ptiles) * (rows // cfg.seq_len_s)

    def body(c, carry):
        r0 = pl.multiple_of(row0 + c * nseq, SUBLANES)
        ms_chunk = [m[pl.ds(r0, nseq), :] for m in mod_refs]
        for q in range(ROW_CHUNK // piece):
            rs = _piece_rows(c, q, piece)
            ms = [mc[q * sub_seq:(q + 1) * sub_seq][:, None, :] for mc in ms_chunk]
            x3 = [x[rs, :].reshape(sub_seq, cfg.seq_len_s, x.shape[-1]) for x in x_refs]
            o_ref[rs, :] = fn(x3, ms).reshape(piece, o_ref.shape[-1]).astype(o_ref.dtype)
        return carry
    lax.fori_loop(0, rows // ROW_CHUNK, body, 0)


def _per_seq(i, cfg, xp_refs, xs_refs, mod_refs, fn, o_ref, piece=ROW_CHUNK):
    @pl.when(i < cfg.n_ptiles)
    def _():
        s = i // cfg.tiles_per_seq
        ms = [m[pl.ds(s, 1), :] for m in mod_refs]

        def body(c, carry):
            for q in range(ROW_CHUNK // piece):
                rs = _piece_rows(c, q, piece)
                o_ref[rs, :] = fn([x[rs, :] for x in xp_refs], ms).astype(o_ref.dtype)
            return carry
        lax.fori_loop(0, o_ref.shape[0] // ROW_CHUNK, body, 0)

    @pl.when(i >= cfg.n_ptiles)
    def _():
        _sample_rows(i, cfg, xs_refs, mod_refs, fn, o_ref, piece)


def _prompt_block(cfg):
    return lambda i: jnp.minimum(i, cfg.n_ptiles - 1)


def _sample_block(cfg):
    return lambda i: jnp.maximum(i - cfg.n_ptiles, 0)


def _mod_body(c_ref, w_ref, b_ref, o_ref):
    sc = _silu(c_ref[...]).astype(BF16)
    o_ref[...] = jnp.dot(sc, w_ref[...].astype(BF16), preferred_element_type=F32) + b_ref[...]


def _mod_table(c_all, mod_w, mod_b, layer):
    _, d, n = mod_w.shape
    r = c_all.shape[0]
    tn = 2048
    return pl.pallas_call(
        _mod_body,
        grid=(n // tn,),
        in_specs=[
            pl.BlockSpec((r, d), lambda j: (0, 0)),
            pl.BlockSpec((None, d, tn), lambda j: (layer, 0, j)),
            pl.BlockSpec((None, 1, tn), lambda j: (layer, 0, j)),
        ],
        out_specs=pl.BlockSpec((r, tn), lambda j: (0, j)),
        out_shape=jax.ShapeDtypeStruct((r, n), F32),
        compiler_params=_cparams(("arbitrary",)),
        name="mod_table",
    )(c_all, mod_w, mod_b.reshape(-1, 1, n))


def _norm_prologue(i, j, cfg, xp_ref, xs_ref, nw_ref, sc_ref, sh_ref, h_ref):
    @pl.when(j == 0)
    def _():
        _per_seq(i, cfg, [xp_ref], [xs_ref], [sc_ref, sh_ref],
                 lambda xs, ms: _rms_mod(xs[0], nw_ref[...], ms[0], ms[1]), h_ref, piece=SUB_ROWS)


def _gmlp_in_body(xp_ref, xs_ref, nw_ref, sh_ref, sc_ref, wu_ref, wv_ref, bu_ref, bv_ref, u_ref, v_ref, h_ref, *, cfg):
    i, j = pl.program_id(0), pl.program_id(1)
    _norm_prologue(i, j, cfg, xp_ref, xs_ref, nw_ref, sc_ref, sh_ref, h_ref)
    h = h_ref[...]
    u_ref[...] = _gelu(jnp.dot(h, wu_ref[...].astype(BF16), preferred_element_type=F32) + bu_ref[...])
    v_ref[...] = _gelu(jnp.dot(h, wv_ref[...].astype(BF16), preferred_element_type=F32) + bv_ref[...])


def _gmlp_in(x_p, x_s, mods, layer, norm_w, w_in, b_in, j_layer, cfg):
    d = x_p.shape[1]
    m = x_p.shape[0] + x_s.shape[0]
    width = w_in.shape[-1] // 2
    tn = TN_NARROW
    nj = width // tn
    r = mods.shape[0]
    pb, sb = _prompt_block(cfg), _sample_block(cfg)
    b2 = b_in.reshape(b_in.shape[0], 1, -1)
    return pl.pallas_call(
        functools.partial(_gmlp_in_body, cfg=cfg),
        grid=(m // TM, nj),
        in_specs=[
            pl.BlockSpec((TM, d), lambda i, j: (pb(i), 0)),
            pl.BlockSpec((TM, d), lambda i, j: (sb(i), 0), pipeline_mode=pl.Buffered(1)),
            pl.BlockSpec((None, 1, d), lambda i, j: (layer, 0, 0)),
            pl.BlockSpec((r, d), lambda i, j: (0, 0)),
            pl.BlockSpec((r, d), lambda i, j: (0, 1)),
            pl.BlockSpec((None, d, tn), lambda i, j: (j_layer, 0, j)),
            pl.BlockSpec((None, d, tn), lambda i, j: (j_layer, 0, j + nj)),
            pl.BlockSpec((None, 1, tn), lambda i, j: (j_layer, 0, j)),
            pl.BlockSpec((None, 1, tn), lambda i, j: (j_layer, 0, j + nj)),
        ],
        out_specs=[pl.BlockSpec((TM, tn), lambda i, j: (i, j)), pl.BlockSpec((TM, tn), lambda i, j: (i, j))],
        out_shape=[jax.ShapeDtypeStruct((m, width), F32), jax.ShapeDtypeStruct((m, width), F32)],
        scratch_shapes=[pltpu.VMEM((TM, d), BF16)],
        compiler_params=_cparams(("arbitrary", "arbitrary")),
        name="gmlp_in",
    )(x_p, x_s, norm_w.reshape(-1, 1, d), mods, mods, w_in, w_in, b2, b2)


def _ffn_in_body(x_ref, nw_ref, sh_ref, sc_ref, wg_ref, wu_ref, a_ref, h_ref, *, cfg):
    i, j = pl.program_id(0), pl.program_id(1)
    _norm_prologue(i, j, cfg, x_ref, x_ref, nw_ref, sc_ref, sh_ref, h_ref)
    h = h_ref[...]
    gate = jnp.dot(h, wg_ref[...].astype(BF16), preferred_element_type=F32)
    up = jnp.dot(h, wu_ref[...].astype(BF16), preferred_element_type=F32)
    a_ref[...] = (_silu(gate) * up).astype(BF16)


def _ffn_in(x, mods, layer, norm_w, w_in, cfg):
    m, d = x.shape
    hidden = w_in.shape[-1] // 2
    nj = hidden // TN
    r = mods.shape[0]
    return pl.pallas_call(
        functools.partial(_ffn_in_body, cfg=cfg),
        grid=(m // TM, nj),
        in_specs=[
            pl.BlockSpec((TM, d), lambda i, j: (i, 0)),
            pl.BlockSpec((None, 1, d), lambda i, j: (layer, 0, 0)),
            pl.BlockSpec((r, d), lambda i, j: (0, 3)),
            pl.BlockSpec((r, d), lambda i, j: (0, 4)),
            pl.BlockSpec((None, d, TN), lambda i, j: (layer, 0, j)),
            pl.BlockSpec((None, d, TN), lambda i, j: (layer, 0, j + nj)),
        ],
        out_specs=pl.BlockSpec((TM, TN), lambda i, j: (i, j)),
        out_shape=jax.ShapeDtypeStruct((m, hidden), BF16),
        scratch_shapes=[pltpu.VMEM((TM, d), BF16)],
        compiler_params=_cparams(("arbitrary", "arbitrary")),
        name="ffn_in",
    )(x, norm_w.reshape(-1, 1, d), mods, mods, w_in, w_in)


_NT = (((1,), (1,)), ((), ()))


def _ssd_in_body(x_ref, nw_ref, sh_ref, sc_ref, w_ref, wdt_ref, o_ref, dt_ref, h_ref, *, cfg):
    i, j = pl.program_id(0), pl.program_id(1)
    _norm_prologue(i, j, cfg, x_ref, x_ref, nw_ref, sc_ref, sh_ref, h_ref)
    h = h_ref[...]
    o_ref[...] = lax.dot_general(h, w_ref[...].astype(BF16), _NT, preferred_element_type=F32)

    @pl.when(j == 0)
    def _():
        dt_ref[...] = lax.dot_general(h, wdt_ref[...].astype(BF16), _NT, preferred_element_type=F32)


def _ssd_in(x, mods, layer, norm_w, w_in_t, n_main, cfg):
    m, d = x.shape
    r = mods.shape[0]
    n_dt = w_in_t.shape[0] - n_main
    nj = n_main // TN
    return pl.pallas_call(
        functools.partial(_ssd_in_body, cfg=cfg),
        grid=(m // TM, nj),
        in_specs=[
            pl.BlockSpec((TM, d), lambda i, j: (i, 0)),
            pl.BlockSpec((None, 1, d), lambda i, j: (layer, 0, 0)),
            pl.BlockSpec((r, d), lambda i, j: (0, 0)),
            pl.BlockSpec((r, d), lambda i, j: (0, 1)),
            pl.BlockSpec((TN, d), lambda i, j: (j, 0)),
            pl.BlockSpec((n_dt, d), lambda i, j: (n_main // n_dt, 0)),
        ],
        out_specs=[pl.BlockSpec((None, TM, TN), lambda i, j: (j, i, 0)), pl.BlockSpec((TM, n_dt), lambda i, j: (i, 0))],
        out_shape=[jax.ShapeDtypeStruct((nj, m, TN), F32), jax.ShapeDtypeStruct((m, n_dt), F32)],
        scratch_shapes=[pltpu.VMEM((TM, d), BF16)],
        compiler_params=_cparams(("arbitrary", "arbitrary")),
        name="ssd_in",
    )(x, norm_w.reshape(-1, 1, d), mods, mods, w_in_t, w_in_t)


TK = 512


def _resid_body(a_ref, *rest, cfg, nk, two_a):
    as_ref = None
    if two_a:
        as_ref, *rest = rest
    w_ref, *rest = rest
    if len(rest) == 4:
        xp_ref, xs_ref, g_ref, o_ref = rest
    else:
        xp_ref, g_ref, o_ref = rest
        xs_ref = xp_ref
    i, k = pl.program_id(0), pl.program_id(1)
    d = o_ref.shape[1]

    def accumulate(first):
        a = a_ref[...] if as_ref is None else jnp.where(i < cfg.n_ptiles, a_ref[...], as_ref[...])
        for c in range(d // TN):
            cols = slice(c * TN, (c + 1) * TN)
            part = jnp.dot(a, w_ref[:, cols].astype(BF16), preferred_element_type=F32)
            if first:
                o_ref[:, cols] = part
            else:
                o_ref[:, cols] += part

    @pl.when(k == 0)
    def _():
        accumulate(True)

    @pl.when(k > 0)
    def _():
        accumulate(False)

    @pl.when(k == nk - 1)
    def _():
        _per_seq(i, cfg, [xp_ref, o_ref], [xs_ref, o_ref], [g_ref], lambda xs, ms: xs[0] + ms[0] * xs[1], o_ref)


def _resid_matmul(a, w, w_layer, x_p, x_s, mods, gate_chunk, cfg, name, a_s=None):
    m = a.shape[-2] + (0 if a_s is None else a_s.shape[-2])
    d = x_p.shape[1]
    r = mods.shape[0]
    pb, sb = _prompt_block(cfg), _sample_block(cfg)
    if a.ndim == 2:
        nk = a.shape[1] // TK
        a_specs, a_args = [pl.BlockSpec((TM, TK), lambda i, k: (i, k))], [a]
    else:
        nk = a.shape[0]
        assert a.shape[2] == TK
        if a_s is None:
            a_specs, a_args = [pl.BlockSpec((None, TM, TK), lambda i, k: (k, i, 0))], [a]
        else:
            a_specs = [pl.BlockSpec((None, TM, TK), lambda i, k: (jnp.where(i < cfg.n_ptiles, k, 0), pb(i), 0)),
                       pl.BlockSpec((None, TM, TK), lambda i, k: (jnp.where(i < cfg.n_ptiles, 0, k), sb(i), 0))]
            a_args = [a, a_s]
    if x_s is None:
        x_specs, xs = [pl.BlockSpec((TM, d), lambda i, k: (i, 0))], [x_p]
    else:
        x_specs = [pl.BlockSpec((TM, d), lambda i, k: (pb(i), 0)),
                   pl.BlockSpec((TM, d), lambda i, k: (sb(i), 0), pipeline_mode=pl.Buffered(1))]
        xs = [x_p, x_s]
    return pl.pallas_call(
        functools.partial(_resid_body, cfg=cfg, nk=nk, two_a=a_s is not None),
        grid=(m // TM, nk),
        in_specs=a_specs + [pl.BlockSpec((None, TK, d), lambda i, k: (w_layer, k, 0))] + x_specs
        + [pl.BlockSpec((r, d), lambda i, k: (0, gate_chunk))],
        out_specs=pl.BlockSpec((TM, d), lambda i, k: (i, 0)),
        out_shape=jax.ShapeDtypeStruct((m, d), F32),
        compiler_params=_cparams(("arbitrary", "arbitrary")),
        name=name,
    )(*a_args, w, *xs, mods)


MIX_ROWS = 4 * GMLP_CHUNK


def _gmlp_mix_body(u_ref, v_ref, lnw_ref, lnb_ref, ws_ref, mask_ref, bias_ref, g_ref, vp_ref, vs_ref,
                   vn_ref, *, n_prompt_steps, steps_per_seq):
    t = pl.program_id(0)
    v = v_ref[...]
    xc = v - jnp.mean(v, axis=-1, keepdims=True)
    vn = xc * lax.rsqrt(jnp.mean(xc * xc, axis=-1, keepdims=True) + LN_EPS) * lnw_ref[...] + lnb_ref[...]
    vn_ref[...] = vn

    @pl.when(jnp.logical_and(t < n_prompt_steps, t % steps_per_seq == steps_per_seq - 1))
    def _():
        vp_ref[...] = vn[MIX_ROWS - GMLP_CHUNK:, :]

    @pl.when(t >= n_prompt_steps)
    def _():
        vs_ref[...] = vn

    mask = mask_ref[...]
    for g in range(GMLP_GROUPS):
        wb = (ws_ref[g] * mask).astype(BF16)
        cols = slice(g * GMLP_CHUNK, (g + 1) * GMLP_CHUNK)
        for c in range(MIX_ROWS // GMLP_CHUNK):
            rows = slice(c * GMLP_CHUNK, (c + 1) * GMLP_CHUNK)
            s = jnp.dot(wb, vn_ref[rows, cols].astype(BF16), preferred_element_type=F32) + bias_ref[:, cols]
            g_ref[rows, cols] = (u_ref[rows, cols] * s).astype(BF16)


def _gmlp_mix(u, v, ln_w, ln_b, wmix, mask, bias, n_prompt_rows, seq_len, n_prompt_seq):
    m, width = u.shape
    n_prompt_steps = n_prompt_rows // MIX_ROWS
    steps_per_seq = seq_len // MIX_ROWS
    n_sample_rows = m - n_prompt_rows

    def variant(t):
        return jnp.where(t < n_prompt_steps, 0, 1)

    return pl.pallas_call(
        functools.partial(_gmlp_mix_body, n_prompt_steps=n_prompt_steps, steps_per_seq=steps_per_seq),
        grid=(m // MIX_ROWS,),
        in_specs=[
            pl.BlockSpec((MIX_ROWS, width), lambda t: (t, 0)),
            pl.BlockSpec((MIX_ROWS, width), lambda t: (t, 0)),
            pl.BlockSpec((1, width), lambda t: (0, 0)),
            pl.BlockSpec((1, width), lambda t: (0, 0)),
            pl.BlockSpec((None, GMLP_GROUPS, GMLP_CHUNK, GMLP_CHUNK), lambda t: (variant(t), 0, 0, 0)),
            pl.BlockSpec((None, GMLP_CHUNK, GMLP_CHUNK), lambda t: (variant(t), 0, 0)),
            pl.BlockSpec((None, GMLP_CHUNK, width), lambda t: (variant(t), 0, 0)),
        ],
        out_specs=[
            pl.BlockSpec((MIX_ROWS, width), lambda t: (t, 0)),
            pl.BlockSpec((GMLP_CHUNK, width), lambda t: (jnp.minimum(t // steps_per_seq, n_prompt_seq - 1), 0)),
            pl.BlockSpec((MIX_ROWS, width), lambda t: (jnp.maximum(t - n_prompt_steps, 0), 0)),
        ],
        out_shape=[
            jax.ShapeDtypeStruct((m, width), BF16),
            jax.ShapeDtypeStruct((n_prompt_seq * GMLP_CHUNK, width), F32),
            jax.ShapeDtypeStruct((n_sample_rows, width), F32),
        ],
        scratch_shapes=[pltpu.VMEM((MIX_ROWS, width), F32)],
        compiler_params=_cparams(("arbitrary",)),
        name="gmlp_mix",
    )(u, v, ln_w.reshape(1, width), ln_b.reshape(1, width), wmix, mask, bias)


GROUP_HEADS = 8
GROUP_W = GROUP_HEADS * SSD_HEAD_DIM
T = SSD_CHUNK
BC_PER_BLOCK = GROUP_W // SSD_STATE


def _split3(x):
    hi = x.astype(BF16)
    r = x - hi.astype(F32)
    mid = r.astype(BF16)
    lo = (r - mid.astype(F32)).astype(BF16)
    return hi, mid, lo


def _dot_exact_rhs(m_b, pieces):
    return sum(jnp.dot(m_b, p, preferred_element_type=F32) for p in pieces)


def _dot_exact_lhs(pieces, e_b):
    return sum(jnp.dot(p, e_b, preferred_element_type=F32) for p in pieces)


def _ssd_group(xs, bm, cm, z, dt_x, cs_x, cs_end_x, cs_row, d_a, states, n_seg, dsk, nw, keep):
    seg = T // n_seg
    xdt = xs * dt_x
    ecs_x = jnp.exp(cs_x)
    dte_x = jnp.exp(cs_end_x - cs_x)
    xdt_b = xdt.astype(BF16)
    xd_t = (xdt * dte_x).T.astype(BF16)
    bmb = bm.astype(BF16)
    cmb = cm.astype(BF16)
    cb = lax.dot_general(cmb, bmb, _NT, preferred_element_type=F32)

    row = lax.broadcasted_iota(jnp.int32, (T, SSD_STATE), 0)
    y_offs, new_states = [], []
    for s in range(n_seg):
        st = states[s]
        c_seg = cmb if n_seg == 1 else cm[s * seg:(s + 1) * seg].astype(BF16)
        y_offs.append(lax.dot_general(c_seg, st.astype(BF16), _NT, preferred_element_type=F32))
        if n_seg == 1:
            b_seg = bmb
        else:
            b_seg = jnp.where(jnp.logical_and(row >= s * seg, row < (s + 1) * seg), bm, 0.0).astype(BF16)
        upd = jnp.dot(xd_t, b_seg, preferred_element_type=F32)
        decayed = jnp.concatenate(
            [st[h * SSD_HEAD_DIM:(h + 1) * SSD_HEAD_DIM, :] * d_a(s, h) for h in range(GROUP_HEADS)], axis=0)
        new_states.append(decayed + upd)
    y_off = jnp.concatenate(y_offs, axis=0) if n_seg > 1 else y_offs[0]

    lane = lax.broadcasted_iota(jnp.int32, (T, LANES), 1)
    lo_half = lane < SSD_HEAD_DIM
    ys = []
    for q in range(GROUP_HEADS // 2):
        v = cs_x[:, q * LANES:(q + 1) * LANES]
        r = pltpu.roll(v, SSD_HEAD_DIM, 1)
        cols = (jnp.where(lo_half, v, r), jnp.where(lo_half, r, v))
        ws = []
        for e in range(2):
            diff = cols[e] - cs_row(2 * q + e)
            ws.append((cb * jnp.exp(jnp.where(keep, diff, -jnp.inf))).astype(BF16))
        xp = xdt_b[:, q * LANES:(q + 1) * LANES]
        zero = jnp.zeros_like(xp)
        xpair = jnp.concatenate([jnp.where(lo_half, xp, zero), jnp.where(lo_half, zero, xp)], axis=0)
        ys.append(jnp.dot(jnp.concatenate(ws, axis=1), xpair, preferred_element_type=F32))
    y_diag = jnp.concatenate(ys, axis=1)

    y = y_diag + y_off * ecs_x + dsk * xs
    y = y * _silu(z)
    yn = y * lax.rsqrt(jnp.mean(y * y, axis=-1, keepdims=True) + NORM_EPS) * nw
    return yn.astype(BF16), new_states


def _causal_keep(seq_len):
    r = lax.broadcasted_iota(jnp.int32, (T, T), 0)
    c = lax.broadcasted_iota(jnp.int32, (T, T), 1)
    keep = c <= r
    if seq_len < T:
        keep = jnp.logical_and(keep, (r // seq_len) == (c // seq_len))
    return keep


def _ssd_prompt_body(z_ref, xs_ref, bc_ref, dt_ref, cw_ref, cbias_ref, dtb_ref, alog_ref, dsk_ref, nw_ref,
                     m_ref, e_ref, yn_ref, st_ref,
                     cbuf_ref, xc_ref, bcs_ref, dt3_ref, cs3_ref, cst_ref, *, c):
    n_x = xs_ref.shape[0]
    pad = SUBLANES
    taps = SSD_CONV - 1

    @pl.when(c == 0)
    def _():
        st_ref[...] = jnp.zeros_like(st_ref)
        cbuf_ref[:, 0:pad, :] = jnp.zeros((cbuf_ref.shape[0], pad, GROUP_W), F32)

    def conv(k, src):
        w = cw_ref[k]
        cbuf_ref[k, pad:pad + T, :] = src
        acc = cbias_ref[k] + src * w[taps:taps + 1, :]
        for kk in range(taps):
            acc = acc + cbuf_ref[k, pad - taps + kk:pad - taps + kk + T, :] * w[kk:kk + 1, :]
        cbuf_ref[k, pad - taps:pad, :] = cbuf_ref[k, pad + T - taps:pad + T, :]
        return _silu(acc)

    def conv_x(k, carry):
        xc_ref[k] = conv(k, xs_ref[k])
        return carry
    lax.fori_loop(0, n_x, conv_x, 0)

    def conv_bc(k, carry):
        out = conv(n_x + k, bc_ref[k])
        for gg in range(BC_PER_BLOCK):
            bcs_ref[k * BC_PER_BLOCK + gg] = out[:, gg * SSD_STATE:(gg + 1) * SSD_STATE]
        return carry
    lax.fori_loop(0, bc_ref.shape[0], conv_bc, 0)

    dt = jax.nn.softplus(dt_ref[...] + dtb_ref[...])
    a = dt * (-jnp.exp(alog_ref[...]))
    cs = _dot_exact_rhs(m_ref[...], _split3(a))
    cst_ref[...] = cs.T
    for p, (dt_p, cs_p) in enumerate(zip(_split3(dt), _split3(cs))):
        dt3_ref[p] = dt_p
        cs3_ref[p] = cs_p

    keep = _causal_keep(T)

    def spread(g):
        e_g = e_ref[g]
        return (_dot_exact_lhs([dt3_ref[p] for p in range(3)], e_g), _dot_exact_lhs([cs3_ref[p] for p in range(3)], e_g))

    def group(g, dt_x, cs_x):
        h0 = g * GROUP_HEADS
        rows = pl.ds(g * GROUP_W, GROUP_W)
        yn, new_states = _ssd_group(
            xc_ref[g], bcs_ref[g], bcs_ref[SSD_GROUPS + g], z_ref[g], dt_x, cs_x, cs_x[T - 1:T, :],
            lambda h: cst_ref[pl.ds(h0 + h, 1), :],
            lambda s, h: jnp.exp(cst_ref[pl.ds(h0 + h, 1), T - 1:T]),
            [st_ref[rows, :]], 1, dsk_ref[g], nw_ref[g], keep)
        yn_ref[g] = yn
        st_ref[rows, :] = new_states[0]

    spread_next = spread(0)
    for g in range(SSD_GROUPS):
        dt_x, cs_x = spread_next
        if g + 1 < SSD_GROUPS:
            spread_next = spread(g + 1)
        group(g, dt_x, cs_x)


def _ssd_sample_body(z_ref, xs_ref, b_ref, c_ref, px_ref, pb_ref, pc_ref, dt_ref, wx_ref, wb_ref, wc_ref,
                     bx_ref, bb_ref, bc_ref, dtb_ref, alog_ref, dsk_ref, nw_ref, m_ref, mseg_ref, e_ref,
                     st_in_ref, yn_ref, st_ref, cst_ref, cet_ref, *, seq_len, g):
    n_seg = T // seq_len

    def conv(x_ref, p_ref, w_ref, bias_ref):
        x = x_ref[...]
        width = x.shape[-1]
        taps = SSD_CONV - 1
        prev = [jnp.broadcast_to(p_ref[k][:, None, :], (n_seg, seq_len, width)).reshape(T, width) for k in range(taps)]
        tpos = lax.broadcasted_iota(jnp.int32, x.shape, 0) % seq_len
        acc = bias_ref[...] + x * w_ref[taps:SSD_CONV, :]
        for sh in range(1, SSD_CONV):
            hist = prev[taps - 1]
            for t in range(sh - 2, -1, -1):
                hist = jnp.where(tpos == t, prev[taps - sh + t], hist)
            shifted = jnp.where(tpos >= sh, pltpu.roll(x, sh, 0), hist)
            acc = acc + shifted * w_ref[taps - sh:SSD_CONV - sh, :]
        return _silu(acc)

    xs = conv(xs_ref, px_ref, wx_ref, bx_ref)
    bm = conv(b_ref, pb_ref, wb_ref, bb_ref)
    cm = conv(c_ref, pc_ref, wc_ref, bc_ref)

    dt = jax.nn.softplus(dt_ref[...] + dtb_ref[...])
    a3 = _split3(dt * (-jnp.exp(alog_ref[...])))
    cs = _dot_exact_rhs(m_ref[...], a3)
    cs_end = _dot_exact_rhs(mseg_ref[...], a3)
    cst_ref[...] = cs.T
    cet_ref[...] = cs_end.T
    e_g = e_ref[...]
    h0 = g * GROUP_HEADS
    yn, new_states = _ssd_group(
        xs, bm, cm, z_ref[...], _dot_exact_lhs(_split3(dt), e_g), _dot_exact_lhs(_split3(cs), e_g),
        _dot_exact_lhs(_split3(cs_end), e_g),
        lambda h: cst_ref[pl.ds(h0 + h, 1), :],
        lambda s, h: jnp.exp(cet_ref[pl.ds(h0 + h, 1), s * seq_len:s * seq_len + 1]),
        [st_in_ref[s] for s in range(n_seg)], n_seg, dsk_ref[...], nw_ref[...], _causal_keep(seq_len))
    yn_ref[...] = yn
    for s in range(n_seg):
        st_ref[s] = new_states[s]


def _ssd_masks(seq_len):
    r = jnp.arange(T)
    same = (r[:, None] // seq_len) == (r[None, :] // seq_len)
    return jnp.logical_and(same, r[None, :] <= r[:, None]).astype(BF16), same.astype(BF16)


N_PROMPT_IN, N_SAMPLE_IN = 12, 22


def _ssd_scan_body(*refs, n_chunks, seq_len_s):
    s = pl.program_id(0)
    p_in = refs[:N_PROMPT_IN]
    s_in = refs[N_PROMPT_IN:N_PROMPT_IN + N_SAMPLE_IN]
    yn_p, st_p, yn_s, st_s = refs[N_PROMPT_IN + N_SAMPLE_IN:N_PROMPT_IN + N_SAMPLE_IN + 4]
    scratch = refs[N_PROMPT_IN + N_SAMPLE_IN + 4:]
    _ssd_prompt_body(*p_in, yn_p, st_p, *scratch[:6], c=s % n_chunks)
    _ssd_sample_body(*s_in, yn_s, st_s, *scratch[6:], seq_len=seq_len_s, g=s % SSD_GROUPS)


def _ssd_scan(zx, dt_raw, conv_w, conv_b, dtb, alog, dsk, nw, state_s, conv_hist_s, n_pseq, seq_len_p, seq_len_s):
    heads = dt_raw.shape[1]
    inner = SSD_GROUPS * GROUP_W
    n_chunks = seq_len_p // T
    m_p = n_pseq * seq_len_p
    n_seg = T // seq_len_s
    n_sseq = state_s.shape[0]
    row0 = m_p // T
    n_steps = n_pseq * n_chunks
    assert n_steps == (n_sseq // n_seg) * SSD_GROUPS
    n_bc = 2 * SSD_GROUPS // BC_PER_BLOCK
    n_cblk = SSD_GROUPS + n_bc
    cw = conv_w.reshape(SSD_CONV, n_cblk, GROUP_W).transpose(1, 0, 2)
    cbias = conv_b.reshape(n_cblk, 1, GROUP_W)
    e_all = (jnp.arange(inner)[None, :] // SSD_HEAD_DIM == jnp.arange(heads)[:, None]).astype(BF16)
    e_grp = e_all.reshape(heads, SSD_GROUPS, GROUP_W).transpose(1, 0, 2)
    m_tril, _ = _ssd_masks(T)
    m_s, mseg_s = _ssd_masks(seq_len_s)
    xoff = SSD_GROUPS
    boff = 2 * SSD_GROUPS
    coff = boff + SSD_GROUPS // BC_PER_BLOCK
    cb0 = inner // SSD_STATE
    cc0 = cb0 + SSD_GROUPS

    full = lambda shape: pl.BlockSpec(shape, lambda s: (0,) * len(shape))
    bp = lambda s: s // n_chunks
    ts = lambda s: s // SSD_GROUPS
    gs = lambda s: s % SSD_GROUPS
    prompt_in = [
        pl.BlockSpec((SSD_GROUPS, T, GROUP_W), lambda s: (0, s, 0)),
        pl.BlockSpec((SSD_GROUPS, T, GROUP_W), lambda s: (1, s, 0)),
        pl.BlockSpec((n_bc, T, GROUP_W), lambda s: (2 * SSD_GROUPS // n_bc, s, 0)),
        pl.BlockSpec((T, heads), lambda s: (s, 0)),
        full((n_cblk, SSD_CONV, GROUP_W)), full((n_cblk, 1, GROUP_W)), full((1, heads)), full((1, heads)),
        full((SSD_GROUPS, 1, GROUP_W)), full((SSD_GROUPS, 1, GROUP_W)), full((T, T)),
        full((SSD_GROUPS, heads, GROUP_W)),
    ]
    per_g = lambda shape: pl.BlockSpec(shape, lambda s: (gs(s),) + (0,) * (len(shape) - 1))
    sample_in = [
        pl.BlockSpec((None, T, GROUP_W), lambda s: (gs(s), row0 + ts(s), 0)),
        pl.BlockSpec((None, T, GROUP_W), lambda s: (xoff + gs(s), row0 + ts(s), 0)),
        pl.BlockSpec((None, T, SSD_STATE),
                     lambda s: (boff + gs(s) // BC_PER_BLOCK, row0 + ts(s), gs(s) % BC_PER_BLOCK)),
        pl.BlockSpec((None, T, SSD_STATE),
                     lambda s: (coff + gs(s) // BC_PER_BLOCK, row0 + ts(s), gs(s) % BC_PER_BLOCK)),
        pl.BlockSpec((SSD_CONV - 1, n_seg, GROUP_W), lambda s: (0, ts(s), gs(s))),
        pl.BlockSpec((SSD_CONV - 1, n_seg, SSD_STATE), lambda s: (0, ts(s), cb0 + gs(s))),
        pl.BlockSpec((SSD_CONV - 1, n_seg, SSD_STATE), lambda s: (0, ts(s), cc0 + gs(s))),
        pl.BlockSpec((T, heads), lambda s: (row0 + ts(s), 0)),
        pl.BlockSpec((SSD_CONV, GROUP_W), lambda s: (0, gs(s))),
        pl.BlockSpec((SSD_CONV, SSD_STATE), lambda s: (0, cb0 + gs(s))),
        pl.BlockSpec((SSD_CONV, SSD_STATE), lambda s: (0, cc0 + gs(s))),
        pl.BlockSpec((1, GROUP_W), lambda s: (0, gs(s))),
        pl.BlockSpec((1, SSD_STATE), lambda s: (0, cb0 + gs(s))),
        pl.BlockSpec((1, SSD_STATE), lambda s: (0, cc0 + gs(s))),
        full((1, heads)), full((1, heads)),
        per_g((None, 1, GROUP_W)), per_g((None, 1, GROUP_W)),
        full((T, T)), full((T, T)),
        pl.BlockSpec((heads, GROUP_W), lambda s: (0, gs(s))),
        pl.BlockSpec((n_seg, GROUP_W, SSD_STATE), lambda s: (ts(s), gs(s), 0)),
    ]
    assert len(prompt_in) == N_PROMPT_IN and len(sample_in) == N_SAMPLE_IN
    return pl.pallas_call(
        functools.partial(_ssd_scan_body, n_chunks=n_chunks, seq_len_s=seq_len_s),
        grid=(n_steps,),
        in_specs=prompt_in + sample_in,
        out_specs=[pl.BlockSpec((SSD_GROUPS, T, GROUP_W), lambda s: (0, s, 0)),
                   pl.BlockSpec((None, inner, SSD_STATE), lambda s: (bp(s), 0, 0)),
                   pl.BlockSpec((None, T, GROUP_W), lambda s: (gs(s), ts(s), 0)),
                   pl.BlockSpec((n_seg, GROUP_W, SSD_STATE), lambda s: (ts(s), gs(s), 0))],
        out_shape=[jax.ShapeDtypeStruct((SSD_GROUPS, m_p, GROUP_W), BF16),
                   jax.ShapeDtypeStruct((n_pseq, inner, SSD_STATE), F32),
                   jax.ShapeDtypeStruct((SSD_GROUPS, n_sseq * seq_len_s, GROUP_W), BF16),
                   jax.ShapeDtypeStruct(state_s.shape, F32)],
        scratch_shapes=[pltpu.VMEM((n_cblk, SUBLANES + T, GROUP_W), F32),
                        pltpu.VMEM((SSD_GROUPS, T, GROUP_W), F32),
                        pltpu.VMEM((2 * SSD_GROUPS, T, SSD_STATE), F32),
                        pltpu.VMEM((3, T, heads), BF16),
                        pltpu.VMEM((3, T, heads), BF16),
                        pltpu.VMEM((heads, T), F32),
                        pltpu.VMEM((heads, T), F32), pltpu.VMEM((heads, T), F32)],
        compiler_params=_cparams(("arbitrary",)),
        name="ssd_scan",
    )(zx, zx, zx, dt_raw, cw, cbias, dtb, alog, dsk, nw, m_tril, e_grp,
      zx, zx, zx, zx, conv_hist_s, conv_hist_s, conv_hist_s, dt_raw, conv_w, conv_w, conv_w, conv_b, conv_b, conv_b,
      dtb, alog, dsk, nw, m_s, mseg_s, e_all, state_s)


def _final_body(x_ref, w_ref, yp_ref, ys_ref, *, n_ptiles):
    i = pl.program_id(0)
    x = x_ref[...]
    y = x * lax.rsqrt(jnp.mean(x * x, axis=-1, keepdims=True) + NORM_EPS) * w_ref[...]

    @pl.when(i < n_ptiles)
    def _():
        yp_ref[...] = y

    @pl.when(i >= n_ptiles)
    def _():
        ys_ref[...] = y


def _final_norm(x, w, n_prompt_rows):
    m, d = x.shape
    tm = TM
    n_ptiles = n_prompt_rows // tm
    return pl.pallas_call(
        functools.partial(_final_body, n_ptiles=n_ptiles),
        grid=(m // tm,),
        in_specs=[pl.BlockSpec((tm, d), lambda i: (i, 0)), pl.BlockSpec((1, d), lambda i: (0, 0))],
        out_specs=[pl.BlockSpec((tm, d), lambda i: (jnp.minimum(i, n_ptiles - 1), 0)),
                   pl.BlockSpec((tm, d), lambda i: (jnp.maximum(i - n_ptiles, 0), 0))],
        out_shape=[jax.ShapeDtypeStruct((n_prompt_rows, d), F32), jax.ShapeDtypeStruct((m - n_prompt_rows, d), F32)],
        compiler_params=_cparams(("arbitrary",)),
        name="final_norm",
    )(x, w.reshape(1, d))


def kernel(x_prompt, x_sample, c_prompt, c_sample, state_ssm, state_conv, mod_w, mod_b, norm_mix_w, norm_ffn_w,
           a_w_in, a_b_in, a_ln_w, a_ln_b, a_w_s, a_b_s, a_w_out,
           b_w_in, b_conv_w, b_conv_b, b_dt_bias, b_a_log, b_d, b_norm_w, b_w_out,
           f_w_in, f_w_out, final_norm_w):
    bp, lp, d = x_prompt.shape
    bs, ls, _ = x_sample.shape
    depth = mod_w.shape[0]
    n_prompt = bp * lp
    n_sample = bs * ls
    m_all = n_prompt + n_sample
    assert lp % TM == 0 and n_sample % TM == 0 and TM % ls == 0 and bp <= SUBLANES
    assert ls >= SSD_CONV - 1 and T % ls == 0 and lp % T == 0 and GMLP_CHUNK % ls == 0 and lp % MIX_ROWS == 0
    assert depth == 2 and a_w_in.shape[0] == 1 and b_w_in.shape[0] == 1
    cfg = Cfg(n_ptiles=n_prompt // TM, tiles_per_seq=lp // TM, seq_len_s=ls, srow0=SUBLANES)

    x_p = x_prompt.reshape(n_prompt, d)
    x_s = x_sample.reshape(n_sample, d)
    c_all = jnp.concatenate([c_prompt, jnp.zeros((SUBLANES - bp, d), F32), c_sample], axis=0)
    mods = _mod_table(c_all, mod_w, mod_b, 0)

    u, v = _gmlp_in(x_p, x_s, mods, 0, norm_mix_w, a_w_in, a_b_in, 0, cfg)
    r = jnp.arange(GMLP_CHUNK)
    tril = r[None, :] <= r[:, None]
    mask = jnp.stack([tril, jnp.logical_and(tril, (r[:, None] // ls) == (r[None, :] // ls))]).astype(F32)
    rep = GMLP_CHUNK // ls
    wmix = jnp.stack([a_w_s[0], jnp.tile(a_w_s[0, :, :ls, :ls], (1, rep, rep))])
    width = a_w_in.shape[-1] // 2
    bias = jnp.stack([jnp.repeat(a_b_s[0].T, width // GMLP_GROUPS, axis=1),
                      jnp.repeat(jnp.tile(a_b_s[0, :, :ls].T, (rep, 1)), width // GMLP_GROUPS, axis=1)])
    gated, v_p, v_s = _gmlp_mix(u, v, a_ln_w[0], a_ln_b[0], wmix, mask, bias, n_prompt, lp, bp)
    x = _resid_matmul(gated, a_w_out, 0, x_p, x_s, mods, 2, cfg, "gmlp_out")
    act = _ffn_in(x, mods, 0, norm_ffn_w, f_w_in, cfg)
    x = _resid_matmul(act, f_w_out, 0, x, None, mods, 5, cfg, "ffn_out0")
    mods = _mod_table(c_all, mod_w, mod_b, 1)

    inner = b_w_out.shape[1]
    conv_dim = b_conv_w.shape[-1]
    n_main = inner + conv_dim
    heads = inner // SSD_HEAD_DIM
    zx, dt_raw = _ssd_in(x, mods, 1, norm_mix_w, jnp.swapaxes(b_w_in, 1, 2)[0], n_main, cfg)
    conv_hist = jnp.swapaxes(state_conv[0], 0, 1)
    ssd_params = (b_conv_w[0], b_conv_b, b_dt_bias, b_a_log,
                  jnp.repeat(b_d[0], SSD_HEAD_DIM).reshape(SSD_GROUPS, 1, GROUP_W),
                  b_norm_w.reshape(SSD_GROUPS, 1, GROUP_W))
    state_s = state_ssm[0].reshape(bs, inner, SSD_STATE)
    yn_p, ssm_p, yn_s, ssm_s = _ssd_scan(zx, dt_raw, *ssd_params, state_s, conv_hist, bp, lp, ls)
    x = _resid_matmul(yn_p, b_w_out, 0, x, None, mods, 2, cfg, "ssd_out", a_s=yn_s)
    act = _ffn_in(x, mods, 1, norm_ffn_w, f_w_in, cfg)
    x = _resid_matmul(act, f_w_out, 1, x, None, mods, 5, cfg, "ffn_out1")

    y_p, y_s = _final_norm(x, final_norm_w, n_prompt)

    zx4 = zx.reshape(zx.shape[0], m_all // ls, ls, TN)

    def tails(groups):
        t = groups[SSD_GROUPS:, :, ls - (SSD_CONV - 1):, :]
        return jnp.moveaxis(t, 0, 2).reshape(t.shape[1], SSD_CONV - 1, conv_dim)
    conv_p = tails(zx4[:, lp // ls - 1:n_prompt // ls:lp // ls])
    conv_s = tails(zx4[:, n_prompt // ls:])
    return (y_p.reshape(bp, lp, d), y_s.reshape(bs, ls, d),
            v_p.reshape(1, bp, GMLP_CHUNK, width), v_s.reshape(1, bs, ls, width),
            ssm_p.reshape(1, bp, heads, SSD_HEAD_DIM, SSD_STATE), ssm_s.reshape(1, bs, heads, SSD_HEAD_DIM, SSD_STATE),
            conv_p[None], conv_s[None])
```

```python
import functools
import math
from typing import NamedTuple

import jax
import jax.numpy as jnp
from jax import lax
from jax.experimental import pallas as pl
from jax.experimental.pallas import tpu as pltpu

F32 = jnp.float32
BF16 = jnp.bfloat16

NORM_EPS = 1e-6
LN_EPS = 1e-5

GMLP_GROUPS = 16
GMLP_CHUNK = 128
SSD_HEAD_DIM = 64
SSD_STATE = 128
SSD_GROUPS = 8
SSD_CONV = 4
SSD_CHUNK = 128

SUBLANES = 8
LANES = 128
VMEM_LIMIT_BYTES = 56 * 1024 * 1024

TM = 1024
TN = 512
TN_NARROW = 256
ROW_CHUNK = 256
SUB_ROWS = 16


class Cfg(NamedTuple):
    n_ptiles: int
    tiles_per_seq: int
    seq_len_s: int
    srow0: int


def _cparams(sem):
    return pltpu.CompilerParams(dimension_semantics=sem, vmem_limit_bytes=VMEM_LIMIT_BYTES)


def _silu(x):
    return x / (1.0 + jnp.exp(-x))


def _gelu(x):
    return 0.5 * x * (1.0 + lax.erf(x * (1.0 / math.sqrt(2.0))))


def _rms_mod(x, w, scale, shift):
    y = x * lax.rsqrt(jnp.mean(x * x, axis=-1, keepdims=True) + NORM_EPS)
    return (y * w) * (1.0 + scale) + shift


def _piece_rows(c, q, piece):
    return pl.ds(pl.multiple_of(c * ROW_CHUNK + q * piece, piece), piece)


def _sample_rows(i, cfg, x_refs, mod_refs, fn, o_ref, piece=ROW_CHUNK):
    rows = o_ref.shape[0]
    nseq = ROW_CHUNK // cfg.seq_len_s
    sub_seq = piece // cfg.seq_len_s
    row0 = cfg.srow0 + (i - cfg.n_ptiles) * (rows // cfg.seq_len_s)

    def body(c, carry):
        r0 = pl.multiple_of(row0 + c * nseq, SUBLANES)
        ms_chunk = [m[pl.ds(r0, nseq), :] for m in mod_refs]
        for q in range(ROW_CHUNK // piece):
            rs = _piece_rows(c, q, piece)
            ms = [mc[q * sub_seq:(q + 1) * sub_seq][:, None, :] for mc in ms_chunk]
            x3 = [x[rs, :].reshape(sub_seq, cfg.seq_len_s, x.shape[-1]) for x in x_refs]
            o_ref[rs, :] = fn(x3, ms).reshape(piece, o_ref.shape[-1]).astype(o_ref.dtype)
        return carry
    lax.fori_loop(0, rows // ROW_CHUNK, body, 0)


def _per_seq(i, cfg, xp_refs, xs_refs, mod_refs, fn, o_ref, piece=ROW_CHUNK):
    @pl.when(i < cfg.n_ptiles)
    def _():
        s = i // cfg.tiles_per_seq
        ms = [m[pl.ds(s, 1), :] for m in mod_refs]

        def body(c, carry):
            for q in range(ROW_CHUNK // piece):
                rs = _piece_rows(c, q, piece)
                o_ref[rs, :] = fn([x[rs, :] for x in xp_refs], ms).astype(o_ref.dtype)
            return carry
        lax.fori_loop(0, o_ref.shape[0] // ROW_CHUNK, body, 0)

    @pl.when(i >= cfg.n_ptiles)
    def _():
        _sample_rows(i, cfg, xs_refs, mod_refs, fn, o_ref, piece)


def _prompt_block(cfg):
    return lambda i: jnp.minimum(i, cfg.n_ptiles - 1)


def _sample_block(cfg):
    return lambda i: jnp.maximum(i - cfg.n_ptiles, 0)


def _mod_body(c_ref, w_ref, b_ref, o_ref):
    sc = _silu(c_ref[...]).astype(BF16)
    o_ref[...] = jnp.dot(sc, w_ref[...].astype(BF16), preferred_element_type=F32) + b_ref[...]


MOD_CHUNKS = 6


def _mod_table(c_all, mod_w, mod_b):
    depth, d, n = mod_w.shape
    assert n == MOD_CHUNKS * d
    r = c_all.shape[0]
    tn = 1024
    nj = n // tn
    return pl.pallas_call(
        _mod_body,
        grid=(depth * nj,),
        in_specs=[
            pl.BlockSpec((r, d), lambda j: (0, 0)),
            pl.BlockSpec((None, d, tn), lambda j: (j // nj, 0, j % nj)),
            pl.BlockSpec((None, 1, tn), lambda j: (j // nj, 0, j % nj)),
        ],
        out_specs=pl.BlockSpec((r, tn), lambda j: (0, j)),
        out_shape=jax.ShapeDtypeStruct((r, depth * n), F32),
        compiler_params=_cparams(("arbitrary",)),
        name="mod_table",
    )(c_all, mod_w, mod_b.reshape(depth, 1, n))


def _norm_prologue(i, j, cfg, xp_ref, xs_ref, nw_ref, sc_ref, sh_ref, h_ref):
    @pl.when(j == 0)
    def _():
        _per_seq(i, cfg, [xp_ref], [xs_ref], [sc_ref, sh_ref],
                 lambda xs, ms: _rms_mod(xs[0], nw_ref[...], ms[0], ms[1]), h_ref, piece=SUB_ROWS)


def _gmlp_in_body(xp_ref, xs_ref, nw_ref, sh_ref, sc_ref, wu_ref, wv_ref, bu_ref, bv_ref, u_ref, v_ref, h_ref, *, cfg):
    i, j = pl.program_id(0), pl.program_id(1)
    _norm_prologue(i, j, cfg, xp_ref, xs_ref, nw_ref, sc_ref, sh_ref, h_ref)
    h = h_ref[...]
    u_ref[...] = _gelu(jnp.dot(h, wu_ref[...].astype(BF16), preferred_element_type=F32) + bu_ref[...])
    v_ref[...] = _gelu(jnp.dot(h, wv_ref[...].astype(BF16), preferred_element_type=F32) + bv_ref[...])


def _gmlp_in(x_p, x_s, mods, layer, norm_w, w_in, b_in, j_layer, cfg):
    d = x_p.shape[1]
    m = x_p.shape[0] + x_s.shape[0]
    width = w_in.shape[-1] // 2
    tn = TN_NARROW
    nj = width // tn
    r = mods.shape[0]
    pb, sb = _prompt_block(cfg), _sample_block(cfg)
    b2 = b_in.reshape(b_in.shape[0], 1, -1)
    return pl.pallas_call(
        functools.partial(_gmlp_in_body, cfg=cfg),
        grid=(m // TM, nj),
        in_specs=[
            pl.BlockSpec((TM, d), lambda i, j: (pb(i), 0)),
            pl.BlockSpec((TM, d), lambda i, j: (sb(i), 0), pipeline_mode=pl.Buffered(1)),
            pl.BlockSpec((None, 1, d), lambda i, j: (layer, 0, 0)),
            pl.BlockSpec((r, d), lambda i, j: (0, MOD_CHUNKS * layer + 0)),
            pl.BlockSpec((r, d), lambda i, j: (0, MOD_CHUNKS * layer + 1)),
            pl.BlockSpec((None, d, tn), lambda i, j: (j_layer, 0, j)),
            pl.BlockSpec((None, d, tn), lambda i, j: (j_layer, 0, j + nj)),
            pl.BlockSpec((None, 1, tn), lambda i, j: (j_layer, 0, j)),
            pl.BlockSpec((None, 1, tn), lambda i, j: (j_layer, 0, j + nj)),
        ],
        out_specs=[pl.BlockSpec((TM, tn), lambda i, j: (i, j)), pl.BlockSpec((TM, tn), lambda i, j: (i, j))],
        out_shape=[jax.ShapeDtypeStruct((m, width), F32), jax.ShapeDtypeStruct((m, width), F32)],
        scratch_shapes=[pltpu.VMEM((TM, d), BF16)],
        compiler_params=_cparams(("arbitrary", "arbitrary")),
        name="gmlp_in",
    )(x_p, x_s, norm_w.reshape(-1, 1, d), mods, mods, w_in, w_in, b2, b2)


def _ffn_in_body(x_ref, nw_ref, sh_ref, sc_ref, wg_ref, wu_ref, a_ref, h_ref, *, cfg):
    i, j = pl.program_id(0), pl.program_id(1)
    _norm_prologue(i, j, cfg, x_ref, x_ref, nw_ref, sc_ref, sh_ref, h_ref)
    h = h_ref[...]
    gate = jnp.dot(h, wg_ref[...].astype(BF16), preferred_element_type=F32)
    up = jnp.dot(h, wu_ref[...].astype(BF16), preferred_element_type=F32)
    a_ref[...] = (_silu(gate) * up).astype(BF16)


def _ffn_in(x, mods, layer, norm_w, w_in, cfg):
    m, d = x.shape
    hidden = w_in.shape[-1] // 2
    nj = hidden // TN
    r = mods.shape[0]
    return pl.pallas_call(
        functools.partial(_ffn_in_body, cfg=cfg),
        grid=(m // TM, nj),
        in_specs=[
            pl.BlockSpec((TM, d), lambda i, j: (i, 0)),
            pl.BlockSpec((None, 1, d), lambda i, j: (layer, 0, 0)),
            pl.BlockSpec((r, d), lambda i, j: (0, MOD_CHUNKS * layer + 3)),
            pl.BlockSpec((r, d), lambda i, j: (0, MOD_CHUNKS * layer + 4)),
            pl.BlockSpec((None, d, TN), lambda i, j: (layer, 0, j)),
            pl.BlockSpec((None, d, TN), lambda i, j: (layer, 0, j + nj)),
        ],
        out_specs=pl.BlockSpec((TM, TN), lambda i, j: (i, j)),
        out_shape=jax.ShapeDtypeStruct((m, hidden), BF16),
        scratch_shapes=[pltpu.VMEM((TM, d), BF16)],
        compiler_params=_cparams(("arbitrary", "arbitrary")),
        name="ffn_in",
    )(x, norm_w.reshape(-1, 1, d), mods, mods, w_in, w_in)


_NT = (((1,), (1,)), ((), ()))


def _ssd_in_body(x_ref, nw_ref, sh_ref, sc_ref, w_ref, wdt_ref, o_ref, dt_ref, h_ref, *, cfg):
    i, j = pl.program_id(0), pl.program_id(1)
    _norm_prologue(i, j, cfg, x_ref, x_ref, nw_ref, sc_ref, sh_ref, h_ref)
    h = h_ref[...]
    o_ref[...] = lax.dot_general(h, w_ref[...].astype(BF16), _NT, preferred_element_type=F32)

    @pl.when(j == 0)
    def _():
        dt_ref[...] = lax.dot_general(h, wdt_ref[...].astype(BF16), _NT, preferred_element_type=F32)


def _ssd_in(x, mods, layer, norm_w, w_in_t, n_main, cfg):
    m, d = x.shape
    r = mods.shape[0]
    n_dt = w_in_t.shape[0] - n_main
    nj = n_main // TN
    return pl.pallas_call(
        functools.partial(_ssd_in_body, cfg=cfg),
        grid=(m // TM, nj),
        in_specs=[
            pl.BlockSpec((TM, d), lambda i, j: (i, 0)),
            pl.BlockSpec((None, 1, d), lambda i, j: (layer, 0, 0)),
            pl.BlockSpec((r, d), lambda i, j: (0, MOD_CHUNKS * layer + 0)),
            pl.BlockSpec((r, d), lambda i, j: (0, MOD_CHUNKS * layer + 1)),
            pl.BlockSpec((TN, d), lambda i, j: (j, 0)),
            pl.BlockSpec((n_dt, d), lambda i, j: (n_main // n_dt, 0)),
        ],
        out_specs=[pl.BlockSpec((None, TM, TN), lambda i, j: (j, i, 0)), pl.BlockSpec((TM, n_dt), lambda i, j: (i, 0))],
        out_shape=[jax.ShapeDtypeStruct((nj, m, TN), F32), jax.ShapeDtypeStruct((m, n_dt), F32)],
        scratch_shapes=[pltpu.VMEM((TM, d), BF16)],
        compiler_params=_cparams(("arbitrary", "arbitrary")),
        name="ssd_in",
    )(x, norm_w.reshape(-1, 1, d), mods, mods, w_in_t, w_in_t)


TK = 512


def _resid_body(a_ref, *rest, cfg, nk, two_a):
    as_ref = None
    if two_a:
        as_ref, *rest = rest
    w_ref, *rest = rest
    if len(rest) == 4:
        xp_ref, xs_ref, g_ref, o_ref = rest
    else:
        xp_ref, g_ref, o_ref = rest
        xs_ref = xp_ref
    i, k = pl.program_id(0), pl.program_id(1)
    d = o_ref.shape[1]

    def accumulate(first):
        a = a_ref[...] if as_ref is None else jnp.where(i < cfg.n_ptiles, a_ref[...], as_ref[...])
        for c in range(d // TN):
            cols = slice(c * TN, (c + 1) * TN)
            part = jnp.dot(a, w_ref[:, cols].astype(BF16), preferred_element_type=F32)
            if first:
                o_ref[:, cols] = part
            else:
                o_ref[:, cols] += part

    @pl.when(k == 0)
    def _():
        accumulate(True)

    @pl.when(k > 0)
    def _():
        accumulate(False)

    @pl.when(k == nk - 1)
    def _():
        _per_seq(i, cfg, [xp_ref, o_ref], [xs_ref, o_ref], [g_ref], lambda xs, ms: xs[0] + ms[0] * xs[1], o_ref)


def _resid_matmul(a, w, w_layer, x_p, x_s, mods, gate_chunk, cfg, name, a_s=None):
    m = a.shape[-2] + (0 if a_s is None else a_s.shape[-2])
    d = x_p.shape[1]
    r = mods.shape[0]
    pb, sb = _prompt_block(cfg), _sample_block(cfg)
    if a.ndim == 2:
        nk = a.shape[1] // TK
        a_specs, a_args = [pl.BlockSpec((TM, TK), lambda i, k: (i, k))], [a]
    else:
        nk = a.shape[0]
        assert a.shape[2] == TK
        if a_s is None:
            a_specs, a_args = [pl.BlockSpec((None, TM, TK), lambda i, k: (k, i, 0))], [a]
        else:
            a_specs = [pl.BlockSpec((None, TM, TK), lambda i, k: (jnp.where(i < cfg.n_ptiles, k, 0), pb(i), 0)),
                       pl.BlockSpec((None, TM, TK), lambda i, k: (jnp.where(i < cfg.n_ptiles, 0, k), sb(i), 0))]
            a_args = [a, a_s]
    if x_s is None:
        x_specs, xs = [pl.BlockSpec((TM, d), lambda i, k: (i, 0))], [x_p]
    else:
        x_specs = [pl.BlockSpec((TM, d), lambda i, k: (pb(i), 0)),
                   pl.BlockSpec((TM, d), lambda i, k: (sb(i), 0), pipeline_mode=pl.Buffered(1))]
        xs = [x_p, x_s]
    return pl.pallas_call(
        functools.partial(_resid_body, cfg=cfg, nk=nk, two_a=a_s is not None),
        grid=(m // TM, nk),
        in_specs=a_specs + [pl.BlockSpec((None, TK, d), lambda i, k: (w_layer, k, 0))] + x_specs
        + [pl.BlockSpec((r, d), lambda i, k: (0, gate_chunk))],
        out_specs=pl.BlockSpec((TM, d), lambda i, k: (i, 0)),
        out_shape=jax.ShapeDtypeStruct((m, d), F32),
        compiler_params=_cparams(("arbitrary", "arbitrary")),
        name=name,
    )(*a_args, w, *xs, mods)


MIX_ROWS = 4 * GMLP_CHUNK


def _gmlp_mix_body(u_ref, v_ref, lnw_ref, lnb_ref, ws_ref, mask_ref, bias_ref, g_ref, vp_ref, vs_ref,
                   vn_ref, *, n_prompt_steps, steps_per_seq):
    t = pl.program_id(0)
    v = v_ref[...]
    xc = v - jnp.mean(v, axis=-1, keepdims=True)
    vn = xc * lax.rsqrt(jnp.mean(xc * xc, axis=-1, keepdims=True) + LN_EPS) * lnw_ref[...] + lnb_ref[...]
    vn_ref[...] = vn

    @pl.when(jnp.logical_and(t < n_prompt_steps, t % steps_per_seq == steps_per_seq - 1))
    def _():
        vp_ref[...] = vn[MIX_ROWS - GMLP_CHUNK:, :]

    @pl.when(t >= n_prompt_steps)
    def _():
        vs_ref[...] = vn

    mask = mask_ref[...]
    for g in range(GMLP_GROUPS):
        wb = (ws_ref[g] * mask).astype(BF16)
        cols = slice(g * GMLP_CHUNK, (g + 1) * GMLP_CHUNK)
        for c in range(MIX_ROWS // GMLP_CHUNK):
            rows = slice(c * GMLP_CHUNK, (c + 1) * GMLP_CHUNK)
            s = jnp.dot(wb, vn_ref[rows, cols].astype(BF16), preferred_element_type=F32) + bias_ref[:, cols]
            g_ref[rows, cols] = (u_ref[rows, cols] * s).astype(BF16)


def _gmlp_mix(u, v, ln_w, ln_b, wmix, mask, bias, n_prompt_rows, seq_len, n_prompt_seq):
    m, width = u.shape
    n_prompt_steps = n_prompt_rows // MIX_ROWS
    steps_per_seq = seq_len // MIX_ROWS
    n_sample_rows = m - n_prompt_rows

    def variant(t):
        return jnp.where(t < n_prompt_steps, 0, 1)

    return pl.pallas_call(
        functools.partial(_gmlp_mix_body, n_prompt_steps=n_prompt_steps, steps_per_seq=steps_per_seq),
        grid=(m // MIX_ROWS,),
        in_specs=[
            pl.BlockSpec((MIX_ROWS, width), lambda t: (t, 0)),
            pl.BlockSpec((MIX_ROWS, width), lambda t: (t, 0)),
            pl.BlockSpec((1, width), lambda t: (0, 0)),
            pl.BlockSpec((1, width), lambda t: (0, 0)),
            pl.BlockSpec((None, GMLP_GROUPS, GMLP_CHUNK, GMLP_CHUNK), lambda t: (variant(t), 0, 0, 0)),
            pl.BlockSpec((None, GMLP_CHUNK, GMLP_CHUNK), lambda t: (variant(t), 0, 0)),
            pl.BlockSpec((None, GMLP_CHUNK, width), lambda t: (variant(t), 0, 0)),
        ],
        out_specs=[
            pl.BlockSpec((MIX_ROWS, width), lambda t: (t, 0)),
            pl.BlockSpec((GMLP_CHUNK, width), lambda t: (jnp.minimum(t // steps_per_seq, n_prompt_seq - 1), 0)),
            pl.BlockSpec((MIX_ROWS, width), lambda t: (jnp.maximum(t - n_prompt_steps, 0), 0)),
        ],
        out_shape=[
            jax.ShapeDtypeStruct((m, width), BF16),
            jax.ShapeDtypeStruct((n_prompt_seq * GMLP_CHUNK, width), F32),
            jax.ShapeDtypeStruct((n_sample_rows, width), F32),
        ],
        scratch_shapes=[pltpu.VMEM((MIX_ROWS, width), F32)],
        compiler_params=_cparams(("arbitrary",)),
        name="gmlp_mix",
    )(u, v, ln_w.reshape(1, width), ln_b.reshape(1, width), wmix, mask, bias)


GROUP_HEADS = 8
GROUP_W = GROUP_HEADS * SSD_HEAD_DIM
T = SSD_CHUNK
BC_PER_BLOCK = GROUP_W // SSD_STATE


def _split3(x):
    hi = x.astype(BF16)
    r = x - hi.astype(F32)
    mid = r.astype(BF16)
    lo = (r - mid.astype(F32)).astype(BF16)
    return hi, mid, lo


def _dot_exact_rhs(m_b, pieces):
    return sum(jnp.dot(m_b, p, preferred_element_type=F32) for p in pieces)


def _dot_exact_lhs(pieces, e_b):
    return sum(jnp.dot(p, e_b, preferred_element_type=F32) for p in pieces)


def _ssd_group(xs, bm, cm, z, dt_x, cs_x, cs_end_x, cs_row, d_a, states, n_seg, dsk, nw, keep):
    seg = T // n_seg
    xdt = xs * dt_x
    ecs_x = jnp.exp(cs_x)
    dte_x = jnp.exp(cs_end_x - cs_x)
    xdt_b = xdt.astype(BF16)
    xd_t = (xdt * dte_x).T.astype(BF16)
    bmb = bm.astype(BF16)
    cmb = cm.astype(BF16)
    cb = lax.dot_general(cmb, bmb, _NT, preferred_element_type=F32)

    row = lax.broadcasted_iota(jnp.int32, (T, SSD_STATE), 0)
    y_offs, new_states = [], []
    for s in range(n_seg):
        st = states[s]
        c_seg = cmb if n_seg == 1 else cm[s * seg:(s + 1) * seg].astype(BF16)
        y_offs.append(lax.dot_general(c_seg, st.astype(BF16), _NT, preferred_element_type=F32))
        if n_seg == 1:
            b_seg = bmb
        else:
            b_seg = jnp.where(jnp.logical_and(row >= s * seg, row < (s + 1) * seg), bm, 0.0).astype(BF16)
        upd = jnp.dot(xd_t, b_seg, preferred_element_type=F32)
        decayed = jnp.concatenate(
            [st[h * SSD_HEAD_DIM:(h + 1) * SSD_HEAD_DIM, :] * d_a(s, h) for h in range(GROUP_HEADS)], axis=0)
        new_states.append(decayed + upd)
    y_off = jnp.concatenate(y_offs, axis=0) if n_seg > 1 else y_offs[0]

    lane = lax.broadcasted_iota(jnp.int32, (T, LANES), 1)
    lo_half = lane < SSD_HEAD_DIM
    ys = []
    for q in range(GROUP_HEADS // 2):
        v = cs_x[:, q * LANES:(q + 1) * LANES]
        r = pltpu.roll(v, SSD_HEAD_DIM, 1)
        cols = (jnp.where(lo_half, v, r), jnp.where(lo_half, r, v))
        ws = []
        for e in range(2):
            diff = cols[e] - cs_row(2 * q + e)
            ws.append((cb * jnp.exp(jnp.where(keep, diff, -jnp.inf))).astype(BF16))
        xp = xdt_b[:, q * LANES:(q + 1) * LANES]
        zero = jnp.zeros_like(xp)
        xpair = jnp.concatenate([jnp.where(lo_half, xp, zero), jnp.where(lo_half, zero, xp)], axis=0)
        ys.append(jnp.dot(jnp.concatenate(ws, axis=1), xpair, preferred_element_type=F32))
    y_diag = jnp.concatenate(ys, axis=1)

    y = y_diag + y_off * ecs_x + dsk * xs
    y = y * _silu(z)
    yn = y * lax.rsqrt(jnp.mean(y * y, axis=-1, keepdims=True) + NORM_EPS) * nw
    return yn.astype(BF16), new_states


def _causal_keep(seq_len):
    r = lax.broadcasted_iota(jnp.int32, (T, T), 0)
    c = lax.broadcasted_iota(jnp.int32, (T, T), 1)
    keep = c <= r
    if seq_len < T:
        keep = jnp.logical_and(keep, (r // seq_len) == (c // seq_len))
    return keep


def _ssd_prompt_body(z_ref, xs_ref, bc_ref, dt_ref, cw_ref, cbias_ref, dtb_ref, alog_ref, dsk_ref, nw_ref,
                     m_ref, e_ref, yn_ref, st_ref,
                     cbuf_ref, xc_ref, bcs_ref, dt3_ref, cs3_ref, cst_ref, *, c):
    n_x = xs_ref.shape[0]
    pad = SUBLANES
    taps = SSD_CONV - 1

    @pl.when(c == 0)
    def _():
        st_ref[...] = jnp.zeros_like(st_ref)
        cbuf_ref[:, 0:pad, :] = jnp.zeros((cbuf_ref.shape[0], pad, GROUP_W), F32)

    def conv(k, src):
        w = cw_ref[k]
        cbuf_ref[k, pad:pad + T, :] = src
        acc = cbias_ref[k] + src * w[taps:taps + 1, :]
        for kk in range(taps):
            acc = acc + cbuf_ref[k, pad - taps + kk:pad - taps + kk + T, :] * w[kk:kk + 1, :]
        cbuf_ref[k, pad - taps:pad, :] = cbuf_ref[k, pad + T - taps:pad + T, :]
        return _silu(acc)

    def conv_x(k, carry):
        xc_ref[k] = conv(k, xs_ref[k])
        return carry
    lax.fori_loop(0, n_x, conv_x, 0)

    def conv_bc(k, carry):
        out = conv(n_x + k, bc_ref[k])
        for gg in range(BC_PER_BLOCK):
            bcs_ref[k * BC_PER_BLOCK + gg] = out[:, gg * SSD_STATE:(gg + 1) * SSD_STATE]
        return carry
    lax.fori_loop(0, bc_ref.shape[0], conv_bc, 0)

    dt = jax.nn.softplus(dt_ref[...] + dtb_ref[...])
    a = dt * (-jnp.exp(alog_ref[...]))
    cs = _dot_exact_rhs(m_ref[...], _split3(a))
    cst_ref[...] = cs.T
    for p, (dt_p, cs_p) in enumerate(zip(_split3(dt), _split3(cs))):
        dt3_ref[p] = dt_p
        cs3_ref[p] = cs_p

    keep = _causal_keep(T)

    def spread(g):
        e_g = e_ref[g]
        return (_dot_exact_lhs([dt3_ref[p] for p in range(3)], e_g), _dot_exact_lhs([cs3_ref[p] for p in range(3)], e_g))

    def group(g, dt_x, cs_x):
        h0 = g * GROUP_HEADS
        rows = pl.ds(g * GROUP_W, GROUP_W)
        yn, new_states = _ssd_group(
            xc_ref[g], bcs_ref[g], bcs_ref[SSD_GROUPS + g], z_ref[g], dt_x, cs_x, cs_x[T - 1:T, :],
            lambda h: cst_ref[pl.ds(h0 + h, 1), :],
            lambda s, h: jnp.exp(cst_ref[pl.ds(h0 + h, 1), T - 1:T]),
            [st_ref[rows, :]], 1, dsk_ref[g], nw_ref[g], keep)
        yn_ref[g] = yn
        st_ref[rows, :] = new_states[0]

    spread_next = spread(0)
    for g in range(SSD_GROUPS):
        dt_x, cs_x = spread_next
        if g + 1 < SSD_GROUPS:
            spread_next = spread(g + 1)
        group(g, dt_x, cs_x)


def _ssd_sample_body(z_ref, xs_ref, b_ref, c_ref, px_ref, pb_ref, pc_ref, dt_ref, wx_ref, wb_ref, wc_ref,
                     bx_ref, bb_ref, bc_ref, dtb_ref, alog_ref, dsk_ref, nw_ref, m_ref, mseg_ref, e_ref,
                     st_in_ref, yn_ref, st_ref, cst_ref, cet_ref, *, seq_len, g):
    n_seg = T // seq_len

    def conv(x_ref, p_ref, w_ref, bias_ref):
        x = x_ref[...]
        width = x.shape[-1]
        taps = SSD_CONV - 1
        prev = [jnp.broadcast_to(p_ref[k][:, None, :], (n_seg, seq_len, width)).reshape(T, width) for k in range(taps)]
        tpos = lax.broadcasted_iota(jnp.int32, x.shape, 0) % seq_len
        acc = bias_ref[...] + x * w_ref[taps:SSD_CONV, :]
        for sh in range(1, SSD_CONV):
            hist = prev[taps - 1]
            for t in range(sh - 2, -1, -1):
                hist = jnp.where(tpos == t, prev[taps - sh + t], hist)
            shifted = jnp.where(tpos >= sh, pltpu.roll(x, sh, 0), hist)
            acc = acc + shifted * w_ref[taps - sh:SSD_CONV - sh, :]
        return _silu(acc)

    xs = conv(xs_ref, px_ref, wx_ref, bx_ref)
    bm = conv(b_ref, pb_ref, wb_ref, bb_ref)
    cm = conv(c_ref, pc_ref, wc_ref, bc_ref)

    dt = jax.nn.softplus(dt_ref[...] + dtb_ref[...])
    a3 = _split3(dt * (-jnp.exp(alog_ref[...])))
    cs = _dot_exact_rhs(m_ref[...], a3)
    cs_end = _dot_exact_rhs(mseg_ref[...], a3)
    cst_ref[...] = cs.T
    cet_ref[...] = cs_end.T
    e_g = e_ref[...]
    h0 = g * GROUP_HEADS
    yn, new_states = _ssd_group(
        xs, bm, cm, z_ref[...], _dot_exact_lhs(_split3(dt), e_g), _dot_exact_lhs(_split3(cs), e_g),
        _dot_exact_lhs(_split3(cs_end), e_g),
        lambda h: cst_ref[pl.ds(h0 + h, 1), :],
        lambda s, h: jnp.exp(cet_ref[pl.ds(h0 + h, 1), s * seq_len:s * seq_len + 1]),
        [st_in_ref[s] for s in range(n_seg)], n_seg, dsk_ref[...], nw_ref[...], _causal_keep(seq_len))
    yn_ref[...] = yn
    for s in range(n_seg):
        st_ref[s] = new_states[s]


def _ssd_masks(seq_len):
    r = jnp.arange(T)
    same = (r[:, None] // seq_len) == (r[None, :] // seq_len)
    return jnp.logical_and(same, r[None, :] <= r[:, None]).astype(BF16), same.astype(BF16)


N_PROMPT_IN, N_SAMPLE_IN = 12, 22


def _ssd_scan_body(*refs, n_chunks, seq_len_s):
    s = pl.program_id(0)
    p_in = refs[:N_PROMPT_IN]
    s_in = refs[N_PROMPT_IN:N_PROMPT_IN + N_SAMPLE_IN]
    yn_p, st_p, yn_s, st_s = refs[N_PROMPT_IN + N_SAMPLE_IN:N_PROMPT_IN + N_SAMPLE_IN + 4]
    scratch = refs[N_PROMPT_IN + N_SAMPLE_IN + 4:]
    _ssd_prompt_body(*p_in, yn_p, st_p, *scratch[:6], c=s % n_chunks)
    _ssd_sample_body(*s_in, yn_s, st_s, *scratch[6:], seq_len=seq_len_s, g=s % SSD_GROUPS)


def _ssd_scan(zx, dt_raw, conv_w, conv_b, dtb, alog, dsk, nw, state_s, conv_hist_s, n_pseq, seq_len_p, seq_len_s):
    heads = dt_raw.shape[1]
    inner = SSD_GROUPS * GROUP_W
    n_chunks = seq_len_p // T
    m_p = n_pseq * seq_len_p
    n_seg = T // seq_len_s
    n_sseq = state_s.shape[0]
    row0 = m_p // T
    n_steps = n_pseq * n_chunks
    assert n_steps == (n_sseq // n_seg) * SSD_GROUPS
    n_bc = 2 * SSD_GROUPS // BC_PER_BLOCK
    n_cblk = SSD_GROUPS + n_bc
    cw = conv_w.reshape(SSD_CONV, n_cblk, GROUP_W).transpose(1, 0, 2)
    cbias = conv_b.reshape(n_cblk, 1, GROUP_W)
    e_all = (jnp.arange(inner)[None, :] // SSD_HEAD_DIM == jnp.arange(heads)[:, None]).astype(BF16)
    e_grp = e_all.reshape(heads, SSD_GROUPS, GROUP_W).transpose(1, 0, 2)
    m_tril, _ = _ssd_masks(T)
    m_s, mseg_s = _ssd_masks(seq_len_s)
    xoff = SSD_GROUPS
    boff = 2 * SSD_GROUPS
    coff = boff + SSD_GROUPS // BC_PER_BLOCK
    cb0 = inner // SSD_STATE
    cc0 = cb0 + SSD_GROUPS

    full = lambda shape: pl.BlockSpec(shape, lambda s: (0,) * len(shape))
    bp = lambda s: s // n_chunks
    ts = lambda s: s // SSD_GROUPS
    gs = lambda s: s % SSD_GROUPS
    prompt_in = [
        pl.BlockSpec((SSD_GROUPS, T, GROUP_W), lambda s: (0, s, 0)),
        pl.BlockSpec((SSD_GROUPS, T, GROUP_W), lambda s: (1, s, 0)),
        pl.BlockSpec((n_bc, T, GROUP_W), lambda s: (2 * SSD_GROUPS // n_bc, s, 0)),
        pl.BlockSpec((T, heads), lambda s: (s, 0)),
        full((n_cblk, SSD_CONV, GROUP_W)), full((n_cblk, 1, GROUP_W)), full((1, heads)), full((1, heads)),
        full((SSD_GROUPS, 1, GROUP_W)), full((SSD_GROUPS, 1, GROUP_W)), full((T, T)),
        full((SSD_GROUPS, heads, GROUP_W)),
    ]
    per_g = lambda shape: pl.BlockSpec(shape, lambda s: (gs(s),) + (0,) * (len(shape) - 1))
    sample_in = [
        pl.BlockSpec((None, T, GROUP_W), lambda s: (gs(s), row0 + ts(s), 0)),
        pl.BlockSpec((None, T, GROUP_W), lambda s: (xoff + gs(s), row0 + ts(s), 0)),
        pl.BlockSpec((None, T, SSD_STATE),
                     lambda s: (boff + gs(s) // BC_PER_BLOCK, row0 + ts(s), gs(s) % BC_PER_BLOCK)),
        pl.BlockSpec((None, T, SSD_STATE),
                     lambda s: (coff + gs(s) // BC_PER_BLOCK, row0 + ts(s), gs(s) % BC_PER_BLOCK)),
        pl.BlockSpec((SSD_CONV - 1, n_seg, GROUP_W), lambda s: (0, ts(s), gs(s))),
        pl.BlockSpec((SSD_CONV - 1, n_seg, SSD_STATE), lambda s: (0, ts(s), cb0 + gs(s))),
        pl.BlockSpec((SSD_CONV - 1, n_seg, SSD_STATE), lambda s: (0, ts(s), cc0 + gs(s))),
        pl.BlockSpec((T, heads), lambda s: (row0 + ts(s), 0)),
        pl.BlockSpec((SSD_CONV, GROUP_W), lambda s: (0, gs(s))),
        pl.BlockSpec((SSD_CONV, SSD_STATE), lambda s: (0, cb0 + gs(s))),
        pl.BlockSpec((SSD_CONV, SSD_STATE), lambda s: (0, cc0 + gs(s))),
        pl.BlockSpec((1, GROUP_W), lambda s: (0, gs(s))),
        pl.BlockSpec((1, SSD_STATE), lambda s: (0, cb0 + gs(s))),
        pl.BlockSpec((1, SSD_STATE), lambda s: (0, cc0 + gs(s))),
        full((1, heads)), full((1, heads)),
        per_g((None, 1, GROUP_W)), per_g((None, 1, GROUP_W)),
        full((T, T)), full((T, T)),
        pl.BlockSpec((heads, GROUP_W), lambda s: (0, gs(s))),
        pl.BlockSpec((n_seg, GROUP_W, SSD_STATE), lambda s: (ts(s), gs(s), 0)),
    ]
    assert len(prompt_in) == N_PROMPT_IN and len(sample_in) == N_SAMPLE_IN
    return pl.pallas_call(
        functools.partial(_ssd_scan_body, n_chunks=n_chunks, seq_len_s=seq_len_s),
        grid=(n_steps,),
        in_specs=prompt_in + sample_in,
        out_specs=[pl.BlockSpec((SSD_GROUPS, T, GROUP_W), lambda s: (0, s, 0)),
                   pl.BlockSpec((None, inner, SSD_STATE), lambda s: (bp(s), 0, 0)),
                   pl.BlockSpec((None, T, GROUP_W), lambda s: (gs(s), ts(s), 0)),
                   pl.BlockSpec((n_seg, GROUP_W, SSD_STATE), lambda s: (ts(s), gs(s), 0))],
        out_shape=[jax.ShapeDtypeStruct((SSD_GROUPS, m_p, GROUP_W), BF16),
                   jax.ShapeDtypeStruct((n_pseq, inner, SSD_STATE), F32),
                   jax.ShapeDtypeStruct((SSD_GROUPS, n_sseq * seq_len_s, GROUP_W), BF16),
                   jax.ShapeDtypeStruct(state_s.shape, F32)],
        scratch_shapes=[pltpu.VMEM((n_cblk, SUBLANES + T, GROUP_W), F32),
                        pltpu.VMEM((SSD_GROUPS, T, GROUP_W), F32),
                        pltpu.VMEM((2 * SSD_GROUPS, T, SSD_STATE), F32),
                        pltpu.VMEM((3, T, heads), BF16),
                        pltpu.VMEM((3, T, heads), BF16),
                        pltpu.VMEM((heads, T), F32),
                        pltpu.VMEM((heads, T), F32), pltpu.VMEM((heads, T), F32)],
        compiler_params=_cparams(("arbitrary",)),
        name="ssd_scan",
    )(zx, zx, zx, dt_raw, cw, cbias, dtb, alog, dsk, nw, m_tril, e_grp,
      zx, zx, zx, zx, conv_hist_s, conv_hist_s, conv_hist_s, dt_raw, conv_w, conv_w, conv_w, conv_b, conv_b, conv_b,
      dtb, alog, dsk, nw, m_s, mseg_s, e_all, state_s)


def _final_body(x_ref, w_ref, yp_ref, ys_ref, *, n_ptiles):
    i = pl.program_id(0)
    x = x_ref[...]
    y = x * lax.rsqrt(jnp.mean(x * x, axis=-1, keepdims=True) + NORM_EPS) * w_ref[...]

    @pl.when(i < n_ptiles)
    def _():
        yp_ref[...] = y

    @pl.when(i >= n_ptiles)
    def _():
        ys_ref[...] = y


def _final_norm(x, w, n_prompt_rows):
    m, d = x.shape
    tm = TM
    n_ptiles = n_prompt_rows // tm
    return pl.pallas_call(
        functools.partial(_final_body, n_ptiles=n_ptiles),
        grid=(m // tm,),
        in_specs=[pl.BlockSpec((tm, d), lambda i: (i, 0)), pl.BlockSpec((1, d), lambda i: (0, 0))],
        out_specs=[pl.BlockSpec((tm, d), lambda i: (jnp.minimum(i, n_ptiles - 1), 0)),
                   pl.BlockSpec((tm, d), lambda i: (jnp.maximum(i - n_ptiles, 0), 0))],
        out_shape=[jax.ShapeDtypeStruct((n_prompt_rows, d), F32), jax.ShapeDtypeStruct((m - n_prompt_rows, d), F32)],
        compiler_params=_cparams(("arbitrary",)),
        name="final_norm",
    )(x, w.reshape(1, d))


def kernel(x_prompt, x_sample, c_prompt, c_sample, state_ssm, state_conv, mod_w, mod_b, norm_mix_w, norm_ffn_w,
           a_w_in, a_b_in, a_ln_w, a_ln_b, a_w_s, a_b_s, a_w_out,
           b_w_in, b_conv_w, b_conv_b, b_dt_bias, b_a_log, b_d, b_norm_w, b_w_out,
           f_w_in, f_w_out, final_norm_w):
    bp, lp, d = x_prompt.shape
    bs, ls, _ = x_sample.shape
    depth = mod_w.shape[0]
    n_prompt = bp * lp
    n_sample = bs * ls
    m_all = n_prompt + n_sample
    assert lp % TM == 0 and n_sample % TM == 0 and TM % ls == 0 and bp <= SUBLANES
    assert ls >= SSD_CONV - 1 and T % ls == 0 and lp % T == 0 and GMLP_CHUNK % ls == 0 and lp % MIX_ROWS == 0
    assert depth == 2 and a_w_in.shape[0] == 1 and b_w_in.shape[0] == 1
    cfg = Cfg(n_ptiles=n_prompt // TM, tiles_per_seq=lp // TM, seq_len_s=ls, srow0=SUBLANES)

    x_p = x_prompt.reshape(n_prompt, d)
    x_s = x_sample.reshape(n_sample, d)
    c_all = jnp.concatenate([c_prompt, jnp.zeros((SUBLANES - bp, d), F32), c_sample], axis=0)
    mods = _mod_table(c_all, mod_w, mod_b)
    gate_m, gate_f = 2, 5

    u, v = _gmlp_in(x_p, x_s, mods, 0, norm_mix_w, a_w_in, a_b_in, 0, cfg)
    r = jnp.arange(GMLP_CHUNK)
    tril = r[None, :] <= r[:, None]
    mask = jnp.stack([tril, jnp.logical_and(tril, (r[:, None] // ls) == (r[None, :] // ls))]).astype(F32)
    rep = GMLP_CHUNK // ls
    wmix = jnp.stack([a_w_s[0], jnp.tile(a_w_s[0, :, :ls, :ls], (1, rep, rep))])
    width = a_w_in.shape[-1] // 2
    bias = jnp.stack([jnp.repeat(a_b_s[0].T, width // GMLP_GROUPS, axis=1),
                      jnp.repeat(jnp.tile(a_b_s[0, :, :ls].T, (rep, 1)), width // GMLP_GROUPS, axis=1)])
    gated, v_p, v_s = _gmlp_mix(u, v, a_ln_w[0], a_ln_b[0], wmix, mask, bias, n_prompt, lp, bp)
    x = _resid_matmul(gated, a_w_out, 0, x_p, x_s, mods, gate_m, cfg, "gmlp_out")
    act = _ffn_in(x, mods, 0, norm_ffn_w, f_w_in, cfg)
    x = _resid_matmul(act, f_w_out, 0, x, None, mods, gate_f, cfg, "ffn_out0")

    inner = b_w_out.shape[1]
    conv_dim = b_conv_w.shape[-1]
    n_main = inner + conv_dim
    heads = inner // SSD_HEAD_DIM
    zx, dt_raw = _ssd_in(x, mods, 1, norm_mix_w, jnp.swapaxes(b_w_in, 1, 2)[0], n_main, cfg)
    conv_hist = jnp.swapaxes(state_conv[0], 0, 1)
    ssd_params = (b_conv_w[0], b_conv_b, b_dt_bias, b_a_log,
                  jnp.repeat(b_d[0], SSD_HEAD_DIM).reshape(SSD_GROUPS, 1, GROUP_W),
                  b_norm_w.reshape(SSD_GROUPS, 1, GROUP_W))
    state_s = state_ssm[0].reshape(bs, inner, SSD_STATE)
    yn_p, ssm_p, yn_s, ssm_s = _ssd_scan(zx, dt_raw, *ssd_params, state_s, conv_hist, bp, lp, ls)
    x = _resid_matmul(yn_p, b_w_out, 0, x, None, mods, MOD_CHUNKS + gate_m, cfg, "ssd_out", a_s=yn_s)
    act = _ffn_in(x, mods, 1, norm_ffn_w, f_w_in, cfg)
    x = _resid_matmul(act, f_w_out, 1, x, None, mods, MOD_CHUNKS + gate_f, cfg, "ffn_out1")

    y_p, y_s = _final_norm(x, final_norm_w, n_prompt)

    zx4 = zx.reshape(zx.shape[0], m_all // ls, ls, TN)

    def tails(groups):
        t = groups[SSD_GROUPS:, :, ls - (SSD_CONV - 1):, :]
        return jnp.moveaxis(t, 0, 2).reshape(t.shape[1], SSD_CONV - 1, conv_dim)
    conv_p = tails(zx4[:, lp // ls - 1:n_prompt // ls:lp // ls])
    conv_s = tails(zx4[:, n_prompt // ls:])
    return (y_p.reshape(bp, lp, d), y_s.reshape(bs, ls, d),
            v_p.reshape(1, bp, GMLP_CHUNK, width), v_s.reshape(1, bs, ls, width),
            ssm_p.reshape(1, bp, heads, SSD_HEAD_DIM, SSD_STATE), ssm_s.reshape(1, bs, heads, SSD_HEAD_DIM, SSD_STATE),
            conv_p[None], conv_s[None])
```

```python
import functools
import math
from typing import NamedTuple

import jax
import jax.numpy as jnp
from jax import lax
from jax.experimental import pallas as pl
from jax.experimental.pallas import tpu as pltpu

F32 = jnp.float32
BF16 = jnp.bfloat16

NORM_EPS = 1e-6
LN_EPS = 1e-5

GMLP_GROUPS = 16
GMLP_CHUNK = 128
SSD_HEAD_DIM = 64
SSD_STATE = 128
SSD_GROUPS = 8
SSD_CONV = 4
SSD_CHUNK = 128

SUBLANES = 8
LANES = 128
VMEM_LIMIT_BYTES = 56 * 1024 * 1024

TM = 1024
TN = 512
TN_NARROW = 256
ROW_CHUNK = 256
SUB_ROWS = 16


class Cfg(NamedTuple):
    n_ptiles: int
    tiles_per_seq: int
    seq_len_s: int
    srow0: int


def _cparams(sem):
    return pltpu.CompilerParams(dimension_semantics=sem, vmem_limit_bytes=VMEM_LIMIT_BYTES)


def _silu(x):
    return x / (1.0 + jnp.exp(-x))


def _gelu(x):
    return 0.5 * x * (1.0 + lax.erf(x * (1.0 / math.sqrt(2.0))))


def _rms_mod(x, w, scale, shift):
    y = x * lax.rsqrt(jnp.mean(x * x, axis=-1, keepdims=True) + NORM_EPS)
    return (y * w) * (1.0 + scale) + shift


def _piece_rows(c, q, piece):
    return pl.ds(pl.multiple_of(c * ROW_CHUNK + q * piece, piece), piece)


def _sample_rows(i, cfg, x_refs, mod_refs, fn, o_ref, piece=ROW_CHUNK):
    rows = o_ref.shape[0]
    nseq = ROW_CHUNK // cfg.seq_len_s
    sub_seq = piece // cfg.seq_len_s
    row0 = cfg.srow0 + (i - cfg.n_ptiles) * (rows // cfg.seq_len_s)

    def body(c, carry):
        r0 = pl.multiple_of(row0 + c * nseq, SUBLANES)
        ms_chunk = [m[pl.ds(r0, nseq), :] for m in mod_refs]
        for q in range(ROW_CHUNK // piece):
            rs = _piece_rows(c, q, piece)
            ms = [mc[q * sub_seq:(q + 1) * sub_seq][:, None, :] for mc in ms_chunk]
            x3 = [x[rs, :].reshape(sub_seq, cfg.seq_len_s, x.shape[-1]) for x in x_refs]
            o_ref[rs, :] = fn(x3, ms).reshape(piece, o_ref.shape[-1]).astype(o_ref.dtype)
        return carry
    lax.fori_loop(0, rows // ROW_CHUNK, body, 0)


def _per_seq(i, cfg, xp_refs, xs_refs, mod_refs, fn, o_ref, piece=ROW_CHUNK, os_ref=None):
    @pl.when(i < cfg.n_ptiles)
    def _():
        s = i // cfg.tiles_per_seq
        ms = [m[pl.ds(s, 1), :] for m in mod_refs]

        def body(c, carry):
            for q in range(ROW_CHUNK // piece):
                rs = _piece_rows(c, q, piece)
                o_ref[rs, :] = fn([x[rs, :] for x in xp_refs], ms).astype(o_ref.dtype)
            return carry
        lax.fori_loop(0, o_ref.shape[0] // ROW_CHUNK, body, 0)

    @pl.when(i >= cfg.n_ptiles)
    def _():
        _sample_rows(i, cfg, xs_refs, mod_refs, fn, o_ref if os_ref is None else os_ref, piece)


def _prompt_block(cfg):
    return lambda i: jnp.minimum(i, cfg.n_ptiles - 1)


def _sample_block(cfg):
    return lambda i: jnp.maximum(i - cfg.n_ptiles, 0)


def _mod_body(c_ref, w_ref, b_ref, o_ref):
    sc = _silu(c_ref[...]).astype(BF16)
    o_ref[...] = jnp.dot(sc, w_ref[...].astype(BF16), preferred_element_type=F32) + b_ref[...]


MOD_CHUNKS = 6


def _mod_table(c_all, mod_w, mod_b):
    depth, d, n = mod_w.shape
    assert n == MOD_CHUNKS * d
    r = c_all.shape[0]
    tn = 1024
    nj = n // tn
    return pl.pallas_call(
        _mod_body,
        grid=(depth * nj,),
        in_specs=[
            pl.BlockSpec((r, d), lambda j: (0, 0)),
            pl.BlockSpec((None, d, tn), lambda j: (j // nj, 0, j % nj)),
            pl.BlockSpec((None, 1, tn), lambda j: (j // nj, 0, j % nj)),
        ],
        out_specs=pl.BlockSpec((r, tn), lambda j: (0, j)),
        out_shape=jax.ShapeDtypeStruct((r, depth * n), F32),
        compiler_params=_cparams(("arbitrary",)),
        name="mod_table",
    )(c_all, mod_w, mod_b.reshape(depth, 1, n))


def _norm_prologue(i, j, cfg, xp_ref, xs_ref, nw_ref, sc_ref, sh_ref, h_ref):
    @pl.when(j == 0)
    def _():
        _per_seq(i, cfg, [xp_ref], [xs_ref], [sc_ref, sh_ref],
                 lambda xs, ms: _rms_mod(xs[0], nw_ref[...], ms[0], ms[1]), h_ref, piece=SUB_ROWS)


def _gmlp_in_body(xp_ref, xs_ref, nw_ref, sh_ref, sc_ref, wu_ref, wv_ref, bu_ref, bv_ref, u_ref, v_ref, h_ref, *, cfg):
    i, j = pl.program_id(0), pl.program_id(1)
    _norm_prologue(i, j, cfg, xp_ref, xs_ref, nw_ref, sc_ref, sh_ref, h_ref)
    h = h_ref[...]
    u_ref[...] = _gelu(jnp.dot(h, wu_ref[...].astype(BF16), preferred_element_type=F32) + bu_ref[...])
    v_ref[...] = _gelu(jnp.dot(h, wv_ref[...].astype(BF16), preferred_element_type=F32) + bv_ref[...])


def _gmlp_in(x_p, x_s, mods, layer, norm_w, w_in, b_in, j_layer, cfg):
    d = x_p.shape[1]
    m = x_p.shape[0] + x_s.shape[0]
    width = w_in.shape[-1] // 2
    tn = TN_NARROW
    nj = width // tn
    r = mods.shape[0]
    pb, sb = _prompt_block(cfg), _sample_block(cfg)
    b2 = b_in.reshape(b_in.shape[0], 1, -1)
    return pl.pallas_call(
        functools.partial(_gmlp_in_body, cfg=cfg),
        grid=(m // TM, nj),
        in_specs=[
            pl.BlockSpec((TM, d), lambda i, j: (pb(i), 0)),
            pl.BlockSpec((TM, d), lambda i, j: (sb(i), 0), pipeline_mode=pl.Buffered(1)),
            pl.BlockSpec((None, 1, d), lambda i, j: (layer, 0, 0)),
            pl.BlockSpec((r, d), lambda i, j: (0, MOD_CHUNKS * layer + 0)),
            pl.BlockSpec((r, d), lambda i, j: (0, MOD_CHUNKS * layer + 1)),
            pl.BlockSpec((None, d, tn), lambda i, j: (j_layer, 0, j)),
            pl.BlockSpec((None, d, tn), lambda i, j: (j_layer, 0, j + nj)),
            pl.BlockSpec((None, 1, tn), lambda i, j: (j_layer, 0, j)),
            pl.BlockSpec((None, 1, tn), lambda i, j: (j_layer, 0, j + nj)),
        ],
        out_specs=[pl.BlockSpec((TM, tn), lambda i, j: (i, j)), pl.BlockSpec((TM, tn), lambda i, j: (i, j))],
        out_shape=[jax.ShapeDtypeStruct((m, width), F32), jax.ShapeDtypeStruct((m, width), F32)],
        scratch_shapes=[pltpu.VMEM((TM, d), BF16)],
        compiler_params=_cparams(("arbitrary", "arbitrary")),
        name="gmlp_in",
    )(x_p, x_s, norm_w.reshape(-1, 1, d), mods, mods, w_in, w_in, b2, b2)


def _ffn_in_body(x_ref, nw_ref, sh_ref, sc_ref, wg_ref, wu_ref, a_ref, h_ref, *, cfg):
    i, j = pl.program_id(0), pl.program_id(1)
    _norm_prologue(i, j, cfg, x_ref, x_ref, nw_ref, sc_ref, sh_ref, h_ref)
    h = h_ref[...]
    gate = jnp.dot(h, wg_ref[...].astype(BF16), preferred_element_type=F32)
    up = jnp.dot(h, wu_ref[...].astype(BF16), preferred_element_type=F32)
    a_ref[...] = (_silu(gate) * up).astype(BF16)


def _ffn_in(x, mods, layer, norm_w, w_in, cfg):
    m, d = x.shape
    hidden = w_in.shape[-1] // 2
    nj = hidden // TN
    r = mods.shape[0]
    return pl.pallas_call(
        functools.partial(_ffn_in_body, cfg=cfg),
        grid=(m // TM, nj),
        in_specs=[
            pl.BlockSpec((TM, d), lambda i, j: (i, 0)),
            pl.BlockSpec((None, 1, d), lambda i, j: (layer, 0, 0)),
            pl.BlockSpec((r, d), lambda i, j: (0, MOD_CHUNKS * layer + 3)),
            pl.BlockSpec((r, d), lambda i, j: (0, MOD_CHUNKS * layer + 4)),
            pl.BlockSpec((None, d, TN), lambda i, j: (layer, 0, j)),
            pl.BlockSpec((None, d, TN), lambda i, j: (layer, 0, j + nj)),
        ],
        out_specs=pl.BlockSpec((TM, TN), lambda i, j: (i, j)),
        out_shape=jax.ShapeDtypeStruct((m, hidden), BF16),
        scratch_shapes=[pltpu.VMEM((TM, d), BF16)],
        compiler_params=_cparams(("arbitrary", "arbitrary")),
        name="ffn_in",
    )(x, norm_w.reshape(-1, 1, d), mods, mods, w_in, w_in)


_NT = (((1,), (1,)), ((), ()))


def _ssd_in_body(x_ref, nw_ref, sh_ref, sc_ref, w_ref, wdt_ref, o_ref, dt_ref, h_ref, *, cfg):
    i, j = pl.program_id(0), pl.program_id(1)
    _norm_prologue(i, j, cfg, x_ref, x_ref, nw_ref, sc_ref, sh_ref, h_ref)
    h = h_ref[...]
    o_ref[...] = lax.dot_general(h, w_ref[...].astype(BF16), _NT, preferred_element_type=F32)

    @pl.when(j == 0)
    def _():
        dt_ref[...] = lax.dot_general(h, wdt_ref[...].astype(BF16), _NT, preferred_element_type=F32)


def _ssd_in(x, mods, layer, norm_w, w_in_t, n_main, cfg):
    m, d = x.shape
    r = mods.shape[0]
    n_dt = w_in_t.shape[0] - n_main
    nj = n_main // TN
    return pl.pallas_call(
        functools.partial(_ssd_in_body, cfg=cfg),
        grid=(m // TM, nj),
        in_specs=[
            pl.BlockSpec((TM, d), lambda i, j: (i, 0)),
            pl.BlockSpec((None, 1, d), lambda i, j: (layer, 0, 0)),
            pl.BlockSpec((r, d), lambda i, j: (0, MOD_CHUNKS * layer + 0)),
            pl.BlockSpec((r, d), lambda i, j: (0, MOD_CHUNKS * layer + 1)),
            pl.BlockSpec((TN, d), lambda i, j: (j, 0)),
            pl.BlockSpec((n_dt, d), lambda i, j: (n_main // n_dt, 0)),
        ],
        out_specs=[pl.BlockSpec((None, TM, TN), lambda i, j: (j, i, 0)), pl.BlockSpec((TM, n_dt), lambda i, j: (i, 0))],
        out_shape=[jax.ShapeDtypeStruct((nj, m, TN), F32), jax.ShapeDtypeStruct((m, n_dt), F32)],
        scratch_shapes=[pltpu.VMEM((TM, d), BF16)],
        compiler_params=_cparams(("arbitrary", "arbitrary")),
        name="ssd_in",
    )(x, norm_w.reshape(-1, 1, d), mods, mods, w_in_t, w_in_t)


TK = 512


def _resid_body(a_ref, *rest, cfg, nk, two_a):
    as_ref = None
    if two_a:
        as_ref, *rest = rest
    w_ref, *rest = rest
    if len(rest) == 4:
        xp_ref, xs_ref, g_ref, o_ref = rest
    else:
        xp_ref, g_ref, o_ref = rest
        xs_ref = xp_ref
    i, k = pl.program_id(0), pl.program_id(1)
    d = o_ref.shape[1]

    def accumulate(first):
        a = a_ref[...] if as_ref is None else jnp.where(i < cfg.n_ptiles, a_ref[...], as_ref[...])
        for c in range(d // TN):
            cols = slice(c * TN, (c + 1) * TN)
            part = jnp.dot(a, w_ref[:, cols].astype(BF16), preferred_element_type=F32)
            if first:
                o_ref[:, cols] = part
            else:
                o_ref[:, cols] += part

    @pl.when(k == 0)
    def _():
        accumulate(True)

    @pl.when(k > 0)
    def _():
        accumulate(False)

    @pl.when(k == nk - 1)
    def _():
        _per_seq(i, cfg, [xp_ref, o_ref], [xs_ref, o_ref], [g_ref], lambda xs, ms: xs[0] + ms[0] * xs[1], o_ref)


def _resid_matmul(a, w, w_layer, x_p, x_s, mods, gate_chunk, cfg, name, a_s=None):
    m = a.shape[-2] + (0 if a_s is None else a_s.shape[-2])
    d = x_p.shape[1]
    r = mods.shape[0]
    pb, sb = _prompt_block(cfg), _sample_block(cfg)
    if a.ndim == 2:
        nk = a.shape[1] // TK
        a_specs, a_args = [pl.BlockSpec((TM, TK), lambda i, k: (i, k))], [a]
    else:
        nk = a.shape[0]
        assert a.shape[2] == TK
        if a_s is None:
            a_specs, a_args = [pl.BlockSpec((None, TM, TK), lambda i, k: (k, i, 0))], [a]
        else:
            a_specs = [pl.BlockSpec((None, TM, TK), lambda i, k: (jnp.where(i < cfg.n_ptiles, k, 0), pb(i), 0)),
                       pl.BlockSpec((None, TM, TK), lambda i, k: (jnp.where(i < cfg.n_ptiles, 0, k), sb(i), 0))]
            a_args = [a, a_s]
    if x_s is None:
        x_specs, xs = [pl.BlockSpec((TM, d), lambda i, k: (i, 0))], [x_p]
    else:
        x_specs = [pl.BlockSpec((TM, d), lambda i, k: (pb(i), 0)),
                   pl.BlockSpec((TM, d), lambda i, k: (sb(i), 0), pipeline_mode=pl.Buffered(1))]
        xs = [x_p, x_s]
    return pl.pallas_call(
        functools.partial(_resid_body, cfg=cfg, nk=nk, two_a=a_s is not None),
        grid=(m // TM, nk),
        in_specs=a_specs + [pl.BlockSpec((None, TK, d), lambda i, k: (w_layer, k, 0))] + x_specs
        + [pl.BlockSpec((r, d), lambda i, k: (0, gate_chunk))],
        out_specs=pl.BlockSpec((TM, d), lambda i, k: (i, 0)),
        out_shape=jax.ShapeDtypeStruct((m, d), F32),
        compiler_params=_cparams(("arbitrary", "arbitrary")),
        name=name,
    )(*a_args, w, *xs, mods)


def _resid_final_body(a_ref, w_ref, x_ref, g_ref, fw_ref, yp_ref, ys_ref, *, cfg, nk):
    i, k = pl.program_id(0), pl.program_id(1)
    d = yp_ref.shape[1]
    prompt = i < cfg.n_ptiles

    def accumulate(first, o_ref):
        a = a_ref[...]
        for c in range(d // TN):
            cols = slice(c * TN, (c + 1) * TN)
            part = jnp.dot(a, w_ref[:, cols].astype(BF16), preferred_element_type=F32)
            if first:
                o_ref[:, cols] = part
            else:
                o_ref[:, cols] += part

    @pl.when(jnp.logical_and(prompt, k == 0))
    def _():
        accumulate(True, yp_ref)

    @pl.when(jnp.logical_and(prompt, k > 0))
    def _():
        accumulate(False, yp_ref)

    @pl.when(jnp.logical_and(jnp.logical_not(prompt), k == 0))
    def _():
        accumulate(True, ys_ref)

    @pl.when(jnp.logical_and(jnp.logical_not(prompt), k > 0))
    def _():
        accumulate(False, ys_ref)

    def finish(xs, ms):
        xn = xs[0] + ms[0] * xs[1]
        return xn * lax.rsqrt(jnp.mean(xn * xn, axis=-1, keepdims=True) + NORM_EPS) * fw_ref[...]

    @pl.when(k == nk - 1)
    def _():
        _per_seq(i, cfg, [x_ref, yp_ref], [x_ref, ys_ref], [g_ref], finish, yp_ref, piece=4 * SUB_ROWS, os_ref=ys_ref)


def _resid_final(a, w, w_layer, x, mods, gate_chunk, final_w, n_prompt_rows, cfg):
    m, kdim = a.shape
    d = x.shape[1]
    r = mods.shape[0]
    nk = kdim // TK
    pb, sb = _prompt_block(cfg), _sample_block(cfg)
    return pl.pallas_call(
        functools.partial(_resid_final_body, cfg=cfg, nk=nk),
        grid=(m // TM, nk),
        in_specs=[
            pl.BlockSpec((TM, TK), lambda i, k: (i, k)),
            pl.BlockSpec((None, TK, d), lambda i, k: (w_layer, k, 0)),
            pl.BlockSpec((TM, d), lambda i, k: (i, 0)),
            pl.BlockSpec((r, d), lambda i, k: (0, gate_chunk)),
            pl.BlockSpec((1, d), lambda i, k: (0, 0)),
        ],
        out_specs=[pl.BlockSpec((TM, d), lambda i, k: (pb(i), 0)),
                   pl.BlockSpec((TM, d), lambda i, k: (sb(i), 0), pipeline_mode=pl.Buffered(1))],
        out_shape=[jax.ShapeDtypeStruct((n_prompt_rows, d), F32), jax.ShapeDtypeStruct((m - n_prompt_rows, d), F32)],
        compiler_params=_cparams(("arbitrary", "arbitrary")),
        name="ffn_out_final",
    )(a, w, x, mods, final_w.reshape(1, d))


MIX_ROWS = 4 * GMLP_CHUNK


def _gmlp_mix_body(u_ref, v_ref, lnw_ref, lnb_ref, ws_ref, mask_ref, bias_ref, g_ref, vp_ref, vs_ref,
                   vn_ref, *, n_prompt_steps, steps_per_seq):
    t = pl.program_id(0)
    v = v_ref[...]
    xc = v - jnp.mean(v, axis=-1, keepdims=True)
    vn = xc * lax.rsqrt(jnp.mean(xc * xc, axis=-1, keepdims=True) + LN_EPS) * lnw_ref[...] + lnb_ref[...]
    vn_ref[...] = vn

    @pl.when(jnp.logical_and(t < n_prompt_steps, t % steps_per_seq == steps_per_seq - 1))
    def _():
        vp_ref[...] = vn[MIX_ROWS - GMLP_CHUNK:, :]

    @pl.when(t >= n_prompt_steps)
    def _():
        vs_ref[...] = vn

    mask = mask_ref[...]
    for g in range(GMLP_GROUPS):
        wb = (ws_ref[g] * mask).astype(BF16)
        cols = slice(g * GMLP_CHUNK, (g + 1) * GMLP_CHUNK)
        for c in range(MIX_ROWS // GMLP_CHUNK):
            rows = slice(c * GMLP_CHUNK, (c + 1) * GMLP_CHUNK)
            s = jnp.dot(wb, vn_ref[rows, cols].astype(BF16), preferred_element_type=F32) + bias_ref[:, cols]
            g_ref[rows, cols] = (u_ref[rows, cols] * s).astype(BF16)


def _gmlp_mix(u, v, ln_w, ln_b, wmix, mask, bias, n_prompt_rows, seq_len, n_prompt_seq):
    m, width = u.shape
    n_prompt_steps = n_prompt_rows // MIX_ROWS
    steps_per_seq = seq_len // MIX_ROWS
    n_sample_rows = m - n_prompt_rows

    def variant(t):
        return jnp.where(t < n_prompt_steps, 0, 1)

    return pl.pallas_call(
        functools.partial(_gmlp_mix_body, n_prompt_steps=n_prompt_steps, steps_per_seq=steps_per_seq),
        grid=(m // MIX_ROWS,),
        in_specs=[
            pl.BlockSpec((MIX_ROWS, width), lambda t: (t, 0)),
            pl.BlockSpec((MIX_ROWS, width), lambda t: (t, 0)),
            pl.BlockSpec((1, width), lambda t: (0, 0)),
            pl.BlockSpec((1, width), lambda t: (0, 0)),
            pl.BlockSpec((None, GMLP_GROUPS, GMLP_CHUNK, GMLP_CHUNK), lambda t: (variant(t), 0, 0, 0)),
            pl.BlockSpec((None, GMLP_CHUNK, GMLP_CHUNK), lambda t: (variant(t), 0, 0)),
            pl.BlockSpec((None, GMLP_CHUNK, width), lambda t: (variant(t), 0, 0)),
        ],
        out_specs=[
            pl.BlockSpec((MIX_ROWS, width), lambda t: (t, 0)),
            pl.BlockSpec((GMLP_CHUNK, width), lambda t: (jnp.minimum(t // steps_per_seq, n_prompt_seq - 1), 0)),
            pl.BlockSpec((MIX_ROWS, width), lambda t: (jnp.maximum(t - n_prompt_steps, 0), 0)),
        ],
        out_shape=[
            jax.ShapeDtypeStruct((m, width), BF16),
            jax.ShapeDtypeStruct((n_prompt_seq * GMLP_CHUNK, width), F32),
            jax.ShapeDtypeStruct((n_sample_rows, width), F32),
        ],
        scratch_shapes=[pltpu.VMEM((MIX_ROWS, width), F32)],
        compiler_params=_cparams(("arbitrary",)),
        name="gmlp_mix",
    )(u, v, ln_w.reshape(1, width), ln_b.reshape(1, width), wmix, mask, bias)


GROUP_HEADS = 8
GROUP_W = GROUP_HEADS * SSD_HEAD_DIM
T = SSD_CHUNK
BC_PER_BLOCK = GROUP_W // SSD_STATE


def _split3(x):
    hi = x.astype(BF16)
    r = x - hi.astype(F32)
    mid = r.astype(BF16)
    lo = (r - mid.astype(F32)).astype(BF16)
    return hi, mid, lo


def _dot_exact_rhs(m_b, pieces):
    return sum(jnp.dot(m_b, p, preferred_element_type=F32) for p in pieces)


def _dot_exact_lhs(pieces, e_b):
    return sum(jnp.dot(p, e_b, preferred_element_type=F32) for p in pieces)


def _ssd_group(xs, bm, cm, z, dt_x, cs_x, cs_end_x, cs_row, d_a, states, n_seg, dsk, nw, keep):
    seg = T // n_seg
    xdt = xs * dt_x
    ecs_x = jnp.exp(cs_x)
    dte_x = jnp.exp(cs_end_x - cs_x)
    xdt_b = xdt.astype(BF16)
    xd_t = (xdt * dte_x).T.astype(BF16)
    bmb = bm.astype(BF16)
    cmb = cm.astype(BF16)
    cb = lax.dot_general(cmb, bmb, _NT, preferred_element_type=F32)

    row = lax.broadcasted_iota(jnp.int32, (T, SSD_STATE), 0)
    y_offs, new_states = [], []
    for s in range(n_seg):
        st = states[s]
        c_seg = cmb if n_seg == 1 else cm[s * seg:(s + 1) * seg].astype(BF16)
        y_offs.append(lax.dot_general(c_seg, st.astype(BF16), _NT, preferred_element_type=F32))
        if n_seg == 1:
            b_seg = bmb
        else:
            b_seg = jnp.where(jnp.logical_and(row >= s * seg, row < (s + 1) * seg), bm, 0.0).astype(BF16)
        upd = jnp.dot(xd_t, b_seg, preferred_element_type=F32)
        decayed = jnp.concatenate(
            [st[h * SSD_HEAD_DIM:(h + 1) * SSD_HEAD_DIM, :] * d_a(s, h) for h in range(GROUP_HEADS)], axis=0)
        new_states.append(decayed + upd)
    y_off = jnp.concatenate(y_offs, axis=0) if n_seg > 1 else y_offs[0]

    lane = lax.broadcasted_iota(jnp.int32, (T, LANES), 1)
    lo_half = lane < SSD_HEAD_DIM
    ys = []
    for q in range(GROUP_HEADS // 2):
        v = cs_x[:, q * LANES:(q + 1) * LANES]
        r = pltpu.roll(v, SSD_HEAD_DIM, 1)
        cols = (jnp.where(lo_half, v, r), jnp.where(lo_half, r, v))
        ws = []
        for e in range(2):
            diff = cols[e] - cs_row(2 * q + e)
            ws.append((cb * jnp.exp(jnp.where(keep, diff, -jnp.inf))).astype(BF16))
        xp = xdt_b[:, q * LANES:(q + 1) * LANES]
        zero = jnp.zeros_like(xp)
        xpair = jnp.concatenate([jnp.where(lo_half, xp, zero), jnp.where(lo_half, zero, xp)], axis=0)
        ys.append(jnp.dot(jnp.concatenate(ws, axis=1), xpair, preferred_element_type=F32))
    y_diag = jnp.concatenate(ys, axis=1)

    y = y_diag + y_off * ecs_x + dsk * xs
    y = y * _silu(z)
    yn = y * lax.rsqrt(jnp.mean(y * y, axis=-1, keepdims=True) + NORM_EPS) * nw
    return yn.astype(BF16), new_states


def _causal_keep(seq_len):
    r = lax.broadcasted_iota(jnp.int32, (T, T), 0)
    c = lax.broadcasted_iota(jnp.int32, (T, T), 1)
    keep = c <= r
    if seq_len < T:
        keep = jnp.logical_and(keep, (r // seq_len) == (c // seq_len))
    return keep


def _ssd_prompt_body(z_ref, xs_ref, bc_ref, dt_ref, cw_ref, cbias_ref, dtb_ref, alog_ref, dsk_ref, nw_ref,
                     m_ref, e_ref, yn_ref, st_ref,
                     cbuf_ref, xc_ref, bcs_ref, dt3_ref, cs3_ref, cst_ref, *, c):
    n_x = xs_ref.shape[0]
    pad = SUBLANES
    taps = SSD_CONV - 1

    @pl.when(c == 0)
    def _():
        st_ref[...] = jnp.zeros_like(st_ref)
        cbuf_ref[:, 0:pad, :] = jnp.zeros((cbuf_ref.shape[0], pad, GROUP_W), F32)

    def conv(k, src):
        w = cw_ref[k]
        cbuf_ref[k, pad:pad + T, :] = src
        acc = cbias_ref[k] + src * w[taps:taps + 1, :]
        for kk in range(taps):
            acc = acc + cbuf_ref[k, pad - taps + kk:pad - taps + kk + T, :] * w[kk:kk + 1, :]
        cbuf_ref[k, pad - taps:pad, :] = cbuf_ref[k, pad + T - taps:pad + T, :]
        return _silu(acc)

    def conv_x(k, carry):
        xc_ref[k] = conv(k, xs_ref[k])
        return carry
    lax.fori_loop(0, n_x, conv_x, 0)

    def conv_bc(k, carry):
        out = conv(n_x + k, bc_ref[k])
        for gg in range(BC_PER_BLOCK):
            bcs_ref[k * BC_PER_BLOCK + gg] = out[:, gg * SSD_STATE:(gg + 1) * SSD_STATE]
        return carry
    lax.fori_loop(0, bc_ref.shape[0], conv_bc, 0)

    dt = jax.nn.softplus(dt_ref[...] + dtb_ref[...])
    a = dt * (-jnp.exp(alog_ref[...]))
    cs = _dot_exact_rhs(m_ref[...], _split3(a))
    cst_ref[...] = cs.T
    for p, (dt_p, cs_p) in enumerate(zip(_split3(dt), _split3(cs))):
        dt3_ref[p] = dt_p
        cs3_ref[p] = cs_p

    keep = _causal_keep(T)

    def spread(g):
        e_g = e_ref[g]
        return (_dot_exact_lhs([dt3_ref[p] for p in range(3)], e_g), _dot_exact_lhs([cs3_ref[p] for p in range(3)], e_g))

    def group(g, dt_x, cs_x):
        h0 = g * GROUP_HEADS
        rows = pl.ds(g * GROUP_W, GROUP_W)
        yn, new_states = _ssd_group(
            xc_ref[g], bcs_ref[g], bcs_ref[SSD_GROUPS + g], z_ref[g], dt_x, cs_x, cs_x[T - 1:T, :],
            lambda h: cst_ref[pl.ds(h0 + h, 1), :],
            lambda s, h: jnp.exp(cst_ref[pl.ds(h0 + h, 1), T - 1:T]),
            [st_ref[rows, :]], 1, dsk_ref[g], nw_ref[g], keep)
        yn_ref[g] = yn
        st_ref[rows, :] = new_states[0]

    spread_next = spread(0)
    for g in range(SSD_GROUPS):
        dt_x, cs_x = spread_next
        if g + 1 < SSD_GROUPS:
            spread_next = spread(g + 1)
        group(g, dt_x, cs_x)


def _ssd_sample_body(z_ref, xs_ref, b_ref, c_ref, px_ref, pb_ref, pc_ref, dt_ref, wx_ref, wb_ref, wc_ref,
                     bx_ref, bb_ref, bc_ref, dtb_ref, alog_ref, dsk_ref, nw_ref, m_ref, mseg_ref, e_ref,
                     st_in_ref, yn_ref, st_ref, cst_ref, cet_ref, *, seq_len, g):
    n_seg = T // seq_len

    def conv(x_ref, p_ref, w_ref, bias_ref):
        x = x_ref[...]
        width = x.shape[-1]
        taps = SSD_CONV - 1
        prev = [jnp.broadcast_to(p_ref[k][:, None, :], (n_seg, seq_len, width)).reshape(T, width) for k in range(taps)]
        tpos = lax.broadcasted_iota(jnp.int32, x.shape, 0) % seq_len
        acc = bias_ref[...] + x * w_ref[taps:SSD_CONV, :]
        for sh in range(1, SSD_CONV):
            hist = prev[taps - 1]
            for t in range(sh - 2, -1, -1):
                hist = jnp.where(tpos == t, prev[taps - sh + t], hist)
            shifted = jnp.where(tpos >= sh, pltpu.roll(x, sh, 0), hist)
            acc = acc + shifted * w_ref[taps - sh:SSD_CONV - sh, :]
        return _silu(acc)

    xs = conv(xs_ref, px_ref, wx_ref, bx_ref)
    bm = conv(b_ref, pb_ref, wb_ref, bb_ref)
    cm = conv(c_ref, pc_ref, wc_ref, bc_ref)

    dt = jax.nn.softplus(dt_ref[...] + dtb_ref[...])
    a3 = _split3(dt * (-jnp.exp(alog_ref[...])))
    cs = _dot_exact_rhs(m_ref[...], a3)
    cs_end = _dot_exact_rhs(mseg_ref[...], a3)
    cst_ref[...] = cs.T
    cet_ref[...] = cs_end.T
    e_g = e_ref[...]
    h0 = g * GROUP_HEADS
    yn, new_states = _ssd_group(
        xs, bm, cm, z_ref[...], _dot_exact_lhs(_split3(dt), e_g), _dot_exact_lhs(_split3(cs), e_g),
        _dot_exact_lhs(_split3(cs_end), e_g),
        lambda h: cst_ref[pl.ds(h0 + h, 1), :],
        lambda s, h: jnp.exp(cet_ref[pl.ds(h0 + h, 1), s * seq_len:s * seq_len + 1]),
        [st_in_ref[s] for s in range(n_seg)], n_seg, dsk_ref[...], nw_ref[...], _causal_keep(seq_len))
    yn_ref[...] = yn
    for s in range(n_seg):
        st_ref[s] = new_states[s]


def _ssd_masks(seq_len):
    r = jnp.arange(T)
    same = (r[:, None] // seq_len) == (r[None, :] // seq_len)
    return jnp.logical_and(same, r[None, :] <= r[:, None]).astype(BF16), same.astype(BF16)


N_PROMPT_IN, N_SAMPLE_IN = 12, 22


def _ssd_scan_body(*refs, n_chunks, seq_len_s):
    s = pl.program_id(0)
    p_in = refs[:N_PROMPT_IN]
    s_in = refs[N_PROMPT_IN:N_PROMPT_IN + N_SAMPLE_IN]
    yn_p, st_p, yn_s, st_s = refs[N_PROMPT_IN + N_SAMPLE_IN:N_PROMPT_IN + N_SAMPLE_IN + 4]
    scratch = refs[N_PROMPT_IN + N_SAMPLE_IN + 4:]
    _ssd_prompt_body(*p_in, yn_p, st_p, *scratch[:6], c=s % n_chunks)
    _ssd_sample_body(*s_in, yn_s, st_s, *scratch[6:], seq_len=seq_len_s, g=s % SSD_GROUPS)


def _ssd_scan(zx, dt_raw, conv_w, conv_b, dtb, alog, dsk, nw, state_s, conv_hist_s, n_pseq, seq_len_p, seq_len_s):
    heads = dt_raw.shape[1]
    inner = SSD_GROUPS * GROUP_W
    n_chunks = seq_len_p // T
    m_p = n_pseq * seq_len_p
    n_seg = T // seq_len_s
    n_sseq = state_s.shape[0]
    row0 = m_p // T
    n_steps = n_pseq * n_chunks
    assert n_steps == (n_sseq // n_seg) * SSD_GROUPS
    n_bc = 2 * SSD_GROUPS // BC_PER_BLOCK
    n_cblk = SSD_GROUPS + n_bc
    cw = conv_w.reshape(SSD_CONV, n_cblk, GROUP_W).transpose(1, 0, 2)
    cbias = conv_b.reshape(n_cblk, 1, GROUP_W)
    e_all = (jnp.arange(inner)[None, :] // SSD_HEAD_DIM == jnp.arange(heads)[:, None]).astype(BF16)
    e_grp = e_all.reshape(heads, SSD_GROUPS, GROUP_W).transpose(1, 0, 2)
    m_tril, _ = _ssd_masks(T)
    m_s, mseg_s = _ssd_masks(seq_len_s)
    xoff = SSD_GROUPS
    boff = 2 * SSD_GROUPS
    coff = boff + SSD_GROUPS // BC_PER_BLOCK
    cb0 = inner // SSD_STATE
    cc0 = cb0 + SSD_GROUPS

    full = lambda shape: pl.BlockSpec(shape, lambda s: (0,) * len(shape))
    bp = lambda s: s // n_chunks
    ts = lambda s: s // SSD_GROUPS
    gs = lambda s: s % SSD_GROUPS
    prompt_in = [
        pl.BlockSpec((SSD_GROUPS, T, GROUP_W), lambda s: (0, s, 0)),
        pl.BlockSpec((SSD_GROUPS, T, GROUP_W), lambda s: (1, s, 0)),
        pl.BlockSpec((n_bc, T, GROUP_W), lambda s: (2 * SSD_GROUPS // n_bc, s, 0)),
        pl.BlockSpec((T, heads), lambda s: (s, 0)),
        full((n_cblk, SSD_CONV, GROUP_W)), full((n_cblk, 1, GROUP_W)), full((1, heads)), full((1, heads)),
        full((SSD_GROUPS, 1, GROUP_W)), full((SSD_GROUPS, 1, GROUP_W)), full((T, T)),
        full((SSD_GROUPS, heads, GROUP_W)),
    ]
    per_g = lambda shape: pl.BlockSpec(shape, lambda s: (gs(s),) + (0,) * (len(shape) - 1))
    sample_in = [
        pl.BlockSpec((None, T, GROUP_W), lambda s: (gs(s), row0 + ts(s), 0)),
        pl.BlockSpec((None, T, GROUP_W), lambda s: (xoff + gs(s), row0 + ts(s), 0)),
        pl.BlockSpec((None, T, SSD_STATE),
                     lambda s: (boff + gs(s) // BC_PER_BLOCK, row0 + ts(s), gs(s) % BC_PER_BLOCK)),
        pl.BlockSpec((None, T, SSD_STATE),
                     lambda s: (coff + gs(s) // BC_PER_BLOCK, row0 + ts(s), gs(s) % BC_PER_BLOCK)),
        pl.BlockSpec((SSD_CONV - 1, n_seg, GROUP_W), lambda s: (0, ts(s), gs(s))),
        pl.BlockSpec((SSD_CONV - 1, n_seg, SSD_STATE), lambda s: (0, ts(s), cb0 + gs(s))),
        pl.BlockSpec((SSD_CONV - 1, n_seg, SSD_STATE), lambda s: (0, ts(s), cc0 + gs(s))),
        pl.BlockSpec((T, heads), lambda s: (row0 + ts(s), 0)),
        pl.BlockSpec((SSD_CONV, GROUP_W), lambda s: (0, gs(s))),
        pl.BlockSpec((SSD_CONV, SSD_STATE), lambda s: (0, cb0 + gs(s))),
        pl.BlockSpec((SSD_CONV, SSD_STATE), lambda s: (0, cc0 + gs(s))),
        pl.BlockSpec((1, GROUP_W), lambda s: (0, gs(s))),
        pl.BlockSpec((1, SSD_STATE), lambda s: (0, cb0 + gs(s))),
        pl.BlockSpec((1, SSD_STATE), lambda s: (0, cc0 + gs(s))),
        full((1, heads)), full((1, heads)),
        per_g((None, 1, GROUP_W)), per_g((None, 1, GROUP_W)),
        full((T, T)), full((T, T)),
        pl.BlockSpec((heads, GROUP_W), lambda s: (0, gs(s))),
        pl.BlockSpec((n_seg, GROUP_W, SSD_STATE), lambda s: (ts(s), gs(s), 0)),
    ]
    assert len(prompt_in) == N_PROMPT_IN and len(sample_in) == N_SAMPLE_IN
    return pl.pallas_call(
        functools.partial(_ssd_scan_body, n_chunks=n_chunks, seq_len_s=seq_len_s),
        grid=(n_steps,),
        in_specs=prompt_in + sample_in,
        out_specs=[pl.BlockSpec((SSD_GROUPS, T, GROUP_W), lambda s: (0, s, 0)),
                   pl.BlockSpec((None, inner, SSD_STATE), lambda s: (bp(s), 0, 0)),
                   pl.BlockSpec((None, T, GROUP_W), lambda s: (gs(s), ts(s), 0)),
                   pl.BlockSpec((n_seg, GROUP_W, SSD_STATE), lambda s: (ts(s), gs(s), 0))],
        out_shape=[jax.ShapeDtypeStruct((SSD_GROUPS, m_p, GROUP_W), BF16),
                   jax.ShapeDtypeStruct((n_pseq, inner, SSD_STATE), F32),
                   jax.ShapeDtypeStruct((SSD_GROUPS, n_sseq * seq_len_s, GROUP_W), BF16),
                   jax.ShapeDtypeStruct(state_s.shape, F32)],
        scratch_shapes=[pltpu.VMEM((n_cblk, SUBLANES + T, GROUP_W), F32),
                        pltpu.VMEM((SSD_GROUPS, T, GROUP_W), F32),
                        pltpu.VMEM((2 * SSD_GROUPS, T, SSD_STATE), F32),
                        pltpu.VMEM((3, T, heads), BF16),
                        pltpu.VMEM((3, T, heads), BF16),
                        pltpu.VMEM((heads, T), F32),
                        pltpu.VMEM((heads, T), F32), pltpu.VMEM((heads, T), F32)],
        compiler_params=_cparams(("arbitrary",)),
        name="ssd_scan",
    )(zx, zx, zx, dt_raw, cw, cbias, dtb, alog, dsk, nw, m_tril, e_grp,
      zx, zx, zx, zx, conv_hist_s, conv_hist_s, conv_hist_s, dt_raw, conv_w, conv_w, conv_w, conv_b, conv_b, conv_b,
      dtb, alog, dsk, nw, m_s, mseg_s, e_all, state_s)


def kernel(x_prompt, x_sample, c_prompt, c_sample, state_ssm, state_conv, mod_w, mod_b, norm_mix_w, norm_ffn_w,
           a_w_in, a_b_in, a_ln_w, a_ln_b, a_w_s, a_b_s, a_w_out,
           b_w_in, b_conv_w, b_conv_b, b_dt_bias, b_a_log, b_d, b_norm_w, b_w_out,
           f_w_in, f_w_out, final_norm_w):
    bp, lp, d = x_prompt.shape
    bs, ls, _ = x_sample.shape
    depth = mod_w.shape[0]
    n_prompt = bp * lp
    n_sample = bs * ls
    m_all = n_prompt + n_sample
    assert lp % TM == 0 and n_sample % TM == 0 and TM % ls == 0 and bp <= SUBLANES
    assert ls >= SSD_CONV - 1 and T % ls == 0 and lp % T == 0 and GMLP_CHUNK % ls == 0 and lp % MIX_ROWS == 0
    assert depth == 2 and a_w_in.shape[0] == 1 and b_w_in.shape[0] == 1
    cfg = Cfg(n_ptiles=n_prompt // TM, tiles_per_seq=lp // TM, seq_len_s=ls, srow0=SUBLANES)

    x_p = x_prompt.reshape(n_prompt, d)
    x_s = x_sample.reshape(n_sample, d)
    c_all = jnp.concatenate([c_prompt, jnp.zeros((SUBLANES - bp, d), F32), c_sample], axis=0)
    mods = _mod_table(c_all, mod_w, mod_b)
    gate_m, gate_f = 2, 5

    u, v = _gmlp_in(x_p, x_s, mods, 0, norm_mix_w, a_w_in, a_b_in, 0, cfg)
    r = jnp.arange(GMLP_CHUNK)
    tril = r[None, :] <= r[:, None]
    mask = jnp.stack([tril, jnp.logical_and(tril, (r[:, None] // ls) == (r[None, :] // ls))]).astype(F32)
    rep = GMLP_CHUNK // ls
    wmix = jnp.stack([a_w_s[0], jnp.tile(a_w_s[0, :, :ls, :ls], (1, rep, rep))])
    width = a_w_in.shape[-1] // 2
    bias = jnp.stack([jnp.repeat(a_b_s[0].T, width // GMLP_GROUPS, axis=1),
                      jnp.repeat(jnp.tile(a_b_s[0, :, :ls].T, (rep, 1)), width // GMLP_GROUPS, axis=1)])
    gated, v_p, v_s = _gmlp_mix(u, v, a_ln_w[0], a_ln_b[0], wmix, mask, bias, n_prompt, lp, bp)
    x = _resid_matmul(gated, a_w_out, 0, x_p, x_s, mods, gate_m, cfg, "gmlp_out")
    act = _ffn_in(x, mods, 0, norm_ffn_w, f_w_in, cfg)
    x = _resid_matmul(act, f_w_out, 0, x, None, mods, gate_f, cfg, "ffn_out0")

    inner = b_w_out.shape[1]
    conv_dim = b_conv_w.shape[-1]
    n_main = inner + conv_dim
    heads = inner // SSD_HEAD_DIM
    zx, dt_raw = _ssd_in(x, mods, 1, norm_mix_w, jnp.swapaxes(b_w_in, 1, 2)[0], n_main, cfg)
    conv_hist = jnp.swapaxes(state_conv[0], 0, 1)
    ssd_params = (b_conv_w[0], b_conv_b, b_dt_bias, b_a_log,
                  jnp.repeat(b_d[0], SSD_HEAD_DIM).reshape(SSD_GROUPS, 1, GROUP_W),
                  b_norm_w.reshape(SSD_GROUPS, 1, GROUP_W))
    state_s = state_ssm[0].reshape(bs, inner, SSD_STATE)
    yn_p, ssm_p, yn_s, ssm_s = _ssd_scan(zx, dt_raw, *ssd_params, state_s, conv_hist, bp, lp, ls)
    x = _resid_matmul(yn_p, b_w_out, 0, x, None, mods, MOD_CHUNKS + gate_m, cfg, "ssd_out", a_s=yn_s)
    act = _ffn_in(x, mods, 1, norm_ffn_w, f_w_in, cfg)
    y_p, y_s = _resid_final(act, f_w_out, 1, x, mods, MOD_CHUNKS + gate_f, final_norm_w, n_prompt, cfg)

    zx4 = zx.reshape(zx.shape[0], m_all // ls, ls, TN)

    def tails(groups):
        t = groups[SSD_GROUPS:, :, ls - (SSD_CONV - 1):, :]
        return jnp.moveaxis(t, 0, 2).reshape(t.shape[1], SSD_CONV - 1, conv_dim)
    conv_p = tails(zx4[:, lp // ls - 1:n_prompt // ls:lp // ls])
    conv_s = tails(zx4[:, n_prompt // ls:])
    return (y_p.reshape(bp, lp, d), y_s.reshape(bs, ls, d),
            v_p.reshape(1, bp, GMLP_CHUNK, width), v_s.reshape(1, bs, ls, width),
            ssm_p.reshape(1, bp, heads, SSD_HEAD_DIM, SSD_STATE), ssm_s.reshape(1, bs, heads, SSD_HEAD_DIM, SSD_STATE),
            conv_p[None], conv_s[None])
```
